```python
import math
import jax, jax.numpy as jnp
from jax import lax
import numpy as np

D_MODEL = 1024
BATCH = 1
SEQ = 16384
DEPTH = 1
DEC_BATCH = 8
DEC_SEQ = 64
PAST_LEN = 1024

CHUNK = 64
D_RNN = D_MODEL
CONV_W = 4
LRU_BLOCKS = 16
LRU_BLOCK_W = D_RNN // LRU_BLOCKS
LRU_C = 8.0
N_HEADS = 16
N_KV = 4
HEAD_DIM = 64
GROUP = N_HEADS // N_KV
WINDOW = 128
LOOKBACK_CHUNKS = -(-WINDOW // CHUNK)
Q_W = N_HEADS * HEAD_DIM
KV_W = N_KV * HEAD_DIM
REL_BUCKETS = 32
REL_MAX_DIST = 128
N_GROUPS = 4
EXPERTS_PER_GROUP = 8
N_EXPERTS = N_GROUPS * EXPERTS_PER_GROUP
TOP_K = 2
D_EXPERT = 256
EPS = 1e-6
NEG_INF = -1e30
IN_SPLITS = (D_RNN, 2 * D_RNN, 2 * D_RNN + Q_W, 2 * D_RNN + Q_W + KV_W, 2 * D_RNN + Q_W + 2 * KV_W)
IN_W = 2 * D_RNN + Q_W + 2 * KV_W + 2 * D_MODEL

kernel_name = 'hybrid_rglru_swa_hmoe_stream_step'


def _rmsnorm(x, g):
    xf = x.astype(jnp.float32)
    y = xf * lax.rsqrt(jnp.mean(xf * xf, axis=-1, keepdims=True) + EPS)
    return y.astype(x.dtype) * g


def _lin_combine(left, right):
    a1, b1 = left
    a2, b2 = right
    return a1 * a2, a2 * b1 + b2


def _rglru_branch(xb, gate_in, conv_buf, h0, conv_w, conv_b, w_rg_a, b_rg_a, w_rg_x, b_rg_x, lru_lambda):
    B, T, _ = xb.shape
    xp = jnp.concatenate([conv_buf, xb], axis=1)
    xc = conv_b + sum(xp[:, j:j + T] * conv_w[j] for j in range(CONV_W))
    new_buf = xp[:, -(CONV_W - 1):]
    xh = xc.reshape(B, T, LRU_BLOCKS, LRU_BLOCK_W)
    r = jax.nn.sigmoid((jnp.einsum('btnd,nde->btne', xh, w_rg_a).reshape(B, T, D_RNN) + b_rg_a).astype(jnp.float32))
    i = jax.nn.sigmoid((jnp.einsum('btnd,nde->btne', xh, w_rg_x).reshape(B, T, D_RNN) + b_rg_x).astype(jnp.float32))
    log_a = LRU_C * r * jax.nn.log_sigmoid(lru_lambda.astype(jnp.float32))
    a = jnp.exp(log_a)
    mult = jnp.sqrt(-jnp.expm1(2.0 * log_a))
    b = mult * i * xc.astype(jnp.float32)
    b = b.at[:, 0].add(a[:, 0] * h0.astype(jnp.float32))
    _, h = lax.associative_scan(_lin_combine, (a, b), axis=1)
    y = h.astype(xb.dtype) * jax.nn.gelu(gate_in)
    return y, new_buf, h[:, -1].astype(xb.dtype)


def _rel_bucket(rel):
    nb = REL_BUCKETS // 2
    n = -rel
    ret = jnp.where(n < 0, nb, 0)
    n = jnp.abs(n)
    max_exact = nb // 2
    nf = jnp.maximum(n, 1).astype(jnp.float32)
    large = max_exact + (jnp.log(nf / max_exact) / math.log(REL_MAX_DIST / max_exact) * (nb - max_exact)).astype(jnp.int32)
    large = jnp.minimum(large, nb - 1)
    return ret + jnp.where(n < max_exact, n, large)


def _rel_bias(rel, rel_table):
    return jnp.transpose(rel_table[_rel_bucket(rel)], (2, 0, 1))


def _sink_attention(q, k, v, bias, mask, sink):
    Lq, Lk = q.shape[2], k.shape[2]
    s = jnp.einsum('bnqkgd,bnskd->bnkgqs', q, k).astype(jnp.float32) * (HEAD_DIM ** -0.5)
    s = s + bias.astype(jnp.float32).reshape(N_KV, GROUP, Lq, Lk)
    if mask is not None:
        s = jnp.where(mask, s, NEG_INF)
    sink_col = jnp.broadcast_to(sink.astype(jnp.float32).reshape(N_KV, GROUP, 1, 1), s.shape[:-1] + (1,))
    p = jax.nn.softmax(jnp.concatenate([s, sink_col], axis=-1), axis=-1)[..., :-1]
    return jnp.einsum('bnkgqs,bnskd->bnqkgd', p.astype(v.dtype), v)


def _swa_prompt(q, k, v, attn_sink, rel_table):
    B, T = q.shape[:2]
    NC = T // CHUNK
    Lk = (LOOKBACK_CHUNKS + 1) * CHUNK

    def band(a):
        a = a.reshape(B, NC, CHUNK, N_KV, HEAD_DIM)
        a = jnp.pad(a, ((0, 0), (LOOKBACK_CHUNKS, 0), (0, 0), (0, 0), (0, 0)))
        return jnp.concatenate([a[:, j:j + NC] for j in range(LOOKBACK_CHUNKS + 1)], axis=2)

    key_off = jnp.arange(Lk) - LOOKBACK_CHUNKS * CHUNK
    rel = key_off[None, :] - jnp.arange(CHUNK)[:, None]
    bias = _rel_bias(rel, rel_table)
    key_pos = jnp.arange(NC)[:, None] * CHUNK + key_off[None, :]
    mask = (key_pos >= 0)[None, :, None, None, None, :]
    out = _sink_attention(q.reshape(B, NC, CHUNK, N_KV, GROUP, HEAD_DIM), band(k), band(v), bias, mask, attn_sink)
    rows = min(WINDOW, T)
    return out.reshape(B, T, Q_W), k[:, -rows:], v[:, -rows:]


def _swa_step(q, k, v, k_past, v_past, attn_sink, rel_table):
    B, S = q.shape[:2]
    W = k_past.shape[1]
    k_all = jnp.concatenate([k_past, k], axis=1)
    v_all = jnp.concatenate([v_past, v], axis=1)
    key_off = jnp.concatenate([jnp.arange(W) - W, jnp.arange(S)])
    rel = key_off[None, :] - jnp.arange(S)[:, None]
    bias = _rel_bias(rel, rel_table)
    out = _sink_attention(q[:, None], k_all[:, None], v_all[:, None], bias, None, attn_sink)
    return out.reshape(B, S, Q_W), k_all[:, -W:], v_all[:, -W:]


def _hier_moe(x, w_group, b_group, w_router, b_router, w_e_gate, w_e_up, w_e_down):
    xt = x.reshape(-1, D_MODEL)
    g_logits = (xt @ w_group + b_group).astype(jnp.float32)
    g_prob = jax.nn.softmax(g_logits, axis=-1)
    _, g_idx = lax.top_k(g_logits, 1)
    g_onehot = jax.nn.one_hot(g_idx[:, 0], N_GROUPS, dtype=jnp.float32)
    g_w = jnp.sum(g_prob * g_onehot, axis=-1, keepdims=True)
    e_logits = (xt @ w_router + b_router).astype(jnp.float32).reshape(-1, N_GROUPS, EXPERTS_PER_GROUP)
    e_sel = jnp.sum(e_logits * g_onehot[:, :, None], axis=1)
    e_top, e_idx = lax.top_k(e_sel, TOP_K)
    e_w = jax.nn.softmax(e_top, axis=-1) * g_w
    expert_id = g_idx * EXPERTS_PER_GROUP + e_idx
    combine = jnp.sum(jax.nn.one_hot(expert_id, N_EXPERTS, dtype=jnp.float32) * e_w[..., None], axis=1)
    out = jnp.zeros(xt.shape, jnp.float32)
    for e in range(N_EXPERTS):
        h = jax.nn.silu(xt @ w_e_gate[e]) * (xt @ w_e_up[e])
        out = out + combine[:, e:e + 1] * (h @ w_e_down[e]).astype(jnp.float32)
    return out.astype(x.dtype).reshape(x.shape)


def _layer(x, conv_buf, h0, k_past, v_past, is_prompt, rel_table, norm_mix, w_in, b_merge, conv_w, conv_b,
           w_rg_a, b_rg_a, w_rg_x, b_rg_x, lru_lambda, attn_sink, w_lru_proj, w_attn_proj, w_out,
           norm_ffn, w_group, b_group, w_router, b_router, w_e_gate, w_e_up, w_e_down):
    B, T, _ = x.shape
    xn = _rmsnorm(x, norm_mix)
    proj = xn @ w_in
    lru_x, lru_gate, q, k, v, gates = jnp.split(proj, IN_SPLITS, axis=-1)
    g = jax.nn.sigmoid((gates + b_merge).astype(jnp.float32)).astype(x.dtype)
    g_lru, g_attn = g[..., :D_MODEL], g[..., D_MODEL:]
    lru_out, new_conv, h_last = _rglru_branch(lru_x, lru_gate, conv_buf, h0, conv_w, conv_b,
                                              w_rg_a, b_rg_a, w_rg_x, b_rg_x, lru_lambda)
    q = q.reshape(B, T, N_KV, GROUP, HEAD_DIM)
    k = k.reshape(B, T, N_KV, HEAD_DIM)
    v = v.reshape(B, T, N_KV, HEAD_DIM)
    if is_prompt:
        attn_out, new_k, new_v = _swa_prompt(q, k, v, attn_sink, rel_table)
    else:
        attn_out, new_k, new_v = _swa_step(q, k, v, k_past, v_past, attn_sink, rel_table)
    mixed = g_lru * (lru_out @ w_lru_proj) + g_attn * (attn_out @ w_attn_proj)
    x = x + mixed @ w_out
    x = x + _hier_moe(_rmsnorm(x, norm_ffn), w_group, b_group, w_router, b_router, w_e_gate, w_e_up, w_e_down)
    return x, new_conv, h_last, new_k, new_v


def setup_inputs(seed: int = 0) -> dict:
    key = jax.random.key(seed)
    ks = jax.random.split(key, 32)
    f32 = jnp.float32

    def nrm(k, shape, scale):
        return jax.random.normal(k, shape, f32) * scale

    win_rows = min(WINDOW, PAST_LEN)
    u = jax.random.uniform(ks[14], (DEPTH, D_RNN), f32, 0.9, 0.999)
    s = u ** (1.0 / LRU_C)
    lru_lambda = jnp.log(s) - jnp.log1p(-s)
    return {
        'x_prompt': nrm(ks[0], (BATCH, SEQ, D_MODEL), 1.0),
        'x_sample': nrm(ks[1], (DEC_BATCH, DEC_SEQ, D_MODEL), 1.0),
        'state_lru_h': nrm(ks[2], (DEPTH, DEC_BATCH, D_RNN), 0.5),
        'state_lru_conv': nrm(ks[3], (DEPTH, DEC_BATCH, CONV_W - 1, D_RNN), 1.0),
        'cache_swa_k': nrm(ks[4], (DEPTH, DEC_BATCH, win_rows, N_KV, HEAD_DIM), 1.0),
        'cache_swa_v': nrm(ks[5], (DEPTH, DEC_BATCH, win_rows, N_KV, HEAD_DIM), 1.0),
        'norm_mix': 1.0 + nrm(ks[6], (DEPTH, D_MODEL), 0.05),
        'w_in': nrm(ks[7], (DEPTH, D_MODEL, IN_W), D_MODEL ** -0.5),
        'b_merge': nrm(ks[8], (DEPTH, 2 * D_MODEL), 0.1),
        'conv_w': nrm(ks[9], (DEPTH, CONV_W, D_RNN), CONV_W ** -0.5),
        'conv_b': nrm(ks[10], (DEPTH, D_RNN), 0.05),
        'w_rg_a': nrm(ks[11], (DEPTH, LRU_BLOCKS, LRU_BLOCK_W, LRU_BLOCK_W), LRU_BLOCK_W ** -0.5),
        'b_rg_a': nrm(ks[12], (DEPTH, D_RNN), 0.1),
        'w_rg_x': nrm(ks[13], (DEPTH, LRU_BLOCKS, LRU_BLOCK_W, LRU_BLOCK_W), LRU_BLOCK_W ** -0.5),
        'b_rg_x': nrm(ks[15], (DEPTH, D_RNN), 0.1),
        'lru_lambda': lru_lambda,
        'attn_sink': nrm(ks[16], (DEPTH, N_HEADS), 0.5),
        'rel_bias': nrm(ks[17], (REL_BUCKETS, N_HEADS), 0.5),
        'w_lru_proj': nrm(ks[18], (DEPTH, D_RNN, D_MODEL), D_RNN ** -0.5),
        'w_attn_proj': nrm(ks[19], (DEPTH, Q_W, D_MODEL), Q_W ** -0.5),
        'w_out': nrm(ks[20], (DEPTH, D_MODEL, D_MODEL), D_MODEL ** -0.5),
        'norm_ffn': 1.0 + nrm(ks[21], (DEPTH, D_MODEL), 0.05),
        'w_group': nrm(ks[22], (DEPTH, D_MODEL, N_GROUPS), D_MODEL ** -0.5),
        'b_group': nrm(ks[23], (DEPTH, N_GROUPS), 0.01),
        'w_router': nrm(ks[24], (DEPTH, D_MODEL, N_EXPERTS), D_MODEL ** -0.5),
        'b_router': nrm(ks[25], (DEPTH, N_EXPERTS), 0.01),
        'w_e_gate': nrm(ks[26], (DEPTH, N_EXPERTS, D_MODEL, D_EXPERT), D_MODEL ** -0.5),
        'w_e_up': nrm(ks[27], (DEPTH, N_EXPERTS, D_MODEL, D_EXPERT), D_MODEL ** -0.5),
        'w_e_down': nrm(ks[28], (DEPTH, N_EXPERTS, D_EXPERT, D_MODEL), D_EXPERT ** -0.5),
        'norm_final': 1.0 + nrm(ks[29], (D_MODEL,), 0.05),
    }


def reference(x_prompt, x_sample, state_lru_h, state_lru_conv, cache_swa_k, cache_swa_v,
              norm_mix, w_in, b_merge, conv_w, conv_b, w_rg_a, b_rg_a, w_rg_x, b_rg_x, lru_lambda,
              attn_sink, rel_bias, w_lru_proj, w_attn_proj, w_out, norm_ffn, w_group, b_group,
              w_router, b_router, w_e_gate, w_e_up, w_e_down, norm_final):
    B = x_prompt.shape[0]
    zero_buf = jnp.zeros((B, CONV_W - 1, D_RNN), x_prompt.dtype)
    zero_h = jnp.zeros((B, D_RNN), x_prompt.dtype)
    y_p, y_s = x_prompt, x_sample
    p_h, p_c, p_k, p_v, s_h, s_c, s_k, s_v = [], [], [], [], [], [], [], []
    for l in range(DEPTH):
        lw = (norm_mix[l], w_in[l], b_merge[l], conv_w[l], conv_b[l], w_rg_a[l], b_rg_a[l], w_rg_x[l],
              b_rg_x[l], lru_lambda[l], attn_sink[l], w_lru_proj[l], w_attn_proj[l], w_out[l], norm_ffn[l],
              w_group[l], b_group[l], w_router[l], b_router[l], w_e_gate[l], w_e_up[l], w_e_down[l])
        y_p, c1, h1, k1, v1 = _layer(y_p, zero_buf, zero_h, None, None, True, rel_bias, *lw)
        y_s, c2, h2, k2, v2 = _layer(y_s, state_lru_conv[l], state_lru_h[l], cache_swa_k[l], cache_swa_v[l],
                                     False, rel_bias, *lw)
        p_h.append(h1); p_c.append(c1); p_k.append(k1); p_v.append(v1)
        s_h.append(h2); s_c.append(c2); s_k.append(k2); s_v.append(v2)
    y_prompt = _rmsnorm(y_p, norm_final)
    y_sample = _rmsnorm(y_s, norm_final)
    p_lru_h = jnp.stack(p_h)
    p_lru_conv = jnp.stack(p_c)
    p_swa_k = jnp.stack(p_k)
    p_swa_v = jnp.stack(p_v)
    s_lru_h = jnp.stack(s_h)
    s_lru_conv = jnp.stack(s_c)
    s_swa_k = jnp.stack(s_k)
    s_swa_v = jnp.stack(s_v)
    return (y_prompt, y_sample, p_lru_h, p_lru_conv, p_swa_k, p_swa_v, s_lru_h, s_lru_conv, s_swa_k, s_swa_v)
```

```python
import functools
import math

import jax
import jax.numpy as jnp
from jax import lax
from jax.experimental import pallas as pl
from jax.experimental.pallas import tpu as pltpu

D_MODEL = 1024
CHUNK = 64
D_RNN = D_MODEL
CONV_W = 4
LRU_BLOCKS = 16
LRU_BLOCK_W = D_RNN // LRU_BLOCKS
LRU_C = 8.0
N_HEADS = 16
N_KV = 4
HEAD_DIM = 64
GROUP = N_HEADS // N_KV
WINDOW = 128
LOOKBACK_CHUNKS = -(-WINDOW // CHUNK)
HIST = LOOKBACK_CHUNKS * CHUNK
BAND = HIST + CHUNK
Q_W = N_HEADS * HEAD_DIM
KV_W = N_KV * HEAD_DIM
REL_BUCKETS = 32
REL_MAX_DIST = 128
N_GROUPS = 4
EXPERTS_PER_GROUP = 8
N_EXPERTS = N_GROUPS * EXPERTS_PER_GROUP
D_EXPERT = 256
EPS = 1e-6
NEG_INF = -1e30

LANES = 128
SUBLANES = 8
KV2_W = N_KV * LANES
RG_GROUP = 256
N_RG_GROUPS = D_RNN // RG_GROUP
ROUTER_W = LANES
VMEM_LIMIT = 56 * 1024 * 1024

C_LX, C_LG, C_Q = 0, D_RNN, 2 * D_RNN
C_K = C_Q + Q_W
C_V = C_K + KV2_W
C_G = C_V + KV2_W
IN_W2 = C_G + 2 * D_MODEL


def _rmsnorm(x, g):
    return x * lax.rsqrt(jnp.mean(x * x, axis=-1, keepdims=True) + EPS) * g


def _bdot(a, b):
    return jnp.dot(a, b, preferred_element_type=jnp.float32)


def _mixer_kernel(nseg, seg_len, mask_history,
                  x_ref, conv0_ref, h0_ref, k0_ref, v0_ref,
                  norm_mix_ref, w_in_ref, b_merge_ref, conv_w_ref, conv_b_ref, w_rg_ref, b_rg_a_ref, b_rg_x_ref,
                  lam_ref, sink_ref, bias_ref, w_lru_ref, w_attn_ref, w_out_ref, norm_ffn_ref, w_rt_ref, b_rt_ref,
                  x1_ref, xn2_ref, comb_ref, conv_out_ref, h_out_ref, k_out_ref, v_out_ref,
                  xp_ref, hc_ref, kbuf_ref, vbuf_ref, a_ref, b_ref, attn_ref):
    step = pl.program_id(0)
    m_rows = nseg * seg_len
    n_chunks = seg_len // CHUNK
    keep = min(seg_len, HIST)

    @pl.when(step == 0)
    def _():
        xp_ref[:, 0:SUBLANES, :] = conv0_ref[...]
        hc_ref[...] = h0_ref[...]
        kbuf_ref[:, 0:HIST, :] = k0_ref[...]
        vbuf_ref[:, 0:HIST, :] = v0_ref[...]

    x = x_ref[...]
    xn = _rmsnorm(x, norm_mix_ref[...]).astype(jnp.bfloat16)

    lru_x = _bdot(xn, w_in_ref[:, C_LX:C_LX + D_RNN])
    conv_w = conv_w_ref[...]
    xc_parts = []
    for s in range(nseg):
        xp_ref[s, SUBLANES:SUBLANES + seg_len, :] = lru_x[s * seg_len:(s + 1) * seg_len, :]
        acc = conv_b_ref[...]
        for j in range(CONV_W):
            off = SUBLANES - (CONV_W - 1) + j
            acc = acc + xp_ref[s, off:off + seg_len, :] * conv_w[j:j + 1, :]
        xc_parts.append(acc)
        conv_out_ref[s] = xp_ref[s, seg_len:seg_len + SUBLANES, :]
        xp_ref[s, 0:SUBLANES, :] = xp_ref[s, seg_len:seg_len + SUBLANES, :]
    xc = xc_parts[0] if nseg == 1 else jnp.concatenate(xc_parts, axis=0)
    xc_b = xc.astype(jnp.bfloat16)

    lam = lam_ref[...]
    log_sig = jnp.minimum(lam, 0.0) - jnp.log1p(jnp.exp(-jnp.abs(lam)))
    c8 = LRU_C * log_sig
    for j in range(N_RG_GROUPS):
        cs = slice(j * RG_GROUP, (j + 1) * RG_GROUP)
        pre = _bdot(xc_b[:, cs], w_rg_ref[j])
        r = jax.nn.sigmoid(pre[:, :RG_GROUP] + b_rg_a_ref[:, cs])
        i = jax.nn.sigmoid(pre[:, RG_GROUP:] + b_rg_x_ref[:, cs])
        log_a = c8[:, cs] * r
        a_ref[:, cs] = jnp.exp(log_a)
        th = jnp.tanh(log_a)
        b_ref[:, cs] = jnp.sqrt(-2.0 * th / (1.0 - th)) * i * xc[:, cs]

    row = lax.broadcasted_iota(jnp.int32, (SUBLANES, D_RNN), 0)
    for s in range(nseg):
        def scan_body(g, carry, s=s):
            r0 = pl.multiple_of(s * seg_len + g * SUBLANES, SUBLANES)
            a = a_ref[pl.ds(r0, SUBLANES), :]
            b = b_ref[pl.ds(r0, SUBLANES), :]
            for d in (1, 2, 4):
                a_s = pltpu.roll(a, d, axis=0)
                b_s = pltpu.roll(b, d, axis=0)
                m = row >= d
                b = jnp.where(m, a * b_s + b, b)
                a = jnp.where(m, a * a_s, a)
            h = a * carry + b
            b_ref[pl.ds(r0, SUBLANES), :] = h
            return jnp.broadcast_to(h[SUBLANES - 1:SUBLANES, :], (SUBLANES, D_RNN))

        h_last = lax.fori_loop(0, seg_len // SUBLANES, scan_body, hc_ref[s])
        hc_ref[s] = h_last
        h_out_ref[s] = h_last

    lru_gate = _bdot(xn, w_in_ref[:, C_LG:C_LG + D_RNN])
    lru_y = (b_ref[...] * jax.nn.gelu(lru_gate)).astype(jnp.bfloat16)
    mixed = _bdot(lru_y, w_lru_ref[...])
    gates = _bdot(xn, w_in_ref[:, C_G:C_G + D_MODEL]) + b_merge_ref[:, 0:D_MODEL]
    mixed = jax.nn.sigmoid(gates) * mixed

    q = _bdot(xn, w_in_ref[:, C_Q:C_Q + Q_W]).astype(jnp.bfloat16)
    k2 = _bdot(xn, w_in_ref[:, C_K:C_K + KV2_W])
    v2 = _bdot(xn, w_in_ref[:, C_V:C_V + KV2_W])
    lane_q = lax.broadcasted_iota(jnp.int32, (CHUNK, LANES), 1)
    lane_o = lax.broadcasted_iota(jnp.int32, (CHUNK, LANES), 1)
    key_lane = lax.broadcasted_iota(jnp.int32, (1, BAND), 1)
    row_grp = lax.broadcasted_iota(jnp.int32, (GROUP * CHUNK, 1), 0) // CHUNK
    for s in range(nseg):
        rows = slice(s * seg_len, (s + 1) * seg_len)
        kbuf_ref[s, HIST:HIST + seg_len, :] = k2[rows].astype(jnp.bfloat16)
        vbuf_ref[s, HIST:HIST + seg_len, :] = v2[rows].astype(jnp.bfloat16)
        k_out_ref[s] = k2[s * seg_len + seg_len - keep:(s + 1) * seg_len]
        v_out_ref[s] = v2[s * seg_len + seg_len - keep:(s + 1) * seg_len]
        for c in range(n_chunks):
            q_c = q[s * seg_len + c * CHUNK:s * seg_len + (c + 1) * CHUNK]
            slabs = []
            for kv in range(N_KV):
                parts = []
                for g in range(GROUP):
                    col = kv * GROUP * HEAD_DIM + (g // 2) * LANES
                    slab = q_c[:, col:col + LANES]
                    keep_lo = (g % 2) == 0
                    sel = (lane_q < HEAD_DIM) if keep_lo else (lane_q >= HEAD_DIM)
                    parts.append(jnp.where(sel, slab, jnp.zeros_like(slab)))
                q_stack = jnp.concatenate(parts, axis=0)
                k_band = kbuf_ref[s, c * CHUNK:c * CHUNK + BAND, kv * LANES:(kv + 1) * LANES]
                v_band = vbuf_ref[s, c * CHUNK:c * CHUNK + BAND, kv * LANES:(kv + 1) * LANES]
                sc = lax.dot_general(q_stack, k_band, (((1,), (1,)), ((), ())),
                                     preferred_element_type=jnp.float32)
                sc = sc + bias_ref[kv]
                if mask_history:
                    first_valid = HIST - (step * n_chunks + c) * CHUNK
                    sc = jnp.where(key_lane >= first_valid, sc, NEG_INF)
                sink = jnp.zeros((GROUP * CHUNK, 1), jnp.float32)
                for g in range(GROUP):
                    sink = jnp.where(row_grp == g, sink_ref[kv * GROUP + g], sink)
                m = jnp.maximum(jnp.max(sc, axis=-1, keepdims=True), sink)
                p = jnp.exp(sc - m)
                denom = jnp.sum(p, axis=-1, keepdims=True) + jnp.exp(sink - m)
                p = (p / denom).astype(jnp.bfloat16)
                o = _bdot(p, v_band)
                for pair in range(GROUP // 2):
                    lo = o[(2 * pair) * CHUNK:(2 * pair + 1) * CHUNK]
                    hi = o[(2 * pair + 1) * CHUNK:(2 * pair + 2) * CHUNK]
                    slabs.append(jnp.where(lane_o < HEAD_DIM, lo, hi))
            attn_ref[s * seg_len + c * CHUNK:s * seg_len + (c + 1) * CHUNK, :] = (
                jnp.concatenate(slabs, axis=1).astype(jnp.bfloat16))
        kbuf_ref[s, 0:HIST, :] = kbuf_ref[s, seg_len:seg_len + HIST, :]
        vbuf_ref[s, 0:HIST, :] = vbuf_ref[s, seg_len:seg_len + HIST, :]

    attn = _bdot(attn_ref[...], w_attn_ref[...])
    gates = _bdot(xn, w_in_ref[:, C_G + D_MODEL:C_G + 2 * D_MODEL]) + b_merge_ref[:, D_MODEL:2 * D_MODEL]
    mixed = (mixed + jax.nn.sigmoid(gates) * attn).astype(jnp.bfloat16)
    x1 = x + _bdot(mixed, w_out_ref[...])
    x1_ref[...] = x1

    xn2 = _rmsnorm(x1, norm_ffn_ref[...])
    xn2_ref[...] = xn2.astype(jnp.bfloat16)
    logits = jnp.dot(xn2, w_rt_ref[...], precision=lax.Precision.HIGHEST,
                     preferred_element_type=jnp.float32) + b_rt_ref[...]
    lane = lax.broadcasted_iota(jnp.int32, (m_rows, ROUTER_W), 1).astype(jnp.float32)
    far = jnp.float32(2 * ROUTER_W)
    is_group = (lane >= N_EXPERTS) & (lane < N_EXPERTS + N_GROUPS)
    gl = jnp.where(is_group, logits, NEG_INF)
    g_max = jnp.max(gl, axis=-1, keepdims=True)
    g_idx = jnp.min(jnp.where(gl == g_max, lane, far), axis=-1, keepdims=True) - N_EXPERTS
    g_w = 1.0 / jnp.sum(jnp.where(is_group, jnp.exp(gl - g_max), 0.0), axis=-1, keepdims=True)
    in_group = (lane < N_EXPERTS) & (jnp.floor(lane * (1.0 / EXPERTS_PER_GROUP)) == g_idx)
    el = jnp.where(in_group, logits, NEG_INF)
    e1 = jnp.max(el, axis=-1, keepdims=True)
    i1 = jnp.min(jnp.where(el == e1, lane, far), axis=-1, keepdims=True)
    el2 = jnp.where(lane == i1, NEG_INF, el)
    e2 = jnp.max(el2, axis=-1, keepdims=True)
    i2 = jnp.min(jnp.where(el2 == e2, lane, far), axis=-1, keepdims=True)
    t = jnp.exp(e2 - e1)
    w1 = g_w / (1.0 + t)
    w2 = w1 * t
    comb_ref[...] = jnp.where(lane == i1, w1, 0.0) + jnp.where(lane == i2, w2, 0.0)


def _const_spec(shape):
    zeros = (0,) * len(shape)
    return pl.BlockSpec(shape, lambda i: zeros, pipeline_mode=pl.Buffered(1))


def _mixer(x, conv0, h0, k0, v0, weights, *, nseg, seg_len, mask_history):
    n_tok = x.shape[0]
    m_rows = nseg * seg_len
    n_steps = n_tok // m_rows
    keep = min(seg_len, HIST)
    row_spec = lambda w: pl.BlockSpec((m_rows, w), lambda i: (i, 0))
    in_specs = [row_spec(D_MODEL), _const_spec(conv0.shape), _const_spec(h0.shape), _const_spec(k0.shape),
                _const_spec(v0.shape)]
    for w in weights:
        if w.ndim == 1:
            in_specs.append(pl.BlockSpec(memory_space=pltpu.SMEM))
        else:
            in_specs.append(_const_spec(w.shape))
    state_spec = lambda r, w: pl.BlockSpec((nseg, r, w), lambda i: (0, 0, 0))
    out_shape = (
        jax.ShapeDtypeStruct((n_tok, D_MODEL), jnp.float32),
        jax.ShapeDtypeStruct((n_tok, D_MODEL), jnp.bfloat16),
        jax.ShapeDtypeStruct((n_tok, ROUTER_W), jnp.float32),
        jax.ShapeDtypeStruct((nseg, SUBLANES, D_RNN), jnp.float32),
        jax.ShapeDtypeStruct((nseg, SUBLANES, D_RNN), jnp.float32),
        jax.ShapeDtypeStruct((nseg, keep, KV2_W), jnp.float32),
        jax.ShapeDtypeStruct((nseg, keep, KV2_W), jnp.float32),
    )
    out_specs = (row_spec(D_MODEL), row_spec(D_MODEL), row_spec(ROUTER_W),
                 state_spec(SUBLANES, D_RNN), state_spec(SUBLANES, D_RNN),
                 state_spec(keep, KV2_W), state_spec(keep, KV2_W))
    scratch = [
        pltpu.VMEM((nseg, SUBLANES + seg_len, D_RNN), jnp.float32),
        pltpu.VMEM((nseg, SUBLANES, D_RNN), jnp.float32),
        pltpu.VMEM((nseg, HIST + seg_len, KV2_W), jnp.bfloat16),
        pltpu.VMEM((nseg, HIST + seg_len, KV2_W), jnp.bfloat16),
        pltpu.VMEM((m_rows, D_RNN), jnp.float32),
        pltpu.VMEM((m_rows, D_RNN), jnp.float32),
        pltpu.VMEM((m_rows, Q_W), jnp.bfloat16),
    ]
    return pl.pallas_call(
        functools.partial(_mixer_kernel, nseg, seg_len, mask_history),
        grid=(n_steps,),
        in_specs=in_specs,
        out_specs=out_specs,
        out_shape=out_shape,
        scratch_shapes=scratch,
        compiler_params=pltpu.CompilerParams(dimension_semantics=("arbitrary",), vmem_limit_bytes=VMEM_LIMIT),
        name="mixer_prompt" if mask_history else "mixer_sample",
    )(x, conv0, h0, k0, v0, *weights)


def _moe_kernel(xn2_ref, comb_ref, x1_ref, wg_ref, wu_ref, wd_ref, norm_ref, y_ref, acc_ref):
    e = pl.program_id(1)

    @pl.when(e == 0)
    def _():
        acc_ref[...] = jnp.zeros_like(acc_ref)

    xb = xn2_ref[...]
    h = jax.nn.silu(_bdot(xb, wg_ref[...])) * _bdot(xb, wu_ref[...])
    o = _bdot(h.astype(jnp.bfloat16), wd_ref[...])
    lane = lax.broadcasted_iota(jnp.int32, comb_ref.shape, 1)
    c_e = jnp.sum(jnp.where(lane == e, comb_ref[...], 0.0), axis=-1, keepdims=True)
    acc_ref[...] += c_e * o

    @pl.when(e == N_EXPERTS - 1)
    def _():
        y_ref[...] = _rmsnorm(x1_ref[...] + acc_ref[...], norm_ref[...])


def _moe(xn2, comb, x1, wg, wu, wd, norm_final, *, tile):
    n_tok = xn2.shape[0]
    row_spec = lambda w: pl.BlockSpec((tile, w), lambda i, e: (i, 0))
    return pl.pallas_call(
        _moe_kernel,
        grid=(n_tok // tile, N_EXPERTS),
        in_specs=[row_spec(D_MODEL), row_spec(ROUTER_W), row_spec(D_MODEL),
                  pl.BlockSpec((None, D_MODEL, D_EXPERT), lambda i, e: (e, 0, 0)),
                  pl.BlockSpec((None, D_MODEL, D_EXPERT), lambda i, e: (e, 0, 0)),
                  pl.BlockSpec((None, D_EXPERT, D_MODEL), lambda i, e: (e, 0, 0)),
                  pl.BlockSpec((1, D_MODEL), lambda i, e: (0, 0))],
        out_specs=row_spec(D_MODEL),
        out_shape=jax.ShapeDtypeStruct((n_tok, D_MODEL), jnp.float32),
        scratch_shapes=[pltpu.VMEM((tile, D_MODEL), jnp.float32)],
        compiler_params=pltpu.CompilerParams(dimension_semantics=("arbitrary", "arbitrary"),
                                             vmem_limit_bytes=VMEM_LIMIT),
        name="moe_dense",
    )(xn2, comb, x1, wg, wu, wd, norm_final)


def _rel_bucket(rel):
    nb = REL_BUCKETS // 2
    n = -rel
    ret = jnp.where(n < 0, nb, 0)
    n = jnp.abs(n)
    max_exact = nb // 2
    nf = jnp.maximum(n, 1).astype(jnp.float32)
    large = max_exact + (jnp.log(nf / max_exact) / math.log(REL_MAX_DIST / max_exact) * (nb - max_exact)).astype(jnp.int32)
    large = jnp.minimum(large, nb - 1)
    return ret + jnp.where(n < max_exact, n, large)


def _band_bias(rel_table):
    rel = (jnp.arange(BAND) - HIST)[None, :] - jnp.arange(CHUNK)[:, None]
    bias = rel_table[_rel_bucket(rel)]
    bias = jnp.transpose(bias, (2, 0, 1)).reshape(N_KV, GROUP * CHUNK, BAND)
    return bias.astype(jnp.float32)


def _dup_heads(a):
    a = a.reshape(a.shape[:-1] + (N_KV, HEAD_DIM))
    return jnp.concatenate([a, a], axis=-1).reshape(a.shape[:-2] + (KV2_W,))


def _undup_heads(a):
    a = a.reshape(a.shape[:-1] + (N_KV, LANES))
    return a[..., :HEAD_DIM]


def _block_diag_group(w, j):
    per = RG_GROUP // LRU_BLOCK_W
    blk = w[j * per:(j + 1) * per]
    eye = jnp.eye(per, dtype=w.dtype)
    return jnp.einsum('nde,nm->ndme', blk, eye).reshape(RG_GROUP, RG_GROUP)


def kernel(x_prompt, x_sample, state_lru_h, state_lru_conv, cache_swa_k, cache_swa_v, norm_mix, w_in, b_merge, conv_w, conv_b, w_rg_a, b_rg_a, w_rg_x, b_rg_x, lru_lambda, attn_sink, rel_bias, w_lru_proj, w_attn_proj, w_out, norm_ffn, w_group, b_group, w_router, b_router, w_e_gate, w_e_up, w_e_down, norm_final):
    f32, bf16 = jnp.float32, jnp.bfloat16
    l = 0
    w = w_in[l]
    scale = HEAD_DIM ** -0.5
    w_in2 = jnp.concatenate([
        w[:, :2 * D_RNN],
        w[:, 2 * D_RNN:2 * D_RNN + Q_W] * scale,
        _dup_heads(w[:, 2 * D_RNN + Q_W:2 * D_RNN + Q_W + KV_W]),
        _dup_heads(w[:, 2 * D_RNN + Q_W + KV_W:2 * D_RNN + Q_W + 2 * KV_W]),
        w[:, 2 * D_RNN + Q_W + 2 * KV_W:],
    ], axis=1).astype(bf16)
    w_rg = jnp.stack([
        jnp.concatenate([_block_diag_group(w_rg_a[l], j), _block_diag_group(w_rg_x[l], j)], axis=1)
        for j in range(N_RG_GROUPS)]).astype(bf16)
    w_rt = jnp.concatenate([w_router[l], w_group[l],
                            jnp.zeros((D_MODEL, ROUTER_W - N_EXPERTS - N_GROUPS), f32)], axis=1)
    b_rt = jnp.concatenate([b_router[l], b_group[l], jnp.zeros((ROUTER_W - N_EXPERTS - N_GROUPS,), f32)])[None, :]
    weights = (
        norm_mix[l][None, :], w_in2, b_merge[l][None, :], conv_w[l], conv_b[l][None, :], w_rg,
        b_rg_a[l][None, :], b_rg_x[l][None, :], lru_lambda[l][None, :], attn_sink[l], _band_bias(rel_bias),
        w_lru_proj[l].astype(bf16), w_attn_proj[l].astype(bf16), w_out[l].astype(bf16), norm_ffn[l][None, :],
        w_rt, b_rt,
    )
    wg, wu, wd = w_e_gate[l].astype(bf16), w_e_up[l].astype(bf16), w_e_down[l].astype(bf16)
    nf = norm_final[None, :]

    bp, tp, _ = x_prompt.shape
    assert bp == 1
    xp = x_prompt.reshape(tp, D_MODEL)
    zeros = lambda *s: jnp.zeros(s, f32)
    p_x1, p_xn2, p_comb, p_conv, p_h, p_k, p_v = _mixer(
        xp, zeros(1, SUBLANES, D_RNN), zeros(1, SUBLANES, D_RNN),
        jnp.zeros((1, HIST, KV2_W), bf16), jnp.zeros((1, HIST, KV2_W), bf16), weights,
        nseg=1, seg_len=256, mask_history=True)
    y_p = _moe(p_xn2, p_comb, p_x1, wg, wu, wd, nf, tile=1024).reshape(x_prompt.shape)

    bs, ts, _ = x_sample.shape
    xs = x_sample.reshape(bs * ts, D_MODEL)
    conv0 = jnp.pad(state_lru_conv[l], ((0, 0), (SUBLANES - (CONV_W - 1), 0), (0, 0)))
    h0 = jnp.broadcast_to(state_lru_h[l][:, None, :], (bs, SUBLANES, D_RNN))
    ck = cache_swa_k[l].reshape(bs, -1, KV_W)
    cv = cache_swa_v[l].reshape(bs, -1, KV_W)
    s_x1, s_xn2, s_comb, s_conv, s_h, s_k, s_v = _mixer(
        xs, conv0, h0, _dup_heads(ck).astype(bf16), _dup_heads(cv).astype(bf16), weights,
        nseg=bs, seg_len=ts, mask_history=False)
    y_s = _moe(s_xn2, s_comb, s_x1, wg, wu, wd, nf, tile=bs * ts).reshape(x_sample.shape)

    tail = CONV_W - 1
    p_lru_h = p_h[:, 0, :][None]
    p_lru_conv = p_conv[:, SUBLANES - tail:, :][None]
    p_swa_k = _undup_heads(p_k)[None]
    p_swa_v = _undup_heads(p_v)[None]
    s_lru_h = s_h[:, 0, :][None]
    s_lru_conv = s_conv[:, SUBLANES - tail:, :][None]
    win = ck.shape[1]
    s_swa_k = jnp.concatenate([cache_swa_k[l][:, ts:], _undup_heads(s_k)], axis=1)[:, -win:][None]
    s_swa_v = jnp.concatenate([cache_swa_v[l][:, ts:], _undup_heads(s_v)], axis=1)[:, -win:][None]
    return (y_p, y_s, p_lru_h, p_lru_conv, p_swa_k, p_swa_v, s_lru_h, s_lru_conv, s_swa_k, s_swa_v)
```

```python
import functools
import math

import jax
import jax.numpy as jnp
from jax import lax
from jax.experimental import pallas as pl
from jax.experimental.pallas import tpu as pltpu

D_MODEL = 1024
CHUNK = 64
D_RNN = D_MODEL
CONV_W = 4
LRU_BLOCKS = 16
LRU_BLOCK_W = D_RNN // LRU_BLOCKS
LRU_C = 8.0
N_HEADS = 16
N_KV = 4
HEAD_DIM = 64
GROUP = N_HEADS // N_KV
WINDOW = 128
LOOKBACK_CHUNKS = -(-WINDOW // CHUNK)
HIST = LOOKBACK_CHUNKS * CHUNK
BAND = HIST + CHUNK
Q_W = N_HEADS * HEAD_DIM
KV_W = N_KV * HEAD_DIM
REL_BUCKETS = 32
REL_MAX_DIST = 128
N_GROUPS = 4
EXPERTS_PER_GROUP = 8
N_EXPERTS = N_GROUPS * EXPERTS_PER_GROUP
D_EXPERT = 256
EPS = 1e-6
NEG_INF = -1e30

LANES = 128
SUBLANES = 8
KV2_W = N_KV * LANES
RG_GROUP = 256
N_RG_GROUPS = D_RNN // RG_GROUP
ROUTER_W = LANES
VMEM_LIMIT = 56 * 1024 * 1024
PACK_W = D_MODEL // 2
GROUP_FF = EXPERTS_PER_GROUP * D_EXPERT
GID_LANE = EXPERTS_PER_GROUP
SORT_TILE = 512

C_LX, C_LG, C_Q = 0, D_RNN, 2 * D_RNN
C_K = C_Q + Q_W
C_V = C_K + KV2_W
C_G = C_V + KV2_W
IN_W2 = C_G + 2 * D_MODEL


def _rmsnorm(x, g):
    return x * lax.rsqrt(jnp.mean(x * x, axis=-1, keepdims=True) + EPS) * g


def _bdot(a, b):
    return jnp.dot(a, b, preferred_element_type=jnp.float32)


def _pack_bf16_pairs(x):
    lo = lax.bitcast_convert_type(x[:, :PACK_W].astype(jnp.bfloat16).astype(jnp.float32), jnp.uint32)
    hi = lax.bitcast_convert_type(x[:, PACK_W:].astype(jnp.bfloat16).astype(jnp.float32), jnp.uint32)
    return (lo >> 16) | (hi & jnp.uint32(0xFFFF0000))


def _unpack_bf16_pairs(p):
    lo = lax.bitcast_convert_type(p << 16, jnp.float32)
    hi = lax.bitcast_convert_type(p & jnp.uint32(0xFFFF0000), jnp.float32)
    return jnp.concatenate([lo, hi], axis=1)


def _mixer_kernel(nseg, seg_len, mask_history, sparse_out,
                  x_ref, conv0_ref, h0_ref, k0_ref, v0_ref,
                  norm_mix_ref, w_in_ref, b_merge_ref, conv_w_ref, conv_b_ref, w_rg_ref, b_rg_a_ref, b_rg_x_ref,
                  lam_ref, sink_ref, bias_ref, w_lru_ref, w_attn_ref, w_out_ref, norm_ffn_ref, w_rt_ref, b_rt_ref,
                  x1_ref, xn2_ref, comb_ref, conv_out_ref, h_out_ref, k_out_ref, v_out_ref,
                  xp_ref, hc_ref, kbuf_ref, vbuf_ref, a_ref, b_ref, attn_ref):
    step = pl.program_id(0)
    m_rows = nseg * seg_len
    n_chunks = seg_len // CHUNK
    keep = min(seg_len, HIST)

    @pl.when(step == 0)
    def _():
        xp_ref[:, 0:SUBLANES, :] = conv0_ref[...]
        hc_ref[...] = h0_ref[...]
        kbuf_ref[:, 0:HIST, :] = k0_ref[...]
        vbuf_ref[:, 0:HIST, :] = v0_ref[...]

    x = x_ref[...]
    xn = _rmsnorm(x, norm_mix_ref[...]).astype(jnp.bfloat16)

    lru_x = _bdot(xn, w_in_ref[:, C_LX:C_LX + D_RNN])
    conv_w = conv_w_ref[...]
    xc_parts = []
    for s in range(nseg):
        xp_ref[s, SUBLANES:SUBLANES + seg_len, :] = lru_x[s * seg_len:(s + 1) * seg_len, :]
        acc = conv_b_ref[...]
        for j in range(CONV_W):
            off = SUBLANES - (CONV_W - 1) + j
            acc = acc + xp_ref[s, off:off + seg_len, :] * conv_w[j:j + 1, :]
        xc_parts.append(acc)
        conv_out_ref[s] = xp_ref[s, seg_len:seg_len + SUBLANES, :]
        xp_ref[s, 0:SUBLANES, :] = xp_ref[s, seg_len:seg_len + SUBLANES, :]
    xc = xc_parts[0] if nseg == 1 else jnp.concatenate(xc_parts, axis=0)
    xc_b = xc.astype(jnp.bfloat16)

    lam = lam_ref[...]
    log_sig = jnp.minimum(lam, 0.0) - jnp.log1p(jnp.exp(-jnp.abs(lam)))
    c8 = LRU_C * log_sig
    for j in range(N_RG_GROUPS):
        cs = slice(j * RG_GROUP, (j + 1) * RG_GROUP)
        pre = _bdot(xc_b[:, cs], w_rg_ref[j])
        r = jax.nn.sigmoid(pre[:, :RG_GROUP] + b_rg_a_ref[:, cs])
        i = jax.nn.sigmoid(pre[:, RG_GROUP:] + b_rg_x_ref[:, cs])
        log_a = c8[:, cs] * r
        a_ref[:, cs] = jnp.exp(log_a)
        th = jnp.tanh(log_a)
        b_ref[:, cs] = jnp.sqrt(-2.0 * th / (1.0 - th)) * i * xc[:, cs]

    row = lax.broadcasted_iota(jnp.int32, (SUBLANES, D_RNN), 0)
    for s in range(nseg):
        def scan_body(g, carry, s=s):
            r0 = pl.multiple_of(s * seg_len + g * SUBLANES, SUBLANES)
            a = a_ref[pl.ds(r0, SUBLANES), :]
            b = b_ref[pl.ds(r0, SUBLANES), :]
            for d in (1, 2, 4):
                a_s = pltpu.roll(a, d, axis=0)
                b_s = pltpu.roll(b, d, axis=0)
                m = row >= d
                b = jnp.where(m, a * b_s + b, b)
                a = jnp.where(m, a * a_s, a)
            h = a * carry + b
            b_ref[pl.ds(r0, SUBLANES), :] = h
            return jnp.broadcast_to(h[SUBLANES - 1:SUBLANES, :], (SUBLANES, D_RNN))

        h_last = lax.fori_loop(0, seg_len // SUBLANES, scan_body, hc_ref[s])
        hc_ref[s] = h_last
        h_out_ref[s] = h_last

    lru_gate = _bdot(xn, w_in_ref[:, C_LG:C_LG + D_RNN])
    lru_y = (b_ref[...] * jax.nn.gelu(lru_gate)).astype(jnp.bfloat16)
    mixed = _bdot(lru_y, w_lru_ref[...])
    gates = _bdot(xn, w_in_ref[:, C_G:C_G + D_MODEL]) + b_merge_ref[:, 0:D_MODEL]
    mixed = jax.nn.sigmoid(gates) * mixed

    q = _bdot(xn, w_in_ref[:, C_Q:C_Q + Q_W]).astype(jnp.bfloat16)
    k2 = _bdot(xn, w_in_ref[:, C_K:C_K + KV2_W])
    v2 = _bdot(xn, w_in_ref[:, C_V:C_V + KV2_W])
    lane_q = lax.broadcasted_iota(jnp.int32, (CHUNK, LANES), 1)
    lane_o = lax.broadcasted_iota(jnp.int32, (CHUNK, LANES), 1)
    key_lane = lax.broadcasted_iota(jnp.int32, (1, BAND), 1)
    row_grp = lax.broadcasted_iota(jnp.int32, (GROUP * CHUNK, 1), 0) // CHUNK
    for s in range(nseg):
        rows = slice(s * seg_len, (s + 1) * seg_len)
        kbuf_ref[s, HIST:HIST + seg_len, :] = k2[rows].astype(jnp.bfloat16)
        vbuf_ref[s, HIST:HIST + seg_len, :] = v2[rows].astype(jnp.bfloat16)
        k_out_ref[s] = k2[s * seg_len + seg_len - keep:(s + 1) * seg_len]
        v_out_ref[s] = v2[s * seg_len + seg_len - keep:(s + 1) * seg_len]
        for c in range(n_chunks):
            q_c = q[s * seg_len + c * CHUNK:s * seg_len + (c + 1) * CHUNK]
            slabs = []
            for kv in range(N_KV):
                parts = []
                for g in range(GROUP):
                    col = kv * GROUP * HEAD_DIM + (g // 2) * LANES
                    slab = q_c[:, col:col + LANES]
                    keep_lo = (g % 2) == 0
                    sel = (lane_q < HEAD_DIM) if keep_lo else (lane_q >= HEAD_DIM)
                    parts.append(jnp.where(sel, slab, jnp.zeros_like(slab)))
                q_stack = jnp.concatenate(parts, axis=0)
                k_band = kbuf_ref[s, c * CHUNK:c * CHUNK + BAND, kv * LANES:(kv + 1) * LANES]
                v_band = vbuf_ref[s, c * CHUNK:c * CHUNK + BAND, kv * LANES:(kv + 1) * LANES]
                sc = lax.dot_general(q_stack, k_band, (((1,), (1,)), ((), ())),
                                     preferred_element_type=jnp.float32)
                sc = sc + bias_ref[kv]
                if mask_history:
                    first_valid = HIST - (step * n_chunks + c) * CHUNK
                    sc = jnp.where(key_lane >= first_valid, sc, NEG_INF)
                sink = jnp.zeros((GROUP * CHUNK, 1), jnp.float32)
                for g in range(GROUP):
                    sink = jnp.where(row_grp == g, sink_ref[kv * GROUP + g], sink)
                m = jnp.maximum(jnp.max(sc, axis=-1, keepdims=True), sink)
                p = jnp.exp(sc - m)
                denom = jnp.sum(p, axis=-1, keepdims=True) + jnp.exp(sink - m)
                p = (p / denom).astype(jnp.bfloat16)
                o = _bdot(p, v_band)
                for pair in range(GROUP // 2):
                    lo = o[(2 * pair) * CHUNK:(2 * pair + 1) * CHUNK]
                    hi = o[(2 * pair + 1) * CHUNK:(2 * pair + 2) * CHUNK]
                    slabs.append(jnp.where(lane_o < HEAD_DIM, lo, hi))
            attn_ref[s * seg_len + c * CHUNK:s * seg_len + (c + 1) * CHUNK, :] = (
                jnp.concatenate(slabs, axis=1).astype(jnp.bfloat16))
        kbuf_ref[s, 0:HIST, :] = kbuf_ref[s, seg_len:seg_len + HIST, :]
        vbuf_ref[s, 0:HIST, :] = vbuf_ref[s, seg_len:seg_len + HIST, :]

    attn = _bdot(attn_ref[...], w_attn_ref[...])
    gates = _bdot(xn, w_in_ref[:, C_G + D_MODEL:C_G + 2 * D_MODEL]) + b_merge_ref[:, D_MODEL:2 * D_MODEL]
    mixed = (mixed + jax.nn.sigmoid(gates) * attn).astype(jnp.bfloat16)
    x1 = x + _bdot(mixed, w_out_ref[...])
    x1_ref[...] = x1

    xn2 = _rmsnorm(x1, norm_ffn_ref[...])
    xn2_ref[...] = _pack_bf16_pairs(xn2) if sparse_out else xn2.astype(jnp.bfloat16)
    logits = jnp.dot(xn2, w_rt_ref[...], precision=lax.Precision.HIGHEST,
                     preferred_element_type=jnp.float32) + b_rt_ref[...]
    lane = lax.broadcasted_iota(jnp.int32, (m_rows, ROUTER_W), 1).astype(jnp.float32)
    far = jnp.float32(2 * ROUTER_W)
    is_group = (lane >= N_EXPERTS) & (lane < N_EXPERTS + N_GROUPS)
    gl = jnp.where(is_group, logits, NEG_INF)
    g_max = jnp.max(gl, axis=-1, keepdims=True)
    g_idx = jnp.min(jnp.where(gl == g_max, lane, far), axis=-1, keepdims=True) - N_EXPERTS
    g_w = 1.0 / jnp.sum(jnp.where(is_group, jnp.exp(gl - g_max), 0.0), axis=-1, keepdims=True)
    in_group = (lane < N_EXPERTS) & (jnp.floor(lane * (1.0 / EXPERTS_PER_GROUP)) == g_idx)
    el = jnp.where(in_group, logits, NEG_INF)
    e1 = jnp.max(el, axis=-1, keepdims=True)
    i1 = jnp.min(jnp.where(el == e1, lane, far), axis=-1, keepdims=True)
    el2 = jnp.where(lane == i1, NEG_INF, el)
    e2 = jnp.max(el2, axis=-1, keepdims=True)
    i2 = jnp.min(jnp.where(el2 == e2, lane, far), axis=-1, keepdims=True)
    t = jnp.exp(e2 - e1)
    w1 = g_w / (1.0 + t)
    w2 = w1 * t
    if sparse_out:
        e_lane = lane + g_idx * EXPERTS_PER_GROUP
        rec = jnp.where(e_lane == i1, w1, 0.0) + jnp.where(e_lane == i2, w2, 0.0)
        rec = jnp.where(lane < EXPERTS_PER_GROUP, rec, 0.0)
        comb_ref[...] = jnp.where(lane == GID_LANE, g_idx, rec)
    else:
        comb_ref[...] = jnp.where(lane == i1, w1, 0.0) + jnp.where(lane == i2, w2, 0.0)


def _const_spec(shape):
    zeros = (0,) * len(shape)
    return pl.BlockSpec(shape, lambda i: zeros, pipeline_mode=pl.Buffered(1))


def _mixer(x, conv0, h0, k0, v0, weights, *, nseg, seg_len, mask_history, sparse_out):
    n_tok = x.shape[0]
    m_rows = nseg * seg_len
    n_steps = n_tok // m_rows
    keep = min(seg_len, HIST)
    row_spec = lambda w: pl.BlockSpec((m_rows, w), lambda i: (i, 0))
    in_specs = [row_spec(D_MODEL), _const_spec(conv0.shape), _const_spec(h0.shape), _const_spec(k0.shape),
                _const_spec(v0.shape)]
    for w in weights:
        if w.ndim == 1:
            in_specs.append(pl.BlockSpec(memory_space=pltpu.SMEM))
        else:
            in_specs.append(_const_spec(w.shape))
    state_spec = lambda r, w: pl.BlockSpec((nseg, r, w), lambda i: (0, 0, 0))
    out_shape = (
        jax.ShapeDtypeStruct((n_tok, D_MODEL), jnp.float32),
        (jax.ShapeDtypeStruct((n_tok, PACK_W), jnp.uint32) if sparse_out
         else jax.ShapeDtypeStruct((n_tok, D_MODEL), jnp.bfloat16)),
        jax.ShapeDtypeStruct((n_tok, ROUTER_W), jnp.float32),
        jax.ShapeDtypeStruct((nseg, SUBLANES, D_RNN), jnp.float32),
        jax.ShapeDtypeStruct((nseg, SUBLANES, D_RNN), jnp.float32),
        jax.ShapeDtypeStruct((nseg, keep, KV2_W), jnp.float32),
        jax.ShapeDtypeStruct((nseg, keep, KV2_W), jnp.float32),
    )
    out_specs = (row_spec(D_MODEL), row_spec(PACK_W if sparse_out else D_MODEL), row_spec(ROUTER_W),
                 state_spec(SUBLANES, D_RNN), state_spec(SUBLANES, D_RNN),
                 state_spec(keep, KV2_W), state_spec(keep, KV2_W))
    scratch = [
        pltpu.VMEM((nseg, SUBLANES + seg_len, D_RNN), jnp.float32),
        pltpu.VMEM((nseg, SUBLANES, D_RNN), jnp.float32),
        pltpu.VMEM((nseg, HIST + seg_len, KV2_W), jnp.bfloat16),
        pltpu.VMEM((nseg, HIST + seg_len, KV2_W), jnp.bfloat16),
        pltpu.VMEM((m_rows, D_RNN), jnp.float32),
        pltpu.VMEM((m_rows, D_RNN), jnp.float32),
        pltpu.VMEM((m_rows, Q_W), jnp.bfloat16),
    ]
    return pl.pallas_call(
        functools.partial(_mixer_kernel, nseg, seg_len, mask_history, sparse_out),
        grid=(n_steps,),
        in_specs=in_specs,
        out_specs=out_specs,
        out_shape=out_shape,
        scratch_shapes=scratch,
        compiler_params=pltpu.CompilerParams(dimension_semantics=("arbitrary",), vmem_limit_bytes=VMEM_LIMIT),
        name="mixer_prompt" if mask_history else "mixer_sample",
    )(x, conv0, h0, k0, v0, *weights)


def _group_swiglu(x, cols, wgu, wd):
    gu = _bdot(x, wgu)
    h = jax.nn.silu(gu[:, :GROUP_FF]) * gu[:, GROUP_FF:]
    parts = [h[:, e * D_EXPERT:(e + 1) * D_EXPERT] * cols[e] for e in range(EXPERTS_PER_GROUP)]
    return _bdot(jnp.concatenate(parts, axis=1).astype(jnp.bfloat16), wd)


def _moe_groups_kernel(xn2_ref, comb_ref, x1_ref, wgu_ref, wd_ref, norm_ref, y_ref, acc_ref):
    g = pl.program_id(1)

    @pl.when(g == 0)
    def _():
        acc_ref[...] = jnp.zeros_like(acc_ref)

    comb = comb_ref[...]
    lane = lax.broadcasted_iota(jnp.int32, comb.shape, 1)
    cols = [jnp.sum(jnp.where(lane == g * EXPERTS_PER_GROUP + e, comb, 0.0), axis=-1, keepdims=True)
            for e in range(EXPERTS_PER_GROUP)]
    acc_ref[...] += _group_swiglu(xn2_ref[...], cols, wgu_ref[...], wd_ref[...])

    @pl.when(g == N_GROUPS - 1)
    def _():
        y_ref[...] = _rmsnorm(x1_ref[...] + acc_ref[...], norm_ref[...])


def _moe_groups(xn2, comb, x1, wgu, wd, norm_final, *, tile):
    n_tok = xn2.shape[0]
    row_spec = lambda w: pl.BlockSpec((tile, w), lambda i, g: (i, 0))
    return pl.pallas_call(
        _moe_groups_kernel,
        grid=(n_tok // tile, N_GROUPS),
        in_specs=[row_spec(D_MODEL), row_spec(ROUTER_W), row_spec(D_MODEL),
                  pl.BlockSpec((None, D_MODEL, 2 * GROUP_FF), lambda i, g: (g, 0, 0)),
                  pl.BlockSpec((None, GROUP_FF, D_MODEL), lambda i, g: (g, 0, 0)),
                  pl.BlockSpec((1, D_MODEL), lambda i, g: (0, 0))],
        out_specs=row_spec(D_MODEL),
        out_shape=jax.ShapeDtypeStruct((n_tok, D_MODEL), jnp.float32),
        scratch_shapes=[pltpu.VMEM((tile, D_MODEL), jnp.float32)],
        compiler_params=pltpu.CompilerParams(dimension_semantics=("arbitrary", "arbitrary"),
                                             vmem_limit_bytes=VMEM_LIMIT),
        name="moe_groups",
    )(xn2, comb, x1, wgu, wd, norm_final)


def _scatter_rows_kernel(tile, pos_ref, x_ref, rec_ref, xs_ref, recs_ref):
    i = pl.program_id(0)

    @pl.when(i == 0)
    def _():
        xs_ref[...] = jnp.zeros_like(xs_ref)
        recs_ref[...] = jnp.zeros_like(recs_ref)

    def body(j, carry):
        r0 = pl.multiple_of(j * SUBLANES, SUBLANES)
        xb = x_ref[pl.ds(r0, SUBLANES), :]
        rb = rec_ref[pl.ds(r0, SUBLANES), :]
        for k in range(SUBLANES):
            p = pos_ref[i * tile + j * SUBLANES + k]
            xs_ref[pl.ds(p, 1), :] = xb[k:k + 1, :]
            recs_ref[pl.ds(p, 1), :] = rb[k:k + 1, :]
        return carry

    lax.fori_loop(0, tile // SUBLANES, body, 0)


def _scatter_rows(pos, xn2p, rec, n_slots, *, tile):
    n_tok = xn2p.shape[0]
    return pl.pallas_call(
        functools.partial(_scatter_rows_kernel, tile),
        grid_spec=pltpu.PrefetchScalarGridSpec(
            num_scalar_prefetch=1,
            grid=(n_tok // tile,),
            in_specs=[pl.BlockSpec((tile, PACK_W), lambda i, pos: (i, 0)),
                      pl.BlockSpec((tile, ROUTER_W), lambda i, pos: (i, 0))],
            out_specs=[pl.BlockSpec((n_slots, PACK_W), lambda i, pos: (0, 0)),
                       pl.BlockSpec((n_slots, ROUTER_W), lambda i, pos: (0, 0))],
        ),
        out_shape=(jax.ShapeDtypeStruct((n_slots, PACK_W), jnp.uint32),
                   jax.ShapeDtypeStruct((n_slots, ROUTER_W), jnp.float32)),
        compiler_params=pltpu.CompilerParams(dimension_semantics=("arbitrary",), vmem_limit_bytes=VMEM_LIMIT),
        name="moe_scatter",
    )(pos, xn2p, rec)


def _moe_sorted_kernel(tg_ref, xs_ref, recs_ref, wgu_ref, wd_ref, o_ref):
    x = _unpack_bf16_pairs(xs_ref[...]).astype(jnp.bfloat16)
    rec = recs_ref[...]
    cols = [rec[:, e:e + 1] for e in range(EXPERTS_PER_GROUP)]
    o_ref[...] = _pack_bf16_pairs(_group_swiglu(x, cols, wgu_ref[...], wd_ref[...]))


def _moe_sorted(tile_group, xs, recs, wgu, wd, *, tile):
    n_slots = xs.shape[0]
    return pl.pallas_call(
        _moe_sorted_kernel,
        grid_spec=pltpu.PrefetchScalarGridSpec(
            num_scalar_prefetch=1,
            grid=(n_slots // tile,),
            in_specs=[pl.BlockSpec((tile, PACK_W), lambda i, tg: (i, 0)),
                      pl.BlockSpec((tile, ROUTER_W), lambda i, tg: (i, 0)),
                      pl.BlockSpec((None, D_MODEL, 2 * GROUP_FF), lambda i, tg: (tg[i], 0, 0)),
                      pl.BlockSpec((None, GROUP_FF, D_MODEL), lambda i, tg: (tg[i], 0, 0))],
            out_specs=pl.BlockSpec((tile, PACK_W), lambda i, tg: (i, 0)),
        ),
        out_shape=jax.ShapeDtypeStruct((n_slots, PACK_W), jnp.uint32),
        compiler_params=pltpu.CompilerParams(dimension_semantics=("arbitrary",), vmem_limit_bytes=VMEM_LIMIT),
        name="moe_sorted",
    )(tile_group, xs, recs, wgu, wd)


def _gather_norm_kernel(tile, pos_ref, os_ref, x1_ref, norm_ref, y_ref, buf_ref):
    i = pl.program_id(0)

    def body(j, carry):
        r0 = pl.multiple_of(j * SUBLANES, SUBLANES)
        rows = [os_ref[pl.ds(pos_ref[i * tile + j * SUBLANES + k], 1), :] for k in range(SUBLANES)]
        buf_ref[pl.ds(r0, SUBLANES), :] = jnp.concatenate(rows, axis=0)
        return carry

    lax.fori_loop(0, tile // SUBLANES, body, 0)
    y_ref[...] = _rmsnorm(x1_ref[...] + _unpack_bf16_pairs(buf_ref[...]), norm_ref[...])


def _gather_norm(pos, o_sorted, x1, norm_final, *, tile):
    n_tok = x1.shape[0]
    n_slots = o_sorted.shape[0]
    return pl.pallas_call(
        functools.partial(_gather_norm_kernel, tile),
        grid_spec=pltpu.PrefetchScalarGridSpec(
            num_scalar_prefetch=1,
            grid=(n_tok // tile,),
            in_specs=[pl.BlockSpec((n_slots, PACK_W), lambda i, pos: (0, 0), pipeline_mode=pl.Buffered(1)),
                      pl.BlockSpec((tile, D_MODEL), lambda i, pos: (i, 0)),
                      pl.BlockSpec((1, D_MODEL), lambda i, pos: (0, 0))],
            out_specs=pl.BlockSpec((tile, D_MODEL), lambda i, pos: (i, 0)),
            scratch_shapes=[pltpu.VMEM((tile, PACK_W), jnp.uint32)],
        ),
        out_shape=jax.ShapeDtypeStruct((n_tok, D_MODEL), jnp.float32),
        compiler_params=pltpu.CompilerParams(dimension_semantics=("arbitrary",), vmem_limit_bytes=VMEM_LIMIT),
        name="moe_gather_norm",
    )(pos, o_sorted, x1, norm_final)


def _sorted_slots(rec, tile):
    n_tok = rec.shape[0]
    gid = rec[:, GID_LANE].astype(jnp.int32)
    onehot = (gid[:, None] == jnp.arange(N_GROUPS, dtype=jnp.int32)[None, :]).astype(jnp.int32)
    csum = jnp.cumsum(onehot, axis=0)
    rank = jnp.sum((csum - onehot) * onehot, axis=1)
    padded = ((csum[-1] + tile - 1) // tile) * tile
    end = jnp.cumsum(padded)
    pos = jnp.sum(onehot * (end - padded)[None, :], axis=1) + rank
    n_tiles = n_tok // tile + N_GROUPS
    tile_start = jnp.arange(n_tiles, dtype=jnp.int32) * tile
    tile_group = jnp.minimum(jnp.sum((tile_start[:, None] >= end[None, :]).astype(jnp.int32), axis=1), N_GROUPS - 1)
    return pos.astype(jnp.int32), tile_group.astype(jnp.int32), n_tiles * tile


def _rel_bucket(rel):
    nb = REL_BUCKETS // 2
    n = -rel
    ret = jnp.where(n < 0, nb, 0)
    n = jnp.abs(n)
    max_exact = nb // 2
    nf = jnp.maximum(n, 1).astype(jnp.float32)
    large = max_exact + (jnp.log(nf / max_exact) / math.log(REL_MAX_DIST / max_exact) * (nb - max_exact)).astype(jnp.int32)
    large = jnp.minimum(large, nb - 1)
    return ret + jnp.where(n < max_exact, n, large)


def _band_bias(rel_table):
    rel = (jnp.arange(BAND) - HIST)[None, :] - jnp.arange(CHUNK)[:, None]
    bias = rel_table[_rel_bucket(rel)]
    bias = jnp.transpose(bias, (2, 0, 1)).reshape(N_KV, GROUP * CHUNK, BAND)
    return bias.astype(jnp.float32)


def _dup_heads(a):
    a = a.reshape(a.shape[:-1] + (N_KV, HEAD_DIM))
    return jnp.concatenate([a, a], axis=-1).reshape(a.shape[:-2] + (KV2_W,))


def _undup_heads(a):
    a = a.reshape(a.shape[:-1] + (N_KV, LANES))
    return a[..., :HEAD_DIM]


def _block_diag_group(w, j):
    per = RG_GROUP // LRU_BLOCK_W
    blk = w[j * per:(j + 1) * per]
    eye = jnp.eye(per, dtype=w.dtype)
    return jnp.einsum('nde,nm->ndme', blk, eye).reshape(RG_GROUP, RG_GROUP)


def kernel(x_prompt, x_sample, state_lru_h, state_lru_conv, cache_swa_k, cache_swa_v, norm_mix, w_in, b_merge, conv_w, conv_b, w_rg_a, b_rg_a, w_rg_x, b_rg_x, lru_lambda, attn_sink, rel_bias, w_lru_proj, w_attn_proj, w_out, norm_ffn, w_group, b_group, w_router, b_router, w_e_gate, w_e_up, w_e_down, norm_final):
    f32, bf16 = jnp.float32, jnp.bfloat16
    l = 0
    w = w_in[l]
    scale = HEAD_DIM ** -0.5
    w_in2 = jnp.concatenate([
        w[:, :2 * D_RNN],
        w[:, 2 * D_RNN:2 * D_RNN + Q_W] * scale,
        _dup_heads(w[:, 2 * D_RNN + Q_W:2 * D_RNN + Q_W + KV_W]),
        _dup_heads(w[:, 2 * D_RNN + Q_W + KV_W:2 * D_RNN + Q_W + 2 * KV_W]),
        w[:, 2 * D_RNN + Q_W + 2 * KV_W:],
    ], axis=1).astype(bf16)
    w_rg = jnp.stack([
        jnp.concatenate([_block_diag_group(w_rg_a[l], j), _block_diag_group(w_rg_x[l], j)], axis=1)
        for j in range(N_RG_GROUPS)]).astype(bf16)
    w_rt = jnp.concatenate([w_router[l], w_group[l],
                            jnp.zeros((D_MODEL, ROUTER_W - N_EXPERTS - N_GROUPS), f32)], axis=1)
    b_rt = jnp.concatenate([b_router[l], b_group[l], jnp.zeros((ROUTER_W - N_EXPERTS - N_GROUPS,), f32)])[None, :]
    weights = (
        norm_mix[l][None, :], w_in2, b_merge[l][None, :], conv_w[l], conv_b[l][None, :], w_rg,
        b_rg_a[l][None, :], b_rg_x[l][None, :], lru_lambda[l][None, :], attn_sink[l], _band_bias(rel_bias),
        w_lru_proj[l].astype(bf16), w_attn_proj[l].astype(bf16), w_out[l].astype(bf16), norm_ffn[l][None, :],
        w_rt, b_rt,
    )
    def wide(w_e):
        w_e = w_e.reshape(N_GROUPS, EXPERTS_PER_GROUP, D_MODEL, D_EXPERT)
        return jnp.transpose(w_e, (0, 2, 1, 3)).reshape(N_GROUPS, D_MODEL, GROUP_FF)
    wgu = jnp.concatenate([wide(w_e_gate[l]), wide(w_e_up[l])], axis=2).astype(bf16)
    wd = w_e_down[l].reshape(N_GROUPS, GROUP_FF, D_MODEL).astype(bf16)
    nf = norm_final[None, :]

    bp, tp, _ = x_prompt.shape
    assert bp == 1
    xp = x_prompt.reshape(tp, D_MODEL)
    zeros = lambda *s: jnp.zeros(s, f32)
    p_x1, p_xn2, p_comb, p_conv, p_h, p_k, p_v = _mixer(
        xp, zeros(1, SUBLANES, D_RNN), zeros(1, SUBLANES, D_RNN),
        jnp.zeros((1, HIST, KV2_W), bf16), jnp.zeros((1, HIST, KV2_W), bf16), weights,
        nseg=1, seg_len=256, mask_history=True, sparse_out=True)
    pos, tile_group, n_slots = _sorted_slots(p_comb, SORT_TILE)
    x_sorted, rec_sorted = _scatter_rows(pos, p_xn2, p_comb, n_slots, tile=SORT_TILE)
    o_sorted = _moe_sorted(tile_group, x_sorted, rec_sorted, wgu, wd, tile=SORT_TILE)
    y_p = _gather_norm(pos, o_sorted, p_x1, nf, tile=SORT_TILE).reshape(x_prompt.shape)

    bs, ts, _ = x_sample.shape
    xs = x_sample.reshape(bs * ts, D_MODEL)
    conv0 = jnp.pad(state_lru_conv[l], ((0, 0), (SUBLANES - (CONV_W - 1), 0), (0, 0)))
    h0 = jnp.broadcast_to(state_lru_h[l][:, None, :], (bs, SUBLANES, D_RNN))
    ck = cache_swa_k[l].reshape(bs, -1, KV_W)
    cv = cache_swa_v[l].reshape(bs, -1, KV_W)
    s_x1, s_xn2, s_comb, s_conv, s_h, s_k, s_v = _mixer(
        xs, conv0, h0, _dup_heads(ck).astype(bf16), _dup_heads(cv).astype(bf16), weights,
        nseg=bs, seg_len=ts, mask_history=False, sparse_out=False)
    y_s = _moe_groups(s_xn2, s_comb, s_x1, wgu, wd, nf, tile=bs * ts).reshape(x_sample.shape)

    tail = CONV_W - 1
    p_lru_h = p_h[:, 0, :][None]
    p_lru_conv = p_conv[:, SUBLANES - tail:, :][None]
    p_swa_k = _undup_heads(p_k)[None]
    p_swa_v = _undup_heads(p_v)[None]
    s_lru_h = s_h[:, 0, :][None]
    s_lru_conv = s_conv[:, SUBLANES - tail:, :][None]
    win = ck.shape[1]
    s_swa_k = jnp.concatenate([cache_swa_k[l][:, ts:], _undup_heads(s_k)], axis=1)[:, -win:][None]
    s_swa_v = jnp.concatenate([cache_swa_v[l][:, ts:], _undup_heads(s_v)], axis=1)[:, -win:][None]
    return (y_p, y_s, p_lru_h, p_lru_conv, p_swa_k, p_swa_v, s_lru_h, s_lru_conv, s_swa_k, s_swa_v)
```

```python
import functools
import math

import jax
import jax.numpy as jnp
from jax import lax
from jax.experimental import pallas as pl
from jax.experimental.pallas import tpu as pltpu

D_MODEL = 1024
CHUNK = 64
D_RNN = D_MODEL
CONV_W = 4
LRU_BLOCKS = 16
LRU_BLOCK_W = D_RNN // LRU_BLOCKS
LRU_C = 8.0
N_HEADS = 16
N_KV = 4
HEAD_DIM = 64
GROUP = N_HEADS // N_KV
WINDOW = 128
LOOKBACK_CHUNKS = -(-WINDOW // CHUNK)
HIST = LOOKBACK_CHUNKS * CHUNK
BAND = HIST + CHUNK
Q_W = N_HEADS * HEAD_DIM
KV_W = N_KV * HEAD_DIM
REL_BUCKETS = 32
REL_MAX_DIST = 128
N_GROUPS = 4
EXPERTS_PER_GROUP = 8
N_EXPERTS = N_GROUPS * EXPERTS_PER_GROUP
D_EXPERT = 256
EPS = 1e-6
NEG_INF = -1e30

LANES = 128
SUBLANES = 8
KV2_W = N_KV * LANES
RG_GROUP = 256
N_RG_GROUPS = D_RNN // RG_GROUP
ROUTER_W = LANES
VMEM_LIMIT = 56 * 1024 * 1024
PACK_W = D_MODEL // 2
GROUP_FF = EXPERTS_PER_GROUP * D_EXPERT
GID_LANE = EXPERTS_PER_GROUP
SORT_TILE = 512

C_LX, C_LG, C_Q = 0, D_RNN, 2 * D_RNN
C_K = C_Q + Q_W
C_V = C_K + KV_W
C_G = C_V + KV_W


def _rmsnorm(x, g):
    return x * lax.rsqrt(jnp.mean(x * x, axis=-1, keepdims=True) + EPS) * g


def _bdot(a, b):
    return jnp.dot(a, b, preferred_element_type=jnp.float32)


def _pack_bf16_pairs(x):
    lo = lax.bitcast_convert_type(x[:, :PACK_W].astype(jnp.bfloat16).astype(jnp.float32), jnp.uint32)
    hi = lax.bitcast_convert_type(x[:, PACK_W:].astype(jnp.bfloat16).astype(jnp.float32), jnp.uint32)
    return (lo >> 16) | (hi & jnp.uint32(0xFFFF0000))


def _unpack_bf16_pairs(p):
    lo = lax.bitcast_convert_type(p << 16, jnp.float32)
    hi = lax.bitcast_convert_type(p & jnp.uint32(0xFFFF0000), jnp.float32)
    return jnp.concatenate([lo, hi], axis=1)


def _mixer_kernel(nseg, seg_len, mask_history, sparse_out,
                  x_ref, conv0_ref, h0_ref, k0_ref, v0_ref,
                  norm_mix_ref, w_in_ref, w_kv2_ref, b_merge_ref, conv_w_ref, conv_b_ref, w_rg_ref, b_rg_a_ref, b_rg_x_ref,
                  lam_ref, sink_ref, bias_ref, w_lru_ref, w_attn_ref, w_out_ref, norm_ffn_ref, w_rt_ref, b_rt_ref,
                  x1_ref, xn2_ref, comb_ref, conv_out_ref, h_out_ref, k_out_ref, v_out_ref,
                  xp_ref, hc_ref, kbuf_ref, vbuf_ref, a_ref, b_ref, attn_ref):
    step = pl.program_id(0)
    m_rows = nseg * seg_len
    n_chunks = seg_len // CHUNK
    keep = min(seg_len, HIST)

    @pl.when(step == 0)
    def _():
        xp_ref[:, 0:SUBLANES, :] = conv0_ref[...]
        hc_ref[...] = h0_ref[...]
        kbuf_ref[:, 0:HIST, :] = k0_ref[...]
        vbuf_ref[:, 0:HIST, :] = v0_ref[...]

    x = x_ref[...]
    xn = _rmsnorm(x, norm_mix_ref[...]).astype(jnp.bfloat16)

    lru_x = _bdot(xn, w_in_ref[:, C_LX:C_LX + D_RNN])
    conv_w = conv_w_ref[...]
    xc_parts = []
    for s in range(nseg):
        xp_ref[s, SUBLANES:SUBLANES + seg_len, :] = lru_x[s * seg_len:(s + 1) * seg_len, :]
        acc = conv_b_ref[...]
        for j in range(CONV_W):
            off = SUBLANES - (CONV_W - 1) + j
            acc = acc + xp_ref[s, off:off + seg_len, :] * conv_w[j:j + 1, :]
        xc_parts.append(acc)
        conv_out_ref[s] = xp_ref[s, seg_len:seg_len + SUBLANES, :]
        xp_ref[s, 0:SUBLANES, :] = xp_ref[s, seg_len:seg_len + SUBLANES, :]
    xc = xc_parts[0] if nseg == 1 else jnp.concatenate(xc_parts, axis=0)
    xc_b = xc.astype(jnp.bfloat16)

    lam = lam_ref[...]
    log_sig = jnp.minimum(lam, 0.0) - jnp.log1p(jnp.exp(-jnp.abs(lam)))
    c8 = LRU_C * log_sig
    for j in range(N_RG_GROUPS):
        cs = slice(j * RG_GROUP, (j + 1) * RG_GROUP)
        pre = _bdot(xc_b[:, cs], w_rg_ref[j])
        r = jax.nn.sigmoid(pre[:, :RG_GROUP] + b_rg_a_ref[:, cs])
        i = jax.nn.sigmoid(pre[:, RG_GROUP:] + b_rg_x_ref[:, cs])
        log_a = c8[:, cs] * r
        a_ref[:, cs] = jnp.exp(log_a)
        th = jnp.tanh(log_a)
        b_ref[:, cs] = jnp.sqrt(-2.0 * th / (1.0 - th)) * i * xc[:, cs]

    row = lax.broadcasted_iota(jnp.int32, (SUBLANES, D_RNN), 0)
    for s in range(nseg):
        def scan_body(g, carry, s=s):
            r0 = pl.multiple_of(s * seg_len + g * SUBLANES, SUBLANES)
            a = a_ref[pl.ds(r0, SUBLANES), :]
            b = b_ref[pl.ds(r0, SUBLANES), :]
            for d in (1, 2, 4):
                a_s = pltpu.roll(a, d, axis=0)
                b_s = pltpu.roll(b, d, axis=0)
                m = row >= d
                b = jnp.where(m, a * b_s + b, b)
                a = jnp.where(m, a * a_s, a)
            h = a * carry + b
            b_ref[pl.ds(r0, SUBLANES), :] = h
            return jnp.broadcast_to(h[SUBLANES - 1:SUBLANES, :], (SUBLANES, D_RNN))

        h_last = lax.fori_loop(0, seg_len // SUBLANES, scan_body, hc_ref[s])
        hc_ref[s] = h_last
        h_out_ref[s] = h_last

    lru_gate = _bdot(xn, w_in_ref[:, C_LG:C_LG + D_RNN])
    lru_y = (b_ref[...] * jax.nn.gelu(lru_gate)).astype(jnp.bfloat16)
    mixed = _bdot(lru_y, w_lru_ref[...])
    gates = _bdot(xn, w_in_ref[:, C_G:C_G + D_MODEL]) + b_merge_ref[:, 0:D_MODEL]
    mixed = jax.nn.sigmoid(gates) * mixed

    q = (_bdot(xn, w_in_ref[:, C_Q:C_Q + Q_W]) * (HEAD_DIM ** -0.5)).astype(jnp.bfloat16)
    k2 = _bdot(xn, w_kv2_ref[:, 0:KV2_W])
    v2 = _bdot(xn, w_kv2_ref[:, KV2_W:2 * KV2_W])
    lane_q = lax.broadcasted_iota(jnp.int32, (CHUNK, LANES), 1)
    lane_o = lax.broadcasted_iota(jnp.int32, (CHUNK, LANES), 1)
    key_lane = lax.broadcasted_iota(jnp.int32, (1, BAND), 1)
    row_grp = lax.broadcasted_iota(jnp.int32, (GROUP * CHUNK, 1), 0) // CHUNK
    for s in range(nseg):
        rows = slice(s * seg_len, (s + 1) * seg_len)
        kbuf_ref[s, HIST:HIST + seg_len, :] = k2[rows].astype(jnp.bfloat16)
        vbuf_ref[s, HIST:HIST + seg_len, :] = v2[rows].astype(jnp.bfloat16)
        k_out_ref[s] = k2[s * seg_len + seg_len - keep:(s + 1) * seg_len]
        v_out_ref[s] = v2[s * seg_len + seg_len - keep:(s + 1) * seg_len]
        for c in range(n_chunks):
            q_c = q[s * seg_len + c * CHUNK:s * seg_len + (c + 1) * CHUNK]
            slabs = []
            for kv in range(N_KV):
                parts = []
                for g in range(GROUP):
                    col = kv * GROUP * HEAD_DIM + (g // 2) * LANES
                    slab = q_c[:, col:col + LANES]
                    keep_lo = (g % 2) == 0
                    sel = (lane_q < HEAD_DIM) if keep_lo else (lane_q >= HEAD_DIM)
                    parts.append(jnp.where(sel, slab, jnp.zeros_like(slab)))
                q_stack = jnp.concatenate(parts, axis=0)
                k_band = kbuf_ref[s, c * CHUNK:c * CHUNK + BAND, kv * LANES:(kv + 1) * LANES]
                v_band = vbuf_ref[s, c * CHUNK:c * CHUNK + BAND, kv * LANES:(kv + 1) * LANES]
                sc = lax.dot_general(q_stack, k_band, (((1,), (1,)), ((), ())),
                                     preferred_element_type=jnp.float32)
                sc = sc + bias_ref[kv]
                if mask_history:
                    first_valid = HIST - (step * n_chunks + c) * CHUNK
                    sc = jnp.where(key_lane >= first_valid, sc, NEG_INF)
                sink = jnp.zeros((GROUP * CHUNK, 1), jnp.float32)
                for g in range(GROUP):
                    sink = jnp.where(row_grp == g, sink_ref[kv * GROUP + g], sink)
                m = jnp.maximum(jnp.max(sc, axis=-1, keepdims=True), sink)
                p = jnp.exp(sc - m)
                denom = jnp.sum(p, axis=-1, keepdims=True) + jnp.exp(sink - m)
                p = (p / denom).astype(jnp.bfloat16)
                o = _bdot(p, v_band)
                for pair in range(GROUP // 2):
                    lo = o[(2 * pair) * CHUNK:(2 * pair + 1) * CHUNK]
                    hi = o[(2 * pair + 1) * CHUNK:(2 * pair + 2) * CHUNK]
                    slabs.append(jnp.where(lane_o < HEAD_DIM, lo, hi))
            attn_ref[s * seg_len + c * CHUNK:s * seg_len + (c + 1) * CHUNK, :] = (
                jnp.concatenate(slabs, axis=1).astype(jnp.bfloat16))
        kbuf_ref[s, 0:HIST, :] = kbuf_ref[s, seg_len:seg_len + HIST, :]
        vbuf_ref[s, 0:HIST, :] = vbuf_ref[s, seg_len:seg_len + HIST, :]

    attn = _bdot(attn_ref[...], w_attn_ref[...])
    gates = _bdot(xn, w_in_ref[:, C_G + D_MODEL:C_G + 2 * D_MODEL]) + b_merge_ref[:, D_MODEL:2 * D_MODEL]
    mixed = (mixed + jax.nn.sigmoid(gates) * attn).astype(jnp.bfloat16)
    x1 = x + _bdot(mixed, w_out_ref[...])
    x1_ref[...] = x1

    xn2 = _rmsnorm(x1, norm_ffn_ref[...])
    xn2_ref[...] = _pack_bf16_pairs(xn2) if sparse_out else xn2.astype(jnp.bfloat16)
    logits = jnp.dot(xn2, w_rt_ref[...], precision=lax.Precision.HIGHEST,
                     preferred_element_type=jnp.float32) + b_rt_ref[...]
    lane = lax.broadcasted_iota(jnp.int32, (m_rows, ROUTER_W), 1).astype(jnp.float32)
    far = jnp.float32(2 * ROUTER_W)
    is_group = (lane >= N_EXPERTS) & (lane < N_EXPERTS + N_GROUPS)
    gl = jnp.where(is_group, logits, NEG_INF)
    g_max = jnp.max(gl, axis=-1, keepdims=True)
    g_idx = jnp.min(jnp.where(gl == g_max, lane, far), axis=-1, keepdims=True) - N_EXPERTS
    g_w = 1.0 / jnp.sum(jnp.where(is_group, jnp.exp(gl - g_max), 0.0), axis=-1, keepdims=True)
    in_group = (lane < N_EXPERTS) & (jnp.floor(lane * (1.0 / EXPERTS_PER_GROUP)) == g_idx)
    el = jnp.where(in_group, logits, NEG_INF)
    e1 = jnp.max(el, axis=-1, keepdims=True)
    i1 = jnp.min(jnp.where(el == e1, lane, far), axis=-1, keepdims=True)
    el2 = jnp.where(lane == i1, NEG_INF, el)
    e2 = jnp.max(el2, axis=-1, keepdims=True)
    i2 = jnp.min(jnp.where(el2 == e2, lane, far), axis=-1, keepdims=True)
    t = jnp.exp(e2 - e1)
    w1 = g_w / (1.0 + t)
    w2 = w1 * t
    if sparse_out:
        e_lane = lane + g_idx * EXPERTS_PER_GROUP
        rec = jnp.where(e_lane == i1, w1, 0.0) + jnp.where(e_lane == i2, w2, 0.0)
        rec = jnp.where(lane < EXPERTS_PER_GROUP, rec, 0.0)
        comb_ref[...] = jnp.where(lane == GID_LANE, g_idx, rec)
    else:
        comb_ref[...] = jnp.where(lane == i1, w1, 0.0) + jnp.where(lane == i2, w2, 0.0)


def _const_spec(shape):
    zeros = (0,) * len(shape)
    return pl.BlockSpec(shape, lambda i: zeros, pipeline_mode=pl.Buffered(1))


def _mixer(x, conv0, h0, k0, v0, weights, *, nseg, seg_len, mask_history, sparse_out):
    n_tok = x.shape[0]
    m_rows = nseg * seg_len
    n_steps = n_tok // m_rows
    keep = min(seg_len, HIST)
    row_spec = lambda w: pl.BlockSpec((m_rows, w), lambda i: (i, 0))
    in_specs = [row_spec(D_MODEL), _const_spec(conv0.shape), _const_spec(h0.shape), _const_spec(k0.shape),
                _const_spec(v0.shape)]
    for w in weights:
        if w.ndim == 1:
            in_specs.append(pl.BlockSpec(memory_space=pltpu.SMEM))
        else:
            in_specs.append(_const_spec(w.shape))
    state_spec = lambda r, w: pl.BlockSpec((nseg, r, w), lambda i: (0, 0, 0))
    out_shape = (
        jax.ShapeDtypeStruct((n_tok, D_MODEL), jnp.float32),
        (jax.ShapeDtypeStruct((n_tok, PACK_W), jnp.uint32) if sparse_out
         else jax.ShapeDtypeStruct((n_tok, D_MODEL), jnp.bfloat16)),
        jax.ShapeDtypeStruct((n_tok, ROUTER_W), jnp.float32),
        jax.ShapeDtypeStruct((nseg, SUBLANES, D_RNN), jnp.float32),
        jax.ShapeDtypeStruct((nseg, SUBLANES, D_RNN), jnp.float32),
        jax.ShapeDtypeStruct((nseg, keep, KV2_W), jnp.float32),
        jax.ShapeDtypeStruct((nseg, keep, KV2_W), jnp.float32),
    )
    out_specs = (row_spec(D_MODEL), row_spec(PACK_W if sparse_out else D_MODEL), row_spec(ROUTER_W),
                 state_spec(SUBLANES, D_RNN), state_spec(SUBLANES, D_RNN),
                 state_spec(keep, KV2_W), state_spec(keep, KV2_W))
    scratch = [
        pltpu.VMEM((nseg, SUBLANES + seg_len, D_RNN), jnp.float32),
        pltpu.VMEM((nseg, SUBLANES, D_RNN), jnp.float32),
        pltpu.VMEM((nseg, HIST + seg_len, KV2_W), jnp.bfloat16),
        pltpu.VMEM((nseg, HIST + seg_len, KV2_W), jnp.bfloat16),
        pltpu.VMEM((m_rows, D_RNN), jnp.float32),
        pltpu.VMEM((m_rows, D_RNN), jnp.float32),
        pltpu.VMEM((m_rows, Q_W), jnp.bfloat16),
    ]
    return pl.pallas_call(
        functools.partial(_mixer_kernel, nseg, seg_len, mask_history, sparse_out),
        grid=(n_steps,),
        in_specs=in_specs,
        out_specs=out_specs,
        out_shape=out_shape,
        scratch_shapes=scratch,
        compiler_params=pltpu.CompilerParams(dimension_semantics=("arbitrary",), vmem_limit_bytes=VMEM_LIMIT),
        name="mixer_prompt" if mask_history else "mixer_sample",
    )(x, conv0, h0, k0, v0, *weights)


def _group_swiglu(x, cols, wg_ref, wu_ref, wd_ref):
    parts = []
    for e in range(EXPERTS_PER_GROUP):
        h = jax.nn.silu(_bdot(x, wg_ref[e])) * _bdot(x, wu_ref[e])
        parts.append((h * cols[e]).astype(jnp.bfloat16))
    return _bdot(jnp.concatenate(parts, axis=1), wd_ref[...])


def _moe_groups_kernel(xn2_ref, comb_ref, x1_ref, wg_ref, wu_ref, wd_ref, norm_ref, y_ref, acc_ref):
    g = pl.program_id(1)

    @pl.when(g == 0)
    def _():
        acc_ref[...] = jnp.zeros_like(acc_ref)

    comb = comb_ref[...]
    lane = lax.broadcasted_iota(jnp.int32, comb.shape, 1)
    cols = [jnp.sum(jnp.where(lane == g * EXPERTS_PER_GROUP + e, comb, 0.0), axis=-1, keepdims=True)
            for e in range(EXPERTS_PER_GROUP)]
    acc_ref[...] += _group_swiglu(xn2_ref[...], cols, wg_ref, wu_ref, wd_ref)

    @pl.when(g == N_GROUPS - 1)
    def _():
        y_ref[...] = _rmsnorm(x1_ref[...] + acc_ref[...], norm_ref[...])


def _group_weight_specs(group_of):
    return [pl.BlockSpec((EXPERTS_PER_GROUP, D_MODEL, D_EXPERT), lambda *a: (group_of(*a), 0, 0)),
            pl.BlockSpec((EXPERTS_PER_GROUP, D_MODEL, D_EXPERT), lambda *a: (group_of(*a), 0, 0)),
            pl.BlockSpec((None, GROUP_FF, D_MODEL), lambda *a: (group_of(*a), 0, 0))]


def _moe_groups(xn2, comb, x1, wg, wu, wd, norm_final, *, tile):
    n_tok = xn2.shape[0]
    row_spec = lambda w: pl.BlockSpec((tile, w), lambda i, g: (i, 0))
    return pl.pallas_call(
        _moe_groups_kernel,
        grid=(n_tok // tile, N_GROUPS),
        in_specs=[row_spec(D_MODEL), row_spec(ROUTER_W), row_spec(D_MODEL),
                  *_group_weight_specs(lambda i, g: g),
                  pl.BlockSpec((1, D_MODEL), lambda i, g: (0, 0))],
        out_specs=row_spec(D_MODEL),
        out_shape=jax.ShapeDtypeStruct((n_tok, D_MODEL), jnp.float32),
        scratch_shapes=[pltpu.VMEM((tile, D_MODEL), jnp.float32)],
        compiler_params=pltpu.CompilerParams(dimension_semantics=("arbitrary", "arbitrary"),
                                             vmem_limit_bytes=VMEM_LIMIT),
        name="moe_groups",
    )(xn2, comb, x1, wg, wu, wd, norm_final)


def _scatter_rows_kernel(tile, pos_ref, x_ref, rec_ref, xs_ref, recs_ref):
    i = pl.program_id(0)

    @pl.when(i == 0)
    def _():
        xs_ref[...] = jnp.zeros_like(xs_ref)
        recs_ref[...] = jnp.zeros_like(recs_ref)

    def body(j, carry):
        r0 = pl.multiple_of(j * SUBLANES, SUBLANES)
        xb = x_ref[pl.ds(r0, SUBLANES), :]
        rb = rec_ref[pl.ds(r0, SUBLANES), :]
        for k in range(SUBLANES):
            p = pos_ref[i * tile + j * SUBLANES + k]
            xs_ref[pl.ds(p, 1), :] = xb[k:k + 1, :]
            recs_ref[pl.ds(p, 1), :] = rb[k:k + 1, :]
        return carry

    lax.fori_loop(0, tile // SUBLANES, body, 0)


def _scatter_rows(pos, xn2p, rec, n_slots, *, tile):
    n_tok = xn2p.shape[0]
    return pl.pallas_call(
        functools.partial(_scatter_rows_kernel, tile),
        grid_spec=pltpu.PrefetchScalarGridSpec(
            num_scalar_prefetch=1,
            grid=(n_tok // tile,),
            in_specs=[pl.BlockSpec((tile, PACK_W), lambda i, pos: (i, 0)),
                      pl.BlockSpec((tile, ROUTER_W), lambda i, pos: (i, 0))],
            out_specs=[pl.BlockSpec((n_slots, PACK_W), lambda i, pos: (0, 0)),
                       pl.BlockSpec((n_slots, ROUTER_W), lambda i, pos: (0, 0))],
        ),
        out_shape=(jax.ShapeDtypeStruct((n_slots, PACK_W), jnp.uint32),
                   jax.ShapeDtypeStruct((n_slots, ROUTER_W), jnp.float32)),
        compiler_params=pltpu.CompilerParams(dimension_semantics=("arbitrary",), vmem_limit_bytes=VMEM_LIMIT),
        name="moe_scatter",
    )(pos, xn2p, rec)


def _moe_sorted_kernel(tg_ref, xs_ref, recs_ref, wg_ref, wu_ref, wd_ref, o_ref):
    x = _unpack_bf16_pairs(xs_ref[...]).astype(jnp.bfloat16)
    rec = recs_ref[...]
    cols = [rec[:, e:e + 1] for e in range(EXPERTS_PER_GROUP)]
    o_ref[...] = _pack_bf16_pairs(_group_swiglu(x, cols, wg_ref, wu_ref, wd_ref))


def _moe_sorted(tile_group, xs, recs, wg, wu, wd, *, tile):
    n_slots = xs.shape[0]
    return pl.pallas_call(
        _moe_sorted_kernel,
        grid_spec=pltpu.PrefetchScalarGridSpec(
            num_scalar_prefetch=1,
            grid=(n_slots // tile,),
            in_specs=[pl.BlockSpec((tile, PACK_W), lambda i, tg: (i, 0)),
                      pl.BlockSpec((tile, ROUTER_W), lambda i, tg: (i, 0)),
                      *_group_weight_specs(lambda i, tg: tg[i])],
            out_specs=pl.BlockSpec((tile, PACK_W), lambda i, tg: (i, 0)),
        ),
        out_shape=jax.ShapeDtypeStruct((n_slots, PACK_W), jnp.uint32),
        compiler_params=pltpu.CompilerParams(dimension_semantics=("arbitrary",), vmem_limit_bytes=VMEM_LIMIT),
        name="moe_sorted",
    )(tile_group, xs, recs, wg, wu, wd)


def _gather_norm_kernel(tile, pos_ref, os_ref, x1_ref, norm_ref, y_ref, buf_ref):
    i = pl.program_id(0)

    def body(j, carry):
        r0 = pl.multiple_of(j * SUBLANES, SUBLANES)
        rows = [os_ref[pl.ds(pos_ref[i * tile + j * SUBLANES + k], 1), :] for k in range(SUBLANES)]
        buf_ref[pl.ds(r0, SUBLANES), :] = jnp.concatenate(rows, axis=0)
        return carry

    lax.fori_loop(0, tile // SUBLANES, body, 0)
    y_ref[...] = _rmsnorm(x1_ref[...] + _unpack_bf16_pairs(buf_ref[...]), norm_ref[...])


def _gather_norm(pos, o_sorted, x1, norm_final, *, tile):
    n_tok = x1.shape[0]
    n_slots = o_sorted.shape[0]
    return pl.pallas_call(
        functools.partial(_gather_norm_kernel, tile),
        grid_spec=pltpu.PrefetchScalarGridSpec(
            num_scalar_prefetch=1,
            grid=(n_tok // tile,),
            in_specs=[pl.BlockSpec((n_slots, PACK_W), lambda i, pos: (0, 0), pipeline_mode=pl.Buffered(1)),
                      pl.BlockSpec((tile, D_MODEL), lambda i, pos: (i, 0)),
                      pl.BlockSpec((1, D_MODEL), lambda i, pos: (0, 0))],
            out_specs=pl.BlockSpec((tile, D_MODEL), lambda i, pos: (i, 0)),
            scratch_shapes=[pltpu.VMEM((tile, PACK_W), jnp.uint32)],
        ),
        out_shape=jax.ShapeDtypeStruct((n_tok, D_MODEL), jnp.float32),
        compiler_params=pltpu.CompilerParams(dimension_semantics=("arbitrary",), vmem_limit_bytes=VMEM_LIMIT),
        name="moe_gather_norm",
    )(pos, o_sorted, x1, norm_final)


def _sorted_slots(rec, tile):
    n_tok = rec.shape[0]
    gid = rec[:, GID_LANE].astype(jnp.int32)
    onehot = (gid[:, None] == jnp.arange(N_GROUPS, dtype=jnp.int32)[None, :]).astype(jnp.int32)
    csum = jnp.cumsum(onehot, axis=0)
    rank = jnp.sum((csum - onehot) * onehot, axis=1)
    padded = ((csum[-1] + tile - 1) // tile) * tile
    end = jnp.cumsum(padded)
    pos = jnp.sum(onehot * (end - padded)[None, :], axis=1) + rank
    n_tiles = n_tok // tile + N_GROUPS
    tile_start = jnp.arange(n_tiles, dtype=jnp.int32) * tile
    tile_group = jnp.minimum(jnp.sum((tile_start[:, None] >= end[None, :]).astype(jnp.int32), axis=1), N_GROUPS - 1)
    return pos.astype(jnp.int32), tile_group.astype(jnp.int32), n_tiles * tile


def _rel_bucket(rel):
    nb = REL_BUCKETS // 2
    n = -rel
    ret = jnp.where(n < 0, nb, 0)
    n = jnp.abs(n)
    max_exact = nb // 2
    nf = jnp.maximum(n, 1).astype(jnp.float32)
    large = max_exact + (jnp.log(nf / max_exact) / math.log(REL_MAX_DIST / max_exact) * (nb - max_exact)).astype(jnp.int32)
    large = jnp.minimum(large, nb - 1)
    return ret + jnp.where(n < max_exact, n, large)


def _band_bias(rel_table):
    rel = (jnp.arange(BAND) - HIST)[None, :] - jnp.arange(CHUNK)[:, None]
    bucket = _rel_bucket(rel)
    bias = jnp.zeros((N_HEADS, CHUNK, BAND), jnp.float32)
    for b in range(REL_BUCKETS):
        bias = jnp.where((bucket == b)[None], rel_table[b].astype(jnp.float32)[:, None, None], bias)
    return bias.reshape(N_KV, GROUP * CHUNK, BAND)


def _dup_heads(a):
    a = a.reshape(a.shape[:-1] + (N_KV, HEAD_DIM))
    return jnp.concatenate([a, a], axis=-1).reshape(a.shape[:-2] + (KV2_W,))


def _undup_heads(a):
    a = a.reshape(a.shape[:-1] + (N_KV, LANES))
    return a[..., :HEAD_DIM]


def _block_diag_group(w, j):
    per = RG_GROUP // LRU_BLOCK_W
    blk = w[j * per:(j + 1) * per]
    eye = jnp.eye(per, dtype=w.dtype)
    return jnp.einsum('nde,nm->ndme', blk, eye).reshape(RG_GROUP, RG_GROUP)


def kernel(x_prompt, x_sample, state_lru_h, state_lru_conv, cache_swa_k, cache_swa_v, norm_mix, w_in, b_merge, conv_w, conv_b, w_rg_a, b_rg_a, w_rg_x, b_rg_x, lru_lambda, attn_sink, rel_bias, w_lru_proj, w_attn_proj, w_out, norm_ffn, w_group, b_group, w_router, b_router, w_e_gate, w_e_up, w_e_down, norm_final):
    f32, bf16 = jnp.float32, jnp.bfloat16
    l = 0
    w = w_in[l]
    w_kv2 = jnp.concatenate([_dup_heads(w[:, C_K:C_K + KV_W]), _dup_heads(w[:, C_V:C_V + KV_W])], axis=1).astype(bf16)
    w_rg = jnp.stack([
        jnp.concatenate([_block_diag_group(w_rg_a[l], j), _block_diag_group(w_rg_x[l], j)], axis=1)
        for j in range(N_RG_GROUPS)]).astype(bf16)
    w_rt = jnp.concatenate([w_router[l], w_group[l],
                            jnp.zeros((D_MODEL, ROUTER_W - N_EXPERTS - N_GROUPS), f32)], axis=1)
    b_rt = jnp.concatenate([b_router[l], b_group[l], jnp.zeros((ROUTER_W - N_EXPERTS - N_GROUPS,), f32)])[None, :]
    weights = (
        norm_mix[l][None, :], w.astype(bf16), w_kv2, b_merge[l][None, :], conv_w[l], conv_b[l][None, :], w_rg,
        b_rg_a[l][None, :], b_rg_x[l][None, :], lru_lambda[l][None, :], attn_sink[l], _band_bias(rel_bias),
        w_lru_proj[l].astype(bf16), w_attn_proj[l].astype(bf16), w_out[l].astype(bf16), norm_ffn[l][None, :],
        w_rt, b_rt,
    )
    wg, wu = w_e_gate[l].astype(bf16), w_e_up[l].astype(bf16)
    wd = w_e_down[l].astype(bf16).reshape(N_GROUPS, GROUP_FF, D_MODEL)
    nf = norm_final[None, :]

    bp, tp, _ = x_prompt.shape
    assert bp == 1
    xp = x_prompt.reshape(tp, D_MODEL)
    zeros = lambda *s: jnp.zeros(s, f32)
    p_x1, p_xn2, p_comb, p_conv, p_h, p_k, p_v = _mixer(
        xp, zeros(1, SUBLANES, D_RNN), zeros(1, SUBLANES, D_RNN),
        jnp.zeros((1, HIST, KV2_W), bf16), jnp.zeros((1, HIST, KV2_W), bf16), weights,
        nseg=1, seg_len=256, mask_history=True, sparse_out=True)
    pos, tile_group, n_slots = _sorted_slots(p_comb, SORT_TILE)
    x_sorted, rec_sorted = _scatter_rows(pos, p_xn2, p_comb, n_slots, tile=SORT_TILE)
    o_sorted = _moe_sorted(tile_group, x_sorted, rec_sorted, wg, wu, wd, tile=SORT_TILE)
    y_p = _gather_norm(pos, o_sorted, p_x1, nf, tile=SORT_TILE).reshape(x_prompt.shape)

    bs, ts, _ = x_sample.shape
    xs = x_sample.reshape(bs * ts, D_MODEL)
    conv0 = jnp.pad(state_lru_conv[l], ((0, 0), (SUBLANES - (CONV_W - 1), 0), (0, 0)))
    h0 = jnp.broadcast_to(state_lru_h[l][:, None, :], (bs, SUBLANES, D_RNN))
    ck = cache_swa_k[l].reshape(bs, -1, KV_W)
    cv = cache_swa_v[l].reshape(bs, -1, KV_W)
    s_x1, s_xn2, s_comb, s_conv, s_h, s_k, s_v = _mixer(
        xs, conv0, h0, _dup_heads(ck).astype(bf16), _dup_heads(cv).astype(bf16), weights,
        nseg=bs, seg_len=ts, mask_history=False, sparse_out=False)
    y_s = _moe_groups(s_xn2, s_comb, s_x1, wg, wu, wd, nf, tile=bs * ts).reshape(x_sample.shape)

    tail = CONV_W - 1
    p_lru_h = p_h[:, 0, :][None]
    p_lru_conv = p_conv[:, SUBLANES - tail:, :][None]
    p_swa_k = _undup_heads(p_k)[None]
    p_swa_v = _undup_heads(p_v)[None]
    s_lru_h = s_h[:, 0, :][None]
    s_lru_conv = s_conv[:, SUBLANES - tail:, :][None]
    win = ck.shape[1]
    s_swa_k = jnp.concatenate([cache_swa_k[l][:, ts:], _undup_heads(s_k)], axis=1)[:, -win:][None]
    s_swa_v = jnp.concatenate([cache_swa_v[l][:, ts:], _undup_heads(s_v)], axis=1)[:, -win:][None]
    return (y_p, y_s, p_lru_h, p_lru_conv, p_swa_k, p_swa_v, s_lru_h, s_lru_conv, s_swa_k, s_swa_v)
```

```python
import functools
import math

import jax
import jax.numpy as jnp
from jax import lax
from jax.experimental import pallas as pl
from jax.experimental.pallas import tpu as pltpu

D_MODEL = 1024
CHUNK = 64
D_RNN = D_MODEL
CONV_W = 4
LRU_BLOCKS = 16
LRU_BLOCK_W = D_RNN // LRU_BLOCKS
LRU_C = 8.0
N_HEADS = 16
N_KV = 4
HEAD_DIM = 64
GROUP = N_HEADS // N_KV
WINDOW = 128
LOOKBACK_CHUNKS = -(-WINDOW // CHUNK)
HIST = LOOKBACK_CHUNKS * CHUNK
BAND = HIST + CHUNK
Q_W = N_HEADS * HEAD_DIM
KV_W = N_KV * HEAD_DIM
REL_BUCKETS = 32
REL_MAX_DIST = 128
N_GROUPS = 4
EXPERTS_PER_GROUP = 8
N_EXPERTS = N_GROUPS * EXPERTS_PER_GROUP
D_EXPERT = 256
EPS = 1e-6
NEG_INF = -1e30

LANES = 128
SUBLANES = 8
KV2_W = N_KV * LANES
RG_GROUP = 256
N_RG_GROUPS = D_RNN // RG_GROUP
ROUTER_W = LANES
VMEM_LIMIT = 56 * 1024 * 1024
PACK_W = D_MODEL // 2
GROUP_FF = EXPERTS_PER_GROUP * D_EXPERT
GID_LANE = EXPERTS_PER_GROUP
SORT_TILE = 512

C_LX, C_LG, C_Q = 0, D_RNN, 2 * D_RNN
C_K = C_Q + Q_W
C_V = C_K + KV_W
C_G = C_V + KV_W


def _rmsnorm(x, g):
    return x * lax.rsqrt(jnp.mean(x * x, axis=-1, keepdims=True) + EPS) * g


def _bdot(a, b):
    return jnp.dot(a, b, preferred_element_type=jnp.float32)


def _sigmoid(x):
    return 0.5 * jnp.tanh(0.5 * x) + 0.5


def _pack_bf16_pairs(x):
    return pltpu.pack_elementwise([x[:, :PACK_W], x[:, PACK_W:]], packed_dtype=jnp.bfloat16)


def _unpack_bf16_pairs(p):
    lo = pltpu.unpack_elementwise(p, index=0, packed_dtype=jnp.bfloat16, unpacked_dtype=jnp.float32)
    hi = pltpu.unpack_elementwise(p, index=1, packed_dtype=jnp.bfloat16, unpacked_dtype=jnp.float32)
    return jnp.concatenate([lo, hi], axis=1)


def _mixer_kernel(nseg, seg_len, mask_history, sparse_out,
                  x_ref, conv0_ref, h0_ref, k0_ref, v0_ref,
                  norm_mix_ref, w_in_ref, w_kv2_ref, b_merge_ref, conv_w_ref, conv_b_ref, w_rg_ref, b_rg_a_ref, b_rg_x_ref,
                  lam_ref, sink_ref, bias_ref, w_lru_ref, w_attn_ref, w_out_ref, norm_ffn_ref, w_rt_ref, b_rt_ref,
                  x1_ref, xn2_ref, comb_ref, conv_out_ref, h_out_ref, k_out_ref, v_out_ref,
                  xp_ref, hc_ref, kbuf_ref, vbuf_ref, a_ref, b_ref, attn_ref):
    step = pl.program_id(0)
    m_rows = nseg * seg_len
    n_chunks = seg_len // CHUNK
    keep = min(seg_len, HIST)

    @pl.when(step == 0)
    def _():
        xp_ref[:, 0:SUBLANES, :] = conv0_ref[...]
        hc_ref[...] = h0_ref[...]
        kbuf_ref[:, 0:HIST, :] = k0_ref[...]
        vbuf_ref[:, 0:HIST, :] = v0_ref[...]

    x = x_ref[...]
    xn = _rmsnorm(x, norm_mix_ref[...]).astype(jnp.bfloat16)

    lru_x = _bdot(xn, w_in_ref[:, C_LX:C_LX + D_RNN])
    conv_w = conv_w_ref[...]
    xc_parts = []
    for s in range(nseg):
        xp_ref[s, SUBLANES:SUBLANES + seg_len, :] = lru_x[s * seg_len:(s + 1) * seg_len, :]
        acc = conv_b_ref[...]
        for j in range(CONV_W):
            off = SUBLANES - (CONV_W - 1) + j
            acc = acc + xp_ref[s, off:off + seg_len, :] * conv_w[j:j + 1, :]
        xc_parts.append(acc)
        conv_out_ref[s] = xp_ref[s, seg_len:seg_len + SUBLANES, :]
        xp_ref[s, 0:SUBLANES, :] = xp_ref[s, seg_len:seg_len + SUBLANES, :]
    xc = xc_parts[0] if nseg == 1 else jnp.concatenate(xc_parts, axis=0)
    xc_b = xc.astype(jnp.bfloat16)

    lam = lam_ref[...]
    log_sig = jnp.minimum(lam, 0.0) - jnp.log1p(jnp.exp(-jnp.abs(lam)))
    c8 = LRU_C * log_sig
    for j in range(N_RG_GROUPS):
        cs = slice(j * RG_GROUP, (j + 1) * RG_GROUP)
        pre = _bdot(xc_b[:, cs], w_rg_ref[j])
        r = _sigmoid(pre[:, :RG_GROUP] + b_rg_a_ref[:, cs])
        i = _sigmoid(pre[:, RG_GROUP:] + b_rg_x_ref[:, cs])
        log_a = c8[:, cs] * r
        a_ref[:, cs] = jnp.exp(log_a)
        th = jnp.tanh(log_a)
        b_ref[:, cs] = jnp.sqrt(-2.0 * th / (1.0 - th)) * i * xc[:, cs]

    row = lax.broadcasted_iota(jnp.int32, (SUBLANES, D_RNN), 0)
    for s in range(nseg):
        def scan_body(g, carry, s=s):
            r0 = pl.multiple_of(s * seg_len + g * SUBLANES, SUBLANES)
            a = a_ref[pl.ds(r0, SUBLANES), :]
            b = b_ref[pl.ds(r0, SUBLANES), :]
            for d in (1, 2, 4):
                a_s = pltpu.roll(a, d, axis=0)
                b_s = pltpu.roll(b, d, axis=0)
                m = row >= d
                b = jnp.where(m, a * b_s + b, b)
                a = jnp.where(m, a * a_s, a)
            h = a * carry + b
            b_ref[pl.ds(r0, SUBLANES), :] = h
            return jnp.broadcast_to(h[SUBLANES - 1:SUBLANES, :], (SUBLANES, D_RNN))

        h_last = lax.fori_loop(0, seg_len // SUBLANES, scan_body, hc_ref[s])
        hc_ref[s] = h_last
        h_out_ref[s] = h_last

    lru_gate = _bdot(xn, w_in_ref[:, C_LG:C_LG + D_RNN])
    lru_y = (b_ref[...] * jax.nn.gelu(lru_gate)).astype(jnp.bfloat16)
    mixed = _bdot(lru_y, w_lru_ref[...])
    gates = _bdot(xn, w_in_ref[:, C_G:C_G + D_MODEL]) + b_merge_ref[:, 0:D_MODEL]
    mixed = _sigmoid(gates) * mixed

    q = (_bdot(xn, w_in_ref[:, C_Q:C_Q + Q_W]) * (HEAD_DIM ** -0.5)).astype(jnp.bfloat16)
    k2 = _bdot(xn, w_kv2_ref[:, 0:KV2_W])
    v2 = _bdot(xn, w_kv2_ref[:, KV2_W:2 * KV2_W])
    lane_q = lax.broadcasted_iota(jnp.int32, (CHUNK, LANES), 1)
    lane_o = lax.broadcasted_iota(jnp.int32, (CHUNK, LANES), 1)
    key_lane = lax.broadcasted_iota(jnp.int32, (1, BAND), 1)
    row_grp = lax.broadcasted_iota(jnp.int32, (GROUP * CHUNK, 1), 0) // CHUNK
    for s in range(nseg):
        rows = slice(s * seg_len, (s + 1) * seg_len)
        kbuf_ref[s, HIST:HIST + seg_len, :] = k2[rows].astype(jnp.bfloat16)
        vbuf_ref[s, HIST:HIST + seg_len, :] = v2[rows].astype(jnp.bfloat16)
        k_out_ref[s] = k2[s * seg_len + seg_len - keep:(s + 1) * seg_len]
        v_out_ref[s] = v2[s * seg_len + seg_len - keep:(s + 1) * seg_len]
        for c in range(n_chunks):
            q_c = q[s * seg_len + c * CHUNK:s * seg_len + (c + 1) * CHUNK]
            slabs = []
            for kv in range(N_KV):
                parts = []
                for g in range(GROUP):
                    col = kv * GROUP * HEAD_DIM + (g // 2) * LANES
                    slab = q_c[:, col:col + LANES]
                    keep_lo = (g % 2) == 0
                    sel = (lane_q < HEAD_DIM) if keep_lo else (lane_q >= HEAD_DIM)
                    parts.append(jnp.where(sel, slab, jnp.zeros_like(slab)))
                q_stack = jnp.concatenate(parts, axis=0)
                k_band = kbuf_ref[s, c * CHUNK:c * CHUNK + BAND, kv * LANES:(kv + 1) * LANES]
                v_band = vbuf_ref[s, c * CHUNK:c * CHUNK + BAND, kv * LANES:(kv + 1) * LANES]
                sc = lax.dot_general(q_stack, k_band, (((1,), (1,)), ((), ())),
                                     preferred_element_type=jnp.float32)
                sc = sc + bias_ref[kv]
                if mask_history:
                    first_valid = HIST - (step * n_chunks + c) * CHUNK
                    sc = jnp.where(key_lane >= first_valid, sc, NEG_INF)
                sink = jnp.zeros((GROUP * CHUNK, 1), jnp.float32)
                for g in range(GROUP):
                    sink = jnp.where(row_grp == g, sink_ref[kv * GROUP + g], sink)
                m = jnp.maximum(jnp.max(sc, axis=-1, keepdims=True), sink)
                p = jnp.exp(sc - m)
                denom = jnp.sum(p, axis=-1, keepdims=True) + jnp.exp(sink - m)
                p = (p / denom).astype(jnp.bfloat16)
                o = _bdot(p, v_band)
                for pair in range(GROUP // 2):
                    lo = o[(2 * pair) * CHUNK:(2 * pair + 1) * CHUNK]
                    hi = o[(2 * pair + 1) * CHUNK:(2 * pair + 2) * CHUNK]
                    slabs.append(jnp.where(lane_o < HEAD_DIM, lo, hi))
            attn_ref[s * seg_len + c * CHUNK:s * seg_len + (c + 1) * CHUNK, :] = (
                jnp.concatenate(slabs, axis=1).astype(jnp.bfloat16))
        kbuf_ref[s, 0:HIST, :] = kbuf_ref[s, seg_len:seg_len + HIST, :]
        vbuf_ref[s, 0:HIST, :] = vbuf_ref[s, seg_len:seg_len + HIST, :]

    attn = _bdot(attn_ref[...], w_attn_ref[...])
    gates = _bdot(xn, w_in_ref[:, C_G + D_MODEL:C_G + 2 * D_MODEL]) + b_merge_ref[:, D_MODEL:2 * D_MODEL]
    mixed = (mixed + _sigmoid(gates) * attn).astype(jnp.bfloat16)
    x1 = x + _bdot(mixed, w_out_ref[...])
    x1_ref[...] = x1

    xn2 = _rmsnorm(x1, norm_ffn_ref[...])
    xn2_ref[...] = _pack_bf16_pairs(xn2) if sparse_out else xn2.astype(jnp.bfloat16)
    logits = _bdot(xn2.astype(jnp.bfloat16), w_rt_ref[...]) + b_rt_ref[...]
    lane = lax.broadcasted_iota(jnp.int32, (m_rows, ROUTER_W), 1).astype(jnp.float32)
    far = jnp.float32(2 * ROUTER_W)
    is_group = (lane >= N_EXPERTS) & (lane < N_EXPERTS + N_GROUPS)
    gl = jnp.where(is_group, logits, NEG_INF)
    g_max = jnp.max(gl, axis=-1, keepdims=True)
    g_idx = jnp.min(jnp.where(gl == g_max, lane, far), axis=-1, keepdims=True) - N_EXPERTS
    g_w = 1.0 / jnp.sum(jnp.where(is_group, jnp.exp(gl - g_max), 0.0), axis=-1, keepdims=True)
    in_group = (lane < N_EXPERTS) & (jnp.floor(lane * (1.0 / EXPERTS_PER_GROUP)) == g_idx)
    el = jnp.where(in_group, logits, NEG_INF)
    e1 = jnp.max(el, axis=-1, keepdims=True)
    i1 = jnp.min(jnp.where(el == e1, lane, far), axis=-1, keepdims=True)
    el2 = jnp.where(lane == i1, NEG_INF, el)
    e2 = jnp.max(el2, axis=-1, keepdims=True)
    i2 = jnp.min(jnp.where(el2 == e2, lane, far), axis=-1, keepdims=True)
    t = jnp.exp(e2 - e1)
    w1 = g_w / (1.0 + t)
    w2 = w1 * t
    if sparse_out:
        e_lane = lane + g_idx * EXPERTS_PER_GROUP
        rec = jnp.where(e_lane == i1, w1, 0.0) + jnp.where(e_lane == i2, w2, 0.0)
        rec = jnp.where(lane < EXPERTS_PER_GROUP, rec, 0.0)
        comb_ref[...] = jnp.where(lane == GID_LANE, g_idx, rec)
    else:
        comb_ref[...] = jnp.where(lane == i1, w1, 0.0) + jnp.where(lane == i2, w2, 0.0)


def _const_spec(shape):
    zeros = (0,) * len(shape)
    return pl.BlockSpec(shape, lambda i: zeros, pipeline_mode=pl.Buffered(1))


def _mixer(x, conv0, h0, k0, v0, weights, *, nseg, seg_len, mask_history, sparse_out):
    n_tok = x.shape[0]
    m_rows = nseg * seg_len
    n_steps = n_tok // m_rows
    keep = min(seg_len, HIST)
    row_spec = lambda w: pl.BlockSpec((m_rows, w), lambda i: (i, 0))
    in_specs = [row_spec(D_MODEL), _const_spec(conv0.shape), _const_spec(h0.shape), _const_spec(k0.shape),
                _const_spec(v0.shape)]
    for w in weights:
        if w.ndim == 1:
            in_specs.append(pl.BlockSpec(memory_space=pltpu.SMEM))
        else:
            in_specs.append(_const_spec(w.shape))
    state_spec = lambda r, w: pl.BlockSpec((nseg, r, w), lambda i: (0, 0, 0))
    out_shape = (
        jax.ShapeDtypeStruct((n_tok, D_MODEL), jnp.float32),
        (jax.ShapeDtypeStruct((n_tok, PACK_W), jnp.uint32) if sparse_out
         else jax.ShapeDtypeStruct((n_tok, D_MODEL), jnp.bfloat16)),
        jax.ShapeDtypeStruct((n_tok, ROUTER_W), jnp.float32),
        jax.ShapeDtypeStruct((nseg, SUBLANES, D_RNN), jnp.float32),
        jax.ShapeDtypeStruct((nseg, SUBLANES, D_RNN), jnp.float32),
        jax.ShapeDtypeStruct((nseg, keep, KV2_W), jnp.float32),
        jax.ShapeDtypeStruct((nseg, keep, KV2_W), jnp.float32),
    )
    out_specs = (row_spec(D_MODEL), row_spec(PACK_W if sparse_out else D_MODEL), row_spec(ROUTER_W),
                 state_spec(SUBLANES, D_RNN), state_spec(SUBLANES, D_RNN),
                 state_spec(keep, KV2_W), state_spec(keep, KV2_W))
    scratch = [
        pltpu.VMEM((nseg, SUBLANES + seg_len, D_RNN), jnp.float32),
        pltpu.VMEM((nseg, SUBLANES, D_RNN), jnp.float32),
        pltpu.VMEM((nseg, HIST + seg_len, KV2_W), jnp.bfloat16),
        pltpu.VMEM((nseg, HIST + seg_len, KV2_W), jnp.bfloat16),
        pltpu.VMEM((m_rows, D_RNN), jnp.float32),
        pltpu.VMEM((m_rows, D_RNN), jnp.float32),
        pltpu.VMEM((m_rows, Q_W), jnp.bfloat16),
    ]
    return pl.pallas_call(
        functools.partial(_mixer_kernel, nseg, seg_len, mask_history, sparse_out),
        grid=(n_steps,),
        in_specs=in_specs,
        out_specs=out_specs,
        out_shape=out_shape,
        scratch_shapes=scratch,
        compiler_params=pltpu.CompilerParams(dimension_semantics=("arbitrary",), vmem_limit_bytes=VMEM_LIMIT),
        name="mixer_prompt" if mask_history else "mixer_sample",
    )(x, conv0, h0, k0, v0, *weights)


def _group_swiglu(x, cols, wg_ref, wu_ref, wd_ref):
    parts = []
    for e in range(EXPERTS_PER_GROUP):
        h = jax.nn.silu(_bdot(x, wg_ref[e])) * _bdot(x, wu_ref[e])
        parts.append((h * cols[e]).astype(jnp.bfloat16))
    return _bdot(jnp.concatenate(parts, axis=1), wd_ref[...])


def _moe_groups_kernel(xn2_ref, comb_ref, x1_ref, wg_ref, wu_ref, wd_ref, norm_ref, y_ref, acc_ref):
    g = pl.program_id(1)

    @pl.when(g == 0)
    def _():
        acc_ref[...] = jnp.zeros_like(acc_ref)

    comb = comb_ref[...]
    lane = lax.broadcasted_iota(jnp.int32, comb.shape, 1)
    cols = [jnp.sum(jnp.where(lane == g * EXPERTS_PER_GROUP + e, comb, 0.0), axis=-1, keepdims=True)
            for e in range(EXPERTS_PER_GROUP)]
    acc_ref[...] += _group_swiglu(xn2_ref[...], cols, wg_ref, wu_ref, wd_ref)

    @pl.when(g == N_GROUPS - 1)
    def _():
        y_ref[...] = _rmsnorm(x1_ref[...] + acc_ref[...], norm_ref[...])


def _group_weight_specs(group_of):
    return [pl.BlockSpec((EXPERTS_PER_GROUP, D_MODEL, D_EXPERT), lambda *a: (group_of(*a), 0, 0)),
            pl.BlockSpec((EXPERTS_PER_GROUP, D_MODEL, D_EXPERT), lambda *a: (group_of(*a), 0, 0)),
            pl.BlockSpec((None, GROUP_FF, D_MODEL), lambda *a: (group_of(*a), 0, 0))]


def _moe_groups(xn2, comb, x1, wg, wu, wd, norm_final, *, tile):
    n_tok = xn2.shape[0]
    row_spec = lambda w: pl.BlockSpec((tile, w), lambda i, g: (i, 0))
    return pl.pallas_call(
        _moe_groups_kernel,
        grid=(n_tok // tile, N_GROUPS),
        in_specs=[row_spec(D_MODEL), row_spec(ROUTER_W), row_spec(D_MODEL),
                  *_group_weight_specs(lambda i, g: g),
                  pl.BlockSpec((1, D_MODEL), lambda i, g: (0, 0))],
        out_specs=row_spec(D_MODEL),
        out_shape=jax.ShapeDtypeStruct((n_tok, D_MODEL), jnp.float32),
        scratch_shapes=[pltpu.VMEM((tile, D_MODEL), jnp.float32)],
        compiler_params=pltpu.CompilerParams(dimension_semantics=("arbitrary", "arbitrary"),
                                             vmem_limit_bytes=VMEM_LIMIT),
        name="moe_groups",
    )(xn2, comb, x1, wg, wu, wd, norm_final)


def _scatter_rows_kernel(tile, pos_ref, x_ref, rec_ref, xs_ref, recs_ref):
    i = pl.program_id(0)

    @pl.when(i == 0)
    def _():
        xs_ref[...] = jnp.zeros_like(xs_ref)
        recs_ref[...] = jnp.zeros_like(recs_ref)

    def body(j, carry):
        r0 = pl.multiple_of(j * SUBLANES, SUBLANES)
        xb = x_ref[pl.ds(r0, SUBLANES), :]
        rb = rec_ref[pl.ds(r0, SUBLANES), :]
        for k in range(SUBLANES):
            p = pos_ref[i * tile + j * SUBLANES + k]
            xs_ref[pl.ds(p, 1), :] = xb[k:k + 1, :]
            recs_ref[pl.ds(p, 1), :] = rb[k:k + 1, :]
        return carry

    lax.fori_loop(0, tile // SUBLANES, body, 0)


def _scatter_rows(pos, xn2p, rec, n_slots, *, tile):
    n_tok = xn2p.shape[0]
    return pl.pallas_call(
        functools.partial(_scatter_rows_kernel, tile),
        grid_spec=pltpu.PrefetchScalarGridSpec(
            num_scalar_prefetch=1,
            grid=(n_tok // tile,),
            in_specs=[pl.BlockSpec((tile, PACK_W), lambda i, pos: (i, 0)),
                      pl.BlockSpec((tile, ROUTER_W), lambda i, pos: (i, 0))],
            out_specs=[pl.BlockSpec((n_slots, PACK_W), lambda i, pos: (0, 0)),
                       pl.BlockSpec((n_slots, ROUTER_W), lambda i, pos: (0, 0))],
        ),
        out_shape=(jax.ShapeDtypeStruct((n_slots, PACK_W), jnp.uint32),
                   jax.ShapeDtypeStruct((n_slots, ROUTER_W), jnp.float32)),
        compiler_params=pltpu.CompilerParams(dimension_semantics=("arbitrary",), vmem_limit_bytes=VMEM_LIMIT),
        name="moe_scatter",
    )(pos, xn2p, rec)


def _moe_sorted_kernel(tg_ref, xs_ref, recs_ref, wg_ref, wu_ref, wd_ref, o_ref):
    x = _unpack_bf16_pairs(xs_ref[...]).astype(jnp.bfloat16)
    rec = recs_ref[...]
    cols = [rec[:, e:e + 1] for e in range(EXPERTS_PER_GROUP)]
    o_ref[...] = _pack_bf16_pairs(_group_swiglu(x, cols, wg_ref, wu_ref, wd_ref))


def _moe_sorted(tile_group, xs, recs, wg, wu, wd, *, tile):
    n_slots = xs.shape[0]
    return pl.pallas_call(
        _moe_sorted_kernel,
        grid_spec=pltpu.PrefetchScalarGridSpec(
            num_scalar_prefetch=1,
            grid=(n_slots // tile,),
            in_specs=[pl.BlockSpec((tile, PACK_W), lambda i, tg: (i, 0)),
                      pl.BlockSpec((tile, ROUTER_W), lambda i, tg: (i, 0)),
                      *_group_weight_specs(lambda i, tg: tg[i])],
            out_specs=pl.BlockSpec((tile, PACK_W), lambda i, tg: (i, 0)),
        ),
        out_shape=jax.ShapeDtypeStruct((n_slots, PACK_W), jnp.uint32),
        compiler_params=pltpu.CompilerParams(dimension_semantics=("arbitrary",), vmem_limit_bytes=VMEM_LIMIT),
        name="moe_sorted",
    )(tile_group, xs, recs, wg, wu, wd)


def _gather_norm_kernel(tile, pos_ref, os_ref, x1_ref, norm_ref, y_ref, buf_ref):
    i = pl.program_id(0)

    def body(j, carry):
        r0 = pl.multiple_of(j * SUBLANES, SUBLANES)
        rows = [os_ref[pl.ds(pos_ref[i * tile + j * SUBLANES + k], 1), :] for k in range(SUBLANES)]
        buf_ref[pl.ds(r0, SUBLANES), :] = jnp.concatenate(rows, axis=0)
        return carry

    lax.fori_loop(0, tile // SUBLANES, body, 0)
    y_ref[...] = _rmsnorm(x1_ref[...] + _unpack_bf16_pairs(buf_ref[...]), norm_ref[...])


def _gather_norm(pos, o_sorted, x1, norm_final, *, tile):
    n_tok = x1.shape[0]
    n_slots = o_sorted.shape[0]
    return pl.pallas_call(
        functools.partial(_gather_norm_kernel, tile),
        grid_spec=pltpu.PrefetchScalarGridSpec(
            num_scalar_prefetch=1,
            grid=(n_tok // tile,),
            in_specs=[pl.BlockSpec((n_slots, PACK_W), lambda i, pos: (0, 0), pipeline_mode=pl.Buffered(1)),
                      pl.BlockSpec((tile, D_MODEL), lambda i, pos: (i, 0)),
                      pl.BlockSpec((1, D_MODEL), lambda i, pos: (0, 0))],
            out_specs=pl.BlockSpec((tile, D_MODEL), lambda i, pos: (i, 0)),
            scratch_shapes=[pltpu.VMEM((tile, PACK_W), jnp.uint32)],
        ),
        out_shape=jax.ShapeDtypeStruct((n_tok, D_MODEL), jnp.float32),
        compiler_params=pltpu.CompilerParams(dimension_semantics=("arbitrary",), vmem_limit_bytes=VMEM_LIMIT),
        name="moe_gather_norm",
    )(pos, o_sorted, x1, norm_final)


def _sorted_slots(rec, tile):
    n_tok = rec.shape[0]
    gid = rec[:, GID_LANE].astype(jnp.int32)
    onehot = (gid[:, None] == jnp.arange(N_GROUPS, dtype=jnp.int32)[None, :]).astype(jnp.int32)
    csum = jnp.cumsum(onehot, axis=0)
    rank = jnp.sum((csum - onehot) * onehot, axis=1)
    padded = ((csum[-1] + tile - 1) // tile) * tile
    end = jnp.cumsum(padded)
    pos = jnp.sum(onehot * (end - padded)[None, :], axis=1) + rank
    n_tiles = n_tok // tile + N_GROUPS
    tile_start = jnp.arange(n_tiles, dtype=jnp.int32) * tile
    tile_group = jnp.minimum(jnp.sum((tile_start[:, None] >= end[None, :]).astype(jnp.int32), axis=1), N_GROUPS - 1)
    return pos.astype(jnp.int32), tile_group.astype(jnp.int32), n_tiles * tile


def _rel_bucket(rel):
    nb = REL_BUCKETS // 2
    n = -rel
    ret = jnp.where(n < 0, nb, 0)
    n = jnp.abs(n)
    max_exact = nb // 2
    nf = jnp.maximum(n, 1).astype(jnp.float32)
    large = max_exact + (jnp.log(nf / max_exact) / math.log(REL_MAX_DIST / max_exact) * (nb - max_exact)).astype(jnp.int32)
    large = jnp.minimum(large, nb - 1)
    return ret + jnp.where(n < max_exact, n, large)


def _band_bias(rel_table):
    lo = -(HIST + CHUNK - 1)
    rels = jnp.arange(lo, CHUNK, dtype=jnp.int32)
    by_rel = jnp.transpose(rel_table[_rel_bucket(rels)]).astype(jnp.float32)
    rows = [by_rel[:, (CHUNK - 1 - qi):(CHUNK - 1 - qi) + BAND] for qi in range(CHUNK)]
    return jnp.stack(rows, axis=1).reshape(N_KV, GROUP * CHUNK, BAND)


def _dup_heads(a):
    a = a.reshape(a.shape[:-1] + (N_KV, HEAD_DIM))
    return jnp.concatenate([a, a], axis=-1).reshape(a.shape[:-2] + (KV2_W,))


def _undup_heads(a):
    a = a.reshape(a.shape[:-1] + (N_KV, LANES))
    return a[..., :HEAD_DIM]


def _block_diag_group(w, j):
    per = RG_GROUP // LRU_BLOCK_W
    blk = w[j * per:(j + 1) * per]
    eye = jnp.eye(per, dtype=w.dtype)
    return jnp.einsum('nde,nm->ndme', blk, eye).reshape(RG_GROUP, RG_GROUP)


def kernel(x_prompt, x_sample, state_lru_h, state_lru_conv, cache_swa_k, cache_swa_v, norm_mix, w_in, b_merge, conv_w, conv_b, w_rg_a, b_rg_a, w_rg_x, b_rg_x, lru_lambda, attn_sink, rel_bias, w_lru_proj, w_attn_proj, w_out, norm_ffn, w_group, b_group, w_router, b_router, w_e_gate, w_e_up, w_e_down, norm_final):
    f32, bf16 = jnp.float32, jnp.bfloat16
    l = 0
    w = w_in[l]
    w_kv2 = jnp.concatenate([_dup_heads(w[:, C_K:C_K + KV_W]), _dup_heads(w[:, C_V:C_V + KV_W])], axis=1).astype(bf16)
    w_rg = jnp.stack([
        jnp.concatenate([_block_diag_group(w_rg_a[l], j), _block_diag_group(w_rg_x[l], j)], axis=1)
        for j in range(N_RG_GROUPS)]).astype(bf16)
    w_rt = jnp.concatenate([w_router[l], w_group[l],
                            jnp.zeros((D_MODEL, ROUTER_W - N_EXPERTS - N_GROUPS), f32)], axis=1).astype(bf16)
    b_rt = jnp.concatenate([b_router[l], b_group[l], jnp.zeros((ROUTER_W - N_EXPERTS - N_GROUPS,), f32)])[None, :]
    weights = (
        norm_mix[l][None, :], w.astype(bf16), w_kv2, b_merge[l][None, :], conv_w[l], conv_b[l][None, :], w_rg,
        b_rg_a[l][None, :], b_rg_x[l][None, :], lru_lambda[l][None, :], attn_sink[l], _band_bias(rel_bias),
        w_lru_proj[l].astype(bf16), w_attn_proj[l].astype(bf16), w_out[l].astype(bf16), norm_ffn[l][None, :],
        w_rt, b_rt,
    )
    wg, wu = w_e_gate[l].astype(bf16), w_e_up[l].astype(bf16)
    wd = w_e_down[l].astype(bf16).reshape(N_GROUPS, GROUP_FF, D_MODEL)
    nf = norm_final[None, :]

    bp, tp, _ = x_prompt.shape
    assert bp == 1
    xp = x_prompt.reshape(tp, D_MODEL)
    zeros = lambda *s: jnp.zeros(s, f32)
    p_x1, p_xn2, p_comb, p_conv, p_h, p_k, p_v = _mixer(
        xp, zeros(1, SUBLANES, D_RNN), zeros(1, SUBLANES, D_RNN),
        jnp.zeros((1, HIST, KV2_W), bf16), jnp.zeros((1, HIST, KV2_W), bf16), weights,
        nseg=1, seg_len=256, mask_history=True, sparse_out=True)
    pos, tile_group, n_slots = _sorted_slots(p_comb, SORT_TILE)
    x_sorted, rec_sorted = _scatter_rows(pos, p_xn2, p_comb, n_slots, tile=SORT_TILE)
    o_sorted = _moe_sorted(tile_group, x_sorted, rec_sorted, wg, wu, wd, tile=SORT_TILE)
    y_p = _gather_norm(pos, o_sorted, p_x1, nf, tile=SORT_TILE).reshape(x_prompt.shape)

    bs, ts, _ = x_sample.shape
    xs = x_sample.reshape(bs * ts, D_MODEL)
    conv0 = jnp.pad(state_lru_conv[l], ((0, 0), (SUBLANES - (CONV_W - 1), 0), (0, 0)))
    h0 = jnp.broadcast_to(state_lru_h[l][:, None, :], (bs, SUBLANES, D_RNN))
    ck = cache_swa_k[l].reshape(bs, -1, KV_W)
    cv = cache_swa_v[l].reshape(bs, -1, KV_W)
    s_x1, s_xn2, s_comb, s_conv, s_h, s_k, s_v = _mixer(
        xs, conv0, h0, _dup_heads(ck).astype(bf16), _dup_heads(cv).astype(bf16), weights,
        nseg=bs, seg_len=ts, mask_history=False, sparse_out=False)
    y_s = _moe_groups(s_xn2, s_comb, s_x1, wg, wu, wd, nf, tile=bs * ts).reshape(x_sample.shape)

    tail = CONV_W - 1
    p_lru_h = p_h[:, 0, :][None]
    p_lru_conv = p_conv[:, SUBLANES - tail:, :][None]
    p_swa_k = _undup_heads(p_k)[None]
    p_swa_v = _undup_heads(p_v)[None]
    s_lru_h = s_h[:, 0, :][None]
    s_lru_conv = s_conv[:, SUBLANES - tail:, :][None]
    win = ck.shape[1]
    s_swa_k = jnp.concatenate([cache_swa_k[l][:, ts:], _undup_heads(s_k)], axis=1)[:, -win:][None]
    s_swa_v = jnp.concatenate([cache_swa_v[l][:, ts:], _undup_heads(s_v)], axis=1)[:, -win:][None]
    return (y_p, y_s, p_lru_h, p_lru_conv, p_swa_k, p_swa_v, s_lru_h, s_lru_conv, s_swa_k, s_swa_v)
```

```python
import functools
import math

import jax
import jax.numpy as jnp
from jax import lax
from jax.experimental import pallas as pl
from jax.experimental.pallas import tpu as pltpu

D_MODEL = 1024
CHUNK = 64
D_RNN = D_MODEL
CONV_W = 4
LRU_BLOCKS = 16
LRU_BLOCK_W = D_RNN // LRU_BLOCKS
LRU_C = 8.0
N_HEADS = 16
N_KV = 4
HEAD_DIM = 64
GROUP = N_HEADS // N_KV
WINDOW = 128
LOOKBACK_CHUNKS = -(-WINDOW // CHUNK)
HIST = LOOKBACK_CHUNKS * CHUNK
BAND = HIST + CHUNK
Q_W = N_HEADS * HEAD_DIM
KV_W = N_KV * HEAD_DIM
REL_BUCKETS = 32
REL_MAX_DIST = 128
N_GROUPS = 4
EXPERTS_PER_GROUP = 8
N_EXPERTS = N_GROUPS * EXPERTS_PER_GROUP
D_EXPERT = 256
EPS = 1e-6
NEG_INF = -1e30

LANES = 128
SUBLANES = 8
KV2_W = N_KV * LANES
RG_GROUP = 256
N_RG_GROUPS = D_RNN // RG_GROUP
ROUTER_W = LANES
VMEM_LIMIT = 56 * 1024 * 1024
PACK_W = D_MODEL // 2
GROUP_FF = EXPERTS_PER_GROUP * D_EXPERT
GID_LANE = EXPERTS_PER_GROUP
SORT_TILE = 512

C_LX, C_LG, C_Q = 0, D_RNN, 2 * D_RNN
C_K = C_Q + Q_W
C_V = C_K + KV_W
C_G = C_V + KV_W
P_LX, P_LG, P_G = 0, D_RNN, 2 * D_RNN
P_KV = P_G + 2 * D_MODEL
P_W = P_KV + 2 * KV2_W
PROJ_BLOCK = 256


def _rmsnorm(x, g):
    return x * lax.rsqrt(jnp.mean(x * x, axis=-1, keepdims=True) + EPS) * g


def _bdot(a, b):
    return jnp.dot(a, b, preferred_element_type=jnp.float32)


def _sigmoid(x):
    return 0.5 * jnp.tanh(0.5 * x) + 0.5


def _pack_bf16_pairs(x):
    return pltpu.pack_elementwise([x[:, :PACK_W], x[:, PACK_W:]], packed_dtype=jnp.bfloat16)


def _unpack_bf16_pairs(p):
    lo = pltpu.unpack_elementwise(p, index=0, packed_dtype=jnp.bfloat16, unpacked_dtype=jnp.float32)
    hi = pltpu.unpack_elementwise(p, index=1, packed_dtype=jnp.bfloat16, unpacked_dtype=jnp.float32)
    return jnp.concatenate([lo, hi], axis=1)


def _mixer_kernel(nseg, seg_len, streamed, sparse_out,
                  x_ref, xnext_ref, conv0_ref, h0_ref, k0_ref, v0_ref,
                  norm_mix_ref, w_in_ref, w_kv2_ref, b_merge_ref, conv_w_ref, conv_b_ref, w_rg_ref, b_rg_a_ref, b_rg_x_ref,
                  lam_ref, sink_ref, bias_ref, w_lru_ref, w_attn_ref, w_out_ref, norm_ffn_ref, w_rt_ref, b_rt_ref,
                  x1_ref, xn2_ref, comb_ref, conv_out_ref, h_out_ref, k_out_ref, v_out_ref,
                  xp_ref, hc_ref, kbuf_ref, vbuf_ref, a_ref, b_ref, attn_ref, pf_ref, pq_ref, xnn_ref):
    step = pl.program_id(0)
    m_rows = nseg * seg_len
    n_chunks = seg_len // CHUNK
    keep = min(seg_len, HIST)
    mask_history = streamed

    def project(xr, slot):
        xn = _rmsnorm(xr[...], norm_mix_ref[...]).astype(jnp.bfloat16)
        pf_ref[slot, :, P_LX:P_LX + 2 * D_RNN] = _bdot(xn, w_in_ref[:, C_LX:C_LX + 2 * D_RNN])
        pq_ref[slot] = (_bdot(xn, w_in_ref[:, C_Q:C_Q + Q_W]) * (HEAD_DIM ** -0.5)).astype(jnp.bfloat16)
        pf_ref[slot, :, P_G:P_G + 2 * D_MODEL] = _bdot(xn, w_in_ref[:, C_G:C_G + 2 * D_MODEL])
        pf_ref[slot, :, P_KV:P_KV + 2 * KV2_W] = _bdot(xn, w_kv2_ref[...])

    @pl.when(step == 0)
    def _():
        xp_ref[:, 0:SUBLANES, :] = conv0_ref[...]
        hc_ref[...] = h0_ref[...]
        kbuf_ref[:, 0:HIST, :] = k0_ref[...]
        vbuf_ref[:, 0:HIST, :] = v0_ref[...]
        project(x_ref, 0)

    tasks = []
    if streamed:
        slot = step % 2
        nslot = 1 - slot
        xnn_ref[...] = _rmsnorm(xnext_ref[...], norm_mix_ref[...]).astype(jnp.bfloat16)

        def f32_block(src_ref, c_src, c_dst):
            def run():
                pf_ref[nslot, :, c_dst:c_dst + PROJ_BLOCK] = _bdot(xnn_ref[...], src_ref[:, c_src:c_src + PROJ_BLOCK])
            return run

        def q_block(c):
            def run():
                pq_ref[nslot, :, c:c + PROJ_BLOCK] = (
                    _bdot(xnn_ref[...], w_in_ref[:, C_Q + c:C_Q + c + PROJ_BLOCK]) * (HEAD_DIM ** -0.5)
                ).astype(jnp.bfloat16)
            return run

        for c in range(0, 2 * D_RNN, PROJ_BLOCK):
            tasks.append(f32_block(w_in_ref, C_LX + c, P_LX + c))
        for c in range(0, 2 * D_MODEL, PROJ_BLOCK):
            tasks.append(f32_block(w_in_ref, C_G + c, P_G + c))
        for c in range(0, 2 * KV2_W, PROJ_BLOCK):
            tasks.append(f32_block(w_kv2_ref, c, P_KV + c))
        for c in range(0, Q_W, PROJ_BLOCK):
            tasks.append(q_block(c))
    else:
        slot = 0

    def pump(n):
        for _ in range(min(n, len(tasks))):
            tasks.pop(0)()

    x = x_ref[...]

    lru_x = pf_ref[slot, :, P_LX:P_LX + D_RNN]
    conv_w = conv_w_ref[...]
    xc_parts = []
    for s in range(nseg):
        xp_ref[s, SUBLANES:SUBLANES + seg_len, :] = lru_x[s * seg_len:(s + 1) * seg_len, :]
        acc = conv_b_ref[...]
        for j in range(CONV_W):
            off = SUBLANES - (CONV_W - 1) + j
            acc = acc + xp_ref[s, off:off + seg_len, :] * conv_w[j:j + 1, :]
            pump(1)
        xc_parts.append(acc)
        conv_out_ref[s] = xp_ref[s, seg_len:seg_len + SUBLANES, :]
        xp_ref[s, 0:SUBLANES, :] = xp_ref[s, seg_len:seg_len + SUBLANES, :]
    xc = xc_parts[0] if nseg == 1 else jnp.concatenate(xc_parts, axis=0)
    xc_b = xc.astype(jnp.bfloat16)

    lam = lam_ref[...]
    log_sig = jnp.minimum(lam, 0.0) - jnp.log1p(jnp.exp(-jnp.abs(lam)))
    c8 = LRU_C * log_sig
    for j in range(N_RG_GROUPS):
        cs = slice(j * RG_GROUP, (j + 1) * RG_GROUP)
        pre = _bdot(xc_b[:, cs], w_rg_ref[j])
        r = _sigmoid(pre[:, :RG_GROUP] + b_rg_a_ref[:, cs])
        i = _sigmoid(pre[:, RG_GROUP:] + b_rg_x_ref[:, cs])
        log_a = c8[:, cs] * r
        a_ref[:, cs] = jnp.exp(log_a)
        th = jnp.tanh(log_a)
        b_ref[:, cs] = jnp.sqrt(-2.0 * th / (1.0 - th)) * i * xc[:, cs]
        pump(2)

    row = lax.broadcasted_iota(jnp.int32, (SUBLANES, D_RNN), 0)
    for s in range(nseg):
        carry = hc_ref[s]
        for g in range(seg_len // SUBLANES):
            r0 = s * seg_len + g * SUBLANES
            a = a_ref[r0:r0 + SUBLANES, :]
            b = b_ref[r0:r0 + SUBLANES, :]
            for d in (1, 2, 4):
                a_s = pltpu.roll(a, d, axis=0)
                b_s = pltpu.roll(b, d, axis=0)
                m = row >= d
                b = jnp.where(m, a * b_s + b, b)
                a = jnp.where(m, a * a_s, a)
            h = a * carry + b
            b_ref[r0:r0 + SUBLANES, :] = h
            carry = jnp.broadcast_to(h[SUBLANES - 1:SUBLANES, :], (SUBLANES, D_RNN))
            if g % 4 == 3:
                pump(1)
        hc_ref[s] = carry
        h_out_ref[s] = carry

    lru_gate = pf_ref[slot, :, P_LG:P_LG + D_RNN]
    lru_y = (b_ref[...] * jax.nn.gelu(lru_gate)).astype(jnp.bfloat16)
    pump(1)
    mixed = _bdot(lru_y, w_lru_ref[...])
    gates = pf_ref[slot, :, P_G:P_G + D_MODEL] + b_merge_ref[:, 0:D_MODEL]
    mixed = _sigmoid(gates) * mixed
    pump(1)

    q = pq_ref[slot]
    k2 = pf_ref[slot, :, P_KV:P_KV + KV2_W]
    v2 = pf_ref[slot, :, P_KV + KV2_W:P_KV + 2 * KV2_W]
    lane_q = lax.broadcasted_iota(jnp.int32, (CHUNK, LANES), 1)
    lane_o = lax.broadcasted_iota(jnp.int32, (CHUNK, LANES), 1)
    key_lane = lax.broadcasted_iota(jnp.int32, (1, BAND), 1)
    row_grp = lax.broadcasted_iota(jnp.int32, (GROUP * CHUNK, 1), 0) // CHUNK
    for s in range(nseg):
        rows = slice(s * seg_len, (s + 1) * seg_len)
        kbuf_ref[s, HIST:HIST + seg_len, :] = k2[rows].astype(jnp.bfloat16)
        vbuf_ref[s, HIST:HIST + seg_len, :] = v2[rows].astype(jnp.bfloat16)
        k_out_ref[s] = k2[s * seg_len + seg_len - keep:(s + 1) * seg_len]
        v_out_ref[s] = v2[s * seg_len + seg_len - keep:(s + 1) * seg_len]
        for c in range(n_chunks):
            q_c = q[s * seg_len + c * CHUNK:s * seg_len + (c + 1) * CHUNK]
            slabs = []
            for kv in range(N_KV):
                parts = []
                for g in range(GROUP):
                    col = kv * GROUP * HEAD_DIM + (g // 2) * LANES
                    slab = q_c[:, col:col + LANES]
                    keep_lo = (g % 2) == 0
                    sel = (lane_q < HEAD_DIM) if keep_lo else (lane_q >= HEAD_DIM)
                    parts.append(jnp.where(sel, slab, jnp.zeros_like(slab)))
                q_stack = jnp.concatenate(parts, axis=0)
                k_band = kbuf_ref[s, c * CHUNK:c * CHUNK + BAND, kv * LANES:(kv + 1) * LANES]
                v_band = vbuf_ref[s, c * CHUNK:c * CHUNK + BAND, kv * LANES:(kv + 1) * LANES]
                sc = lax.dot_general(q_stack, k_band, (((1,), (1,)), ((), ())),
                                     preferred_element_type=jnp.float32)
                sc = sc + bias_ref[kv]
                if mask_history:
                    first_valid = HIST - (step * n_chunks + c) * CHUNK
                    sc = jnp.where(key_lane >= first_valid, sc, NEG_INF)
                sink = jnp.zeros((GROUP * CHUNK, 1), jnp.float32)
                for g in range(GROUP):
                    sink = jnp.where(row_grp == g, sink_ref[kv * GROUP + g], sink)
                m = jnp.maximum(jnp.max(sc, axis=-1, keepdims=True), sink)
                p = jnp.exp(sc - m)
                denom = jnp.sum(p, axis=-1, keepdims=True) + jnp.exp(sink - m)
                p = (p / denom).astype(jnp.bfloat16)
                o = _bdot(p, v_band)
                for pair in range(GROUP // 2):
                    lo = o[(2 * pair) * CHUNK:(2 * pair + 1) * CHUNK]
                    hi = o[(2 * pair + 1) * CHUNK:(2 * pair + 2) * CHUNK]
                    slabs.append(jnp.where(lane_o < HEAD_DIM, lo, hi))
            pump(1)
            attn_ref[s * seg_len + c * CHUNK:s * seg_len + (c + 1) * CHUNK, :] = (
                jnp.concatenate(slabs, axis=1).astype(jnp.bfloat16))
        kbuf_ref[s, 0:HIST, :] = kbuf_ref[s, seg_len:seg_len + HIST, :]
        vbuf_ref[s, 0:HIST, :] = vbuf_ref[s, seg_len:seg_len + HIST, :]

    pump(len(tasks))
    attn = _bdot(attn_ref[...], w_attn_ref[...])
    gates = pf_ref[slot, :, P_G + D_MODEL:P_G + 2 * D_MODEL] + b_merge_ref[:, D_MODEL:2 * D_MODEL]
    mixed = (mixed + _sigmoid(gates) * attn).astype(jnp.bfloat16)
    x1 = x + _bdot(mixed, w_out_ref[...])
    x1_ref[...] = x1

    xn2 = _rmsnorm(x1, norm_ffn_ref[...])
    xn2_ref[...] = _pack_bf16_pairs(xn2) if sparse_out else xn2.astype(jnp.bfloat16)
    logits = _bdot(xn2.astype(jnp.bfloat16), w_rt_ref[...]) + b_rt_ref[...]
    lane = lax.broadcasted_iota(jnp.int32, (m_rows, ROUTER_W), 1).astype(jnp.float32)
    far = jnp.float32(2 * ROUTER_W)
    is_group = (lane >= N_EXPERTS) & (lane < N_EXPERTS + N_GROUPS)
    gl = jnp.where(is_group, logits, NEG_INF)
    g_max = jnp.max(gl, axis=-1, keepdims=True)
    g_idx = jnp.min(jnp.where(gl == g_max, lane, far), axis=-1, keepdims=True) - N_EXPERTS
    g_w = 1.0 / jnp.sum(jnp.where(is_group, jnp.exp(gl - g_max), 0.0), axis=-1, keepdims=True)
    in_group = (lane < N_EXPERTS) & (jnp.floor(lane * (1.0 / EXPERTS_PER_GROUP)) == g_idx)
    el = jnp.where(in_group, logits, NEG_INF)
    e1 = jnp.max(el, axis=-1, keepdims=True)
    i1 = jnp.min(jnp.where(el == e1, lane, far), axis=-1, keepdims=True)
    el2 = jnp.where(lane == i1, NEG_INF, el)
    e2 = jnp.max(el2, axis=-1, keepdims=True)
    i2 = jnp.min(jnp.where(el2 == e2, lane, far), axis=-1, keepdims=True)
    t = jnp.exp(e2 - e1)
    w1 = g_w / (1.0 + t)
    w2 = w1 * t
    if sparse_out:
        e_lane = lane + g_idx * EXPERTS_PER_GROUP
        rec = jnp.where(e_lane == i1, w1, 0.0) + jnp.where(e_lane == i2, w2, 0.0)
        rec = jnp.where(lane < EXPERTS_PER_GROUP, rec, 0.0)
        comb_ref[...] = jnp.where(lane == GID_LANE, g_idx, rec)
    else:
        comb_ref[...] = jnp.where(lane == i1, w1, 0.0) + jnp.where(lane == i2, w2, 0.0)


def _const_spec(shape):
    zeros = (0,) * len(shape)
    return pl.BlockSpec(shape, lambda i: zeros, pipeline_mode=pl.Buffered(1))


def _mixer(x, conv0, h0, k0, v0, weights, *, nseg, seg_len, mask_history, sparse_out):
    n_tok = x.shape[0]
    m_rows = nseg * seg_len
    n_steps = n_tok // m_rows
    keep = min(seg_len, HIST)
    row_spec = lambda w: pl.BlockSpec((m_rows, w), lambda i: (i, 0))
    next_spec = pl.BlockSpec((m_rows, D_MODEL), lambda i: (jnp.minimum(i + 1, n_steps - 1), 0))
    in_specs = [row_spec(D_MODEL), next_spec, _const_spec(conv0.shape), _const_spec(h0.shape), _const_spec(k0.shape),
                _const_spec(v0.shape)]
    for w in weights:
        if w.ndim == 1:
            in_specs.append(pl.BlockSpec(memory_space=pltpu.SMEM))
        else:
            in_specs.append(_const_spec(w.shape))
    state_spec = lambda r, w: pl.BlockSpec((nseg, r, w), lambda i: (0, 0, 0))
    out_shape = (
        jax.ShapeDtypeStruct((n_tok, D_MODEL), jnp.float32),
        (jax.ShapeDtypeStruct((n_tok, PACK_W), jnp.uint32) if sparse_out
         else jax.ShapeDtypeStruct((n_tok, D_MODEL), jnp.bfloat16)),
        jax.ShapeDtypeStruct((n_tok, ROUTER_W), jnp.float32),
        jax.ShapeDtypeStruct((nseg, SUBLANES, D_RNN), jnp.float32),
        jax.ShapeDtypeStruct((nseg, SUBLANES, D_RNN), jnp.float32),
        jax.ShapeDtypeStruct((nseg, keep, KV2_W), jnp.float32),
        jax.ShapeDtypeStruct((nseg, keep, KV2_W), jnp.float32),
    )
    out_specs = (row_spec(D_MODEL), row_spec(PACK_W if sparse_out else D_MODEL), row_spec(ROUTER_W),
                 state_spec(SUBLANES, D_RNN), state_spec(SUBLANES, D_RNN),
                 state_spec(keep, KV2_W), state_spec(keep, KV2_W))
    scratch = [
        pltpu.VMEM((nseg, SUBLANES + seg_len, D_RNN), jnp.float32),
        pltpu.VMEM((nseg, SUBLANES, D_RNN), jnp.float32),
        pltpu.VMEM((nseg, HIST + seg_len, KV2_W), jnp.bfloat16),
        pltpu.VMEM((nseg, HIST + seg_len, KV2_W), jnp.bfloat16),
        pltpu.VMEM((m_rows, D_RNN), jnp.float32),
        pltpu.VMEM((m_rows, D_RNN), jnp.float32),
        pltpu.VMEM((m_rows, Q_W), jnp.bfloat16),
        pltpu.VMEM((2 if mask_history else 1, m_rows, P_W), jnp.float32),
        pltpu.VMEM((2 if mask_history else 1, m_rows, Q_W), jnp.bfloat16),
        pltpu.VMEM((m_rows, D_MODEL), jnp.bfloat16),
    ]
    return pl.pallas_call(
        functools.partial(_mixer_kernel, nseg, seg_len, mask_history, sparse_out),
        grid=(n_steps,),
        in_specs=in_specs,
        out_specs=out_specs,
        out_shape=out_shape,
        scratch_shapes=scratch,
        compiler_params=pltpu.CompilerParams(dimension_semantics=("arbitrary",), vmem_limit_bytes=VMEM_LIMIT),
        name="mixer_prompt" if mask_history else "mixer_sample",
    )(x, x, conv0, h0, k0, v0, *weights)


def _group_swiglu(x, cols, wg_ref, wu_ref, wd_ref):
    parts = []
    for e in range(EXPERTS_PER_GROUP):
        h = jax.nn.silu(_bdot(x, wg_ref[e])) * _bdot(x, wu_ref[e])
        parts.append((h * cols[e]).astype(jnp.bfloat16))
    return _bdot(jnp.concatenate(parts, axis=1), wd_ref[...])


def _moe_groups_kernel(xn2_ref, comb_ref, x1_ref, wg_ref, wu_ref, wd_ref, norm_ref, y_ref, acc_ref):
    g = pl.program_id(1)

    @pl.when(g == 0)
    def _():
        acc_ref[...] = jnp.zeros_like(acc_ref)

    comb = comb_ref[...]
    lane = lax.broadcasted_iota(jnp.int32, comb.shape, 1)
    cols = [jnp.sum(jnp.where(lane == g * EXPERTS_PER_GROUP + e, comb, 0.0), axis=-1, keepdims=True)
            for e in range(EXPERTS_PER_GROUP)]
    acc_ref[...] += _group_swiglu(xn2_ref[...], cols, wg_ref, wu_ref, wd_ref)

    @pl.when(g == N_GROUPS - 1)
    def _():
        y_ref[...] = _rmsnorm(x1_ref[...] + acc_ref[...], norm_ref[...])


def _group_weight_specs(group_of):
    return [pl.BlockSpec((EXPERTS_PER_GROUP, D_MODEL, D_EXPERT), lambda *a: (group_of(*a), 0, 0)),
            pl.BlockSpec((EXPERTS_PER_GROUP, D_MODEL, D_EXPERT), lambda *a: (group_of(*a), 0, 0)),
            pl.BlockSpec((None, GROUP_FF, D_MODEL), lambda *a: (group_of(*a), 0, 0))]


def _moe_groups(xn2, comb, x1, wg, wu, wd, norm_final, *, tile):
    n_tok = xn2.shape[0]
    row_spec = lambda w: pl.BlockSpec((tile, w), lambda i, g: (i, 0))
    return pl.pallas_call(
        _moe_groups_kernel,
        grid=(n_tok // tile, N_GROUPS),
        in_specs=[row_spec(D_MODEL), row_spec(ROUTER_W), row_spec(D_MODEL),
                  *_group_weight_specs(lambda i, g: g),
                  pl.BlockSpec((1, D_MODEL), lambda i, g: (0, 0))],
        out_specs=row_spec(D_MODEL),
        out_shape=jax.ShapeDtypeStruct((n_tok, D_MODEL), jnp.float32),
        scratch_shapes=[pltpu.VMEM((tile, D_MODEL), jnp.float32)],
        compiler_params=pltpu.CompilerParams(dimension_semantics=("arbitrary", "arbitrary"),
                                             vmem_limit_bytes=VMEM_LIMIT),
        name="moe_groups",
    )(xn2, comb, x1, wg, wu, wd, norm_final)


def _scatter_rows_kernel(tile, pos_ref, x_ref, rec_ref, xs_ref, recs_ref):
    i = pl.program_id(0)

    @pl.when(i == 0)
    def _():
        xs_ref[...] = jnp.zeros_like(xs_ref)
        recs_ref[...] = jnp.zeros_like(recs_ref)

    def body(j, carry):
        r0 = pl.multiple_of(j * SUBLANES, SUBLANES)
        xb = x_ref[pl.ds(r0, SUBLANES), :]
        rb = rec_ref[pl.ds(r0, SUBLANES), :]
        for k in range(SUBLANES):
            p = pos_ref[i * tile + j * SUBLANES + k]
            xs_ref[pl.ds(p, 1), :] = xb[k:k + 1, :]
            recs_ref[pl.ds(p, 1), :] = rb[k:k + 1, :]
        return carry

    lax.fori_loop(0, tile // SUBLANES, body, 0)


def _scatter_rows(pos, xn2p, rec, n_slots, *, tile):
    n_tok = xn2p.shape[0]
    return pl.pallas_call(
        functools.partial(_scatter_rows_kernel, tile),
        grid_spec=pltpu.PrefetchScalarGridSpec(
            num_scalar_prefetch=1,
            grid=(n_tok // tile,),
            in_specs=[pl.BlockSpec((tile, PACK_W), lambda i, pos: (i, 0)),
                      pl.BlockSpec((tile, ROUTER_W), lambda i, pos: (i, 0))],
            out_specs=[pl.BlockSpec((n_slots, PACK_W), lambda i, pos: (0, 0)),
                       pl.BlockSpec((n_slots, ROUTER_W), lambda i, pos: (0, 0))],
        ),
        out_shape=(jax.ShapeDtypeStruct((n_slots, PACK_W), jnp.uint32),
                   jax.ShapeDtypeStruct((n_slots, ROUTER_W), jnp.float32)),
        compiler_params=pltpu.CompilerParams(dimension_semantics=("arbitrary",), vmem_limit_bytes=VMEM_LIMIT),
        name="moe_scatter",
    )(pos, xn2p, rec)


def _moe_sorted_kernel(tg_ref, xs_ref, recs_ref, wg_ref, wu_ref, wd_ref, o_ref):
    x = _unpack_bf16_pairs(xs_ref[...]).astype(jnp.bfloat16)
    rec = recs_ref[...]
    cols = [rec[:, e:e + 1] for e in range(EXPERTS_PER_GROUP)]
    o_ref[...] = _pack_bf16_pairs(_group_swiglu(x, cols, wg_ref, wu_ref, wd_ref))


def _moe_sorted(tile_group, xs, recs, wg, wu, wd, *, tile):
    n_slots = xs.shape[0]
    return pl.pallas_call(
        _moe_sorted_kernel,
        grid_spec=pltpu.PrefetchScalarGridSpec(
            num_scalar_prefetch=1,
            grid=(n_slots // tile,),
            in_specs=[pl.BlockSpec((tile, PACK_W), lambda i, tg: (i, 0)),
                      pl.BlockSpec((tile, ROUTER_W), lambda i, tg: (i, 0)),
                      *_group_weight_specs(lambda i, tg: tg[i])],
            out_specs=pl.BlockSpec((tile, PACK_W), lambda i, tg: (i, 0)),
        ),
        out_shape=jax.ShapeDtypeStruct((n_slots, PACK_W), jnp.uint32),
        compiler_params=pltpu.CompilerParams(dimension_semantics=("arbitrary",), vmem_limit_bytes=VMEM_LIMIT),
        name="moe_sorted",
    )(tile_group, xs, recs, wg, wu, wd)


def _gather_norm_kernel(tile, pos_ref, os_ref, x1_ref, norm_ref, y_ref, buf_ref):
    i = pl.program_id(0)

    def body(j, carry):
        r0 = pl.multiple_of(j * SUBLANES, SUBLANES)
        rows = [os_ref[pl.ds(pos_ref[i * tile + j * SUBLANES + k], 1), :] for k in range(SUBLANES)]
        buf_ref[pl.ds(r0, SUBLANES), :] = jnp.concatenate(rows, axis=0)
        return carry

    lax.fori_loop(0, tile // SUBLANES, body, 0)
    y_ref[...] = _rmsnorm(x1_ref[...] + _unpack_bf16_pairs(buf_ref[...]), norm_ref[...])


def _gather_norm(pos, o_sorted, x1, norm_final, *, tile):
    n_tok = x1.shape[0]
    n_slots = o_sorted.shape[0]
    return pl.pallas_call(
        functools.partial(_gather_norm_kernel, tile),
        grid_spec=pltpu.PrefetchScalarGridSpec(
            num_scalar_prefetch=1,
            grid=(n_tok // tile,),
            in_specs=[pl.BlockSpec((n_slots, PACK_W), lambda i, pos: (0, 0), pipeline_mode=pl.Buffered(1)),
                      pl.BlockSpec((tile, D_MODEL), lambda i, pos: (i, 0)),
                      pl.BlockSpec((1, D_MODEL), lambda i, pos: (0, 0))],
            out_specs=pl.BlockSpec((tile, D_MODEL), lambda i, pos: (i, 0)),
            scratch_shapes=[pltpu.VMEM((tile, PACK_W), jnp.uint32)],
        ),
        out_shape=jax.ShapeDtypeStruct((n_tok, D_MODEL), jnp.float32),
        compiler_params=pltpu.CompilerParams(dimension_semantics=("arbitrary",), vmem_limit_bytes=VMEM_LIMIT),
        name="moe_gather_norm",
    )(pos, o_sorted, x1, norm_final)


def _sorted_slots(rec, tile):
    n_tok = rec.shape[0]
    gid = rec[:, GID_LANE].astype(jnp.int32)
    onehot = (gid[:, None] == jnp.arange(N_GROUPS, dtype=jnp.int32)[None, :]).astype(jnp.int32)
    csum = jnp.cumsum(onehot, axis=0)
    rank = jnp.sum((csum - onehot) * onehot, axis=1)
    padded = ((csum[-1] + tile - 1) // tile) * tile
    end = jnp.cumsum(padded)
    pos = jnp.sum(onehot * (end - padded)[None, :], axis=1) + rank
    n_tiles = n_tok // tile + N_GROUPS
    tile_start = jnp.arange(n_tiles, dtype=jnp.int32) * tile
    tile_group = jnp.minimum(jnp.sum((tile_start[:, None] >= end[None, :]).astype(jnp.int32), axis=1), N_GROUPS - 1)
    return pos.astype(jnp.int32), tile_group.astype(jnp.int32), n_tiles * tile


def _rel_bucket(rel):
    nb = REL_BUCKETS // 2
    n = -rel
    ret = jnp.where(n < 0, nb, 0)
    n = jnp.abs(n)
    max_exact = nb // 2
    nf = jnp.maximum(n, 1).astype(jnp.float32)
    large = max_exact + (jnp.log(nf / max_exact) / math.log(REL_MAX_DIST / max_exact) * (nb - max_exact)).astype(jnp.int32)
    large = jnp.minimum(large, nb - 1)
    return ret + jnp.where(n < max_exact, n, large)


def _band_bias(rel_table):
    lo = -(HIST + CHUNK - 1)
    rels = jnp.arange(lo, CHUNK, dtype=jnp.int32)
    by_rel = jnp.transpose(rel_table[_rel_bucket(rels)]).astype(jnp.float32)
    rows = [by_rel[:, (CHUNK - 1 - qi):(CHUNK - 1 - qi) + BAND] for qi in range(CHUNK)]
    return jnp.stack(rows, axis=1).reshape(N_KV, GROUP * CHUNK, BAND)


def _dup_heads(a):
    a = a.reshape(a.shape[:-1] + (N_KV, HEAD_DIM))
    return jnp.concatenate([a, a], axis=-1).reshape(a.shape[:-2] + (KV2_W,))


def _undup_heads(a):
    a = a.reshape(a.shape[:-1] + (N_KV, LANES))
    return a[..., :HEAD_DIM]


def _block_diag_group(w, j):
    per = RG_GROUP // LRU_BLOCK_W
    blk = w[j * per:(j + 1) * per]
    eye = jnp.eye(per, dtype=w.dtype)
    return jnp.einsum('nde,nm->ndme', blk, eye).reshape(RG_GROUP, RG_GROUP)


def kernel(x_prompt, x_sample, state_lru_h, state_lru_conv, cache_swa_k, cache_swa_v, norm_mix, w_in, b_merge, conv_w, conv_b, w_rg_a, b_rg_a, w_rg_x, b_rg_x, lru_lambda, attn_sink, rel_bias, w_lru_proj, w_attn_proj, w_out, norm_ffn, w_group, b_group, w_router, b_router, w_e_gate, w_e_up, w_e_down, norm_final):
    f32, bf16 = jnp.float32, jnp.bfloat16
    l = 0
    w = w_in[l]
    w_kv2 = jnp.concatenate([_dup_heads(w[:, C_K:C_K + KV_W]), _dup_heads(w[:, C_V:C_V + KV_W])], axis=1).astype(bf16)
    w_rg = jnp.stack([
        jnp.concatenate([_block_diag_group(w_rg_a[l], j), _block_diag_group(w_rg_x[l], j)], axis=1)
        for j in range(N_RG_GROUPS)]).astype(bf16)
    w_rt = jnp.concatenate([w_router[l], w_group[l],
                            jnp.zeros((D_MODEL, ROUTER_W - N_EXPERTS - N_GROUPS), f32)], axis=1).astype(bf16)
    b_rt = jnp.concatenate([b_router[l], b_group[l], jnp.zeros((ROUTER_W - N_EXPERTS - N_GROUPS,), f32)])[None, :]
    weights = (
        norm_mix[l][None, :], w.astype(bf16), w_kv2, b_merge[l][None, :], conv_w[l], conv_b[l][None, :], w_rg,
        b_rg_a[l][None, :], b_rg_x[l][None, :], lru_lambda[l][None, :], attn_sink[l], _band_bias(rel_bias),
        w_lru_proj[l].astype(bf16), w_attn_proj[l].astype(bf16), w_out[l].astype(bf16), norm_ffn[l][None, :],
        w_rt, b_rt,
    )
    wg, wu = w_e_gate[l].astype(bf16), w_e_up[l].astype(bf16)
    wd = w_e_down[l].astype(bf16).reshape(N_GROUPS, GROUP_FF, D_MODEL)
    nf = norm_final[None, :]

    bp, tp, _ = x_prompt.shape
    assert bp == 1
    xp = x_prompt.reshape(tp, D_MODEL)
    zeros = lambda *s: jnp.zeros(s, f32)
    p_x1, p_xn2, p_comb, p_conv, p_h, p_k, p_v = _mixer(
        xp, zeros(1, SUBLANES, D_RNN), zeros(1, SUBLANES, D_RNN),
        jnp.zeros((1, HIST, KV2_W), bf16), jnp.zeros((1, HIST, KV2_W), bf16), weights,
        nseg=1, seg_len=256, mask_history=True, sparse_out=True)
    pos, tile_group, n_slots = _sorted_slots(p_comb, SORT_TILE)
    x_sorted, rec_sorted = _scatter_rows(pos, p_xn2, p_comb, n_slots, tile=SORT_TILE)
    o_sorted = _moe_sorted(tile_group, x_sorted, rec_sorted, wg, wu, wd, tile=SORT_TILE)
    y_p = _gather_norm(pos, o_sorted, p_x1, nf, tile=SORT_TILE).reshape(x_prompt.shape)

    bs, ts, _ = x_sample.shape
    xs = x_sample.reshape(bs * ts, D_MODEL)
    conv0 = jnp.pad(state_lru_conv[l], ((0, 0), (SUBLANES - (CONV_W - 1), 0), (0, 0)))
    h0 = jnp.broadcast_to(state_lru_h[l][:, None, :], (bs, SUBLANES, D_RNN))
    ck = cache_swa_k[l].reshape(bs, -1, KV_W)
    cv = cache_swa_v[l].reshape(bs, -1, KV_W)
    s_x1, s_xn2, s_comb, s_conv, s_h, s_k, s_v = _mixer(
        xs, conv0, h0, _dup_heads(ck).astype(bf16), _dup_heads(cv).astype(bf16), weights,
        nseg=bs, seg_len=ts, mask_history=False, sparse_out=False)
    y_s = _moe_groups(s_xn2, s_comb, s_x1, wg, wu, wd, nf, tile=bs * ts).reshape(x_sample.shape)

    tail = CONV_W - 1
    p_lru_h = p_h[:, 0, :][None]
    p_lru_conv = p_conv[:, SUBLANES - tail:, :][None]
    p_swa_k = _undup_heads(p_k)[None]
    p_swa_v = _undup_heads(p_v)[None]
    s_lru_h = s_h[:, 0, :][None]
    s_lru_conv = s_conv[:, SUBLANES - tail:, :][None]
    win = ck.shape[1]
    s_swa_k = jnp.concatenate([cache_swa_k[l][:, ts:], _undup_heads(s_k)], axis=1)[:, -win:][None]
    s_swa_v = jnp.concatenate([cache_swa_v[l][:, ts:], _undup_heads(s_v)], axis=1)[:, -win:][None]
    return (y_p, y_s, p_lru_h, p_lru_conv, p_swa_k, p_swa_v, s_lru_h, s_lru_conv, s_swa_k, s_swa_v)
```

```python
import functools
import math

import jax
import jax.numpy as jnp
from jax import lax
from jax.experimental import pallas as pl
from jax.experimental.pallas import tpu as pltpu

D_MODEL = 1024
CHUNK = 64
D_RNN = D_MODEL
CONV_W = 4
LRU_BLOCKS = 16
LRU_BLOCK_W = D_RNN // LRU_BLOCKS
LRU_C = 8.0
N_HEADS = 16
N_KV = 4
HEAD_DIM = 64
GROUP = N_HEADS // N_KV
WINDOW = 128
LOOKBACK_CHUNKS = -(-WINDOW // CHUNK)
HIST = LOOKBACK_CHUNKS * CHUNK
BAND = HIST + CHUNK
Q_W = N_HEADS * HEAD_DIM
KV_W = N_KV * HEAD_DIM
REL_BUCKETS = 32
REL_MAX_DIST = 128
N_GROUPS = 4
EXPERTS_PER_GROUP = 8
N_EXPERTS = N_GROUPS * EXPERTS_PER_GROUP
D_EXPERT = 256
EPS = 1e-6
NEG_INF = -1e30

LANES = 128
SUBLANES = 8
KV2_W = N_KV * LANES
RG_GROUP = 256
N_RG_GROUPS = D_RNN // RG_GROUP
ROUTER_W = LANES
VMEM_LIMIT = 56 * 1024 * 1024
PACK_W = D_MODEL // 2
GROUP_FF = EXPERTS_PER_GROUP * D_EXPERT
GID_LANE = EXPERTS_PER_GROUP
SORT_TILE = 512

C_LX, C_LG, C_Q = 0, D_RNN, 2 * D_RNN
C_K = C_Q + Q_W
C_V = C_K + KV_W
C_G = C_V + KV_W
V_STRIDE = 2 * LANES
SCAN_ROWS = SUBLANES * SUBLANES
LOG2E = math.log2(math.e)


def _rmsnorm(x, g):
    return x * lax.rsqrt(jnp.mean(x * x, axis=-1, keepdims=True) + EPS) * g


def _bdot(a, b):
    return jnp.dot(a, b, preferred_element_type=jnp.float32)


def _sigmoid(x):
    return 0.5 * jnp.tanh(0.5 * x) + 0.5


def _pack_bf16_pairs(x):
    return pltpu.pack_elementwise([x[:, :PACK_W], x[:, PACK_W:]], packed_dtype=jnp.bfloat16)


def _unpack_bf16_pairs(p):
    lo = pltpu.unpack_elementwise(p, index=0, packed_dtype=jnp.bfloat16, unpacked_dtype=jnp.float32)
    hi = pltpu.unpack_elementwise(p, index=1, packed_dtype=jnp.bfloat16, unpacked_dtype=jnp.float32)
    return jnp.concatenate([lo, hi], axis=1)


def _mixer_kernel(nseg, seg_len, mask_history, sparse_out,
                  x_ref, conv0_ref, h0_ref, k0_ref, v0_ref,
                  norm_mix_ref, w_in_ref, w_kv2_ref, b_merge_ref, conv_w_ref, conv_b_ref, w_rg_ref, b_rg_a_ref, b_rg_x_ref,
                  lam_ref, sink_ref, bias_ref, w_lru_ref, w_attn_ref, w_out_ref, norm_ffn_ref, w_rt_ref, b_rt_ref,
                  x1_ref, xn2_ref, comb_ref, conv_out_ref, h_out_ref, k_out_ref, v_out_ref,
                  xp_ref, hc_ref, kbuf_ref, vbuf_ref, a_ref, b_ref, attn_ref):
    step = pl.program_id(0)
    m_rows = nseg * seg_len
    n_chunks = seg_len // CHUNK
    keep = min(seg_len, HIST)

    @pl.when(step == 0)
    def _():
        xp_ref[...] = conv0_ref[...]
        hc_ref[...] = h0_ref[...]
        kbuf_ref[:, 0:HIST, :] = k0_ref[...]
        vbuf_ref[:, :, :] = jnp.ones(vbuf_ref.shape, jnp.bfloat16)
        for kv in range(N_KV):
            vbuf_ref[:, 0:HIST, kv * V_STRIDE:kv * V_STRIDE + LANES] = v0_ref[:, :, kv * LANES:(kv + 1) * LANES]

    x = x_ref[...]
    xn = _rmsnorm(x, norm_mix_ref[...]).astype(jnp.bfloat16)

    lru_x = _bdot(xn, w_in_ref[:, C_LX:C_LX + D_RNN])
    conv_w = conv_w_ref[...]
    row = lax.broadcasted_iota(jnp.int32, (SUBLANES, D_RNN), 0)
    xc_parts = []
    for s in range(nseg):
        xs = lru_x[s * seg_len:(s + 1) * seg_len, :]
        prev = xp_ref[s]
        tail = xs[seg_len - SUBLANES:, :]
        acc = conv_b_ref[...] + xs * conv_w[CONV_W - 1:CONV_W, :]
        for k in range(1, CONV_W):
            shifted = pltpu.roll(xs, k, axis=0)
            first = jnp.where(row < k, pltpu.roll(prev, k, axis=0), shifted[0:SUBLANES])
            shifted = jnp.concatenate([first, shifted[SUBLANES:]], axis=0)
            acc = acc + shifted * conv_w[CONV_W - 1 - k:CONV_W - k, :]
        xc_parts.append(acc)
        conv_out_ref[s] = tail
        xp_ref[s] = tail
    xc = xc_parts[0] if nseg == 1 else jnp.concatenate(xc_parts, axis=0)
    xc_b = xc.astype(jnp.bfloat16)

    lam = lam_ref[...]
    log_sig = jnp.minimum(lam, 0.0) - jnp.log1p(jnp.exp(-jnp.abs(lam)))
    c8 = LRU_C * log_sig
    for j in range(N_RG_GROUPS):
        cs = slice(j * RG_GROUP, (j + 1) * RG_GROUP)
        pre = _bdot(xc_b[:, cs], w_rg_ref[j])
        r = _sigmoid(pre[:, :RG_GROUP] + b_rg_a_ref[:, cs])
        i = _sigmoid(pre[:, RG_GROUP:] + b_rg_x_ref[:, cs])
        log_a = c8[:, cs] * r
        a_val = jnp.exp(log_a)
        th = jnp.tanh(log_a)
        b_val = jnp.sqrt(-2.0 * th / (1.0 - th)) * i * xc[:, cs]
        for t in range(RG_GROUP // LANES):
            lt = j * (RG_GROUP // LANES) + t
            a_ref[lt] = a_val[:, t * LANES:(t + 1) * LANES]
            b_ref[lt] = b_val[:, t * LANES:(t + 1) * LANES]

    row_c = lax.broadcasted_iota(jnp.int32, (SUBLANES, LANES), 0)
    for s in range(nseg):
        for lt in range(D_RNN // LANES):
            cs = slice(lt * LANES, (lt + 1) * LANES)
            carry = hc_ref[s, :, cs]
            for blk in range(seg_len // SCAN_ROWS):
                r0 = s * seg_len + blk * SCAN_ROWS
                slab = lambda ref, i: ref[lt, pl.ds(r0 + i, SUBLANES, stride=SUBLANES), :]
                a_loc, b_loc = [slab(a_ref, 0)], [slab(b_ref, 0)]
                for i in range(1, SUBLANES):
                    a_i = slab(a_ref, i)
                    b_loc.append(a_i * b_loc[-1] + slab(b_ref, i))
                    a_loc.append(a_i * a_loc[-1])
                a_e, b_e = a_loc[-1], b_loc[-1]
                for d in (1, 2, 4):
                    m = row_c >= d
                    b_e = jnp.where(m, a_e * pltpu.roll(b_e, d, axis=0) + b_e, b_e)
                    a_e = jnp.where(m, a_e * pltpu.roll(a_e, d, axis=0), a_e)
                h_end = a_e * carry + b_e
                c_in = jnp.where(row_c == 0, carry, pltpu.roll(h_end, 1, axis=0))
                for i in range(SUBLANES):
                    b_ref[lt, pl.ds(r0 + i, SUBLANES, stride=SUBLANES), :] = a_loc[i] * c_in + b_loc[i]
                carry = jnp.broadcast_to(h_end[SUBLANES - 1:SUBLANES, :], (SUBLANES, LANES))
            hc_ref[s, :, cs] = carry
            h_out_ref[s, :, cs] = carry

    lru_gate = _bdot(xn, w_in_ref[:, C_LG:C_LG + D_RNN])
    h_all = jnp.concatenate([b_ref[lt] for lt in range(D_RNN // LANES)], axis=1)
    lru_y = (h_all * jax.nn.gelu(lru_gate)).astype(jnp.bfloat16)
    mixed = _bdot(lru_y, w_lru_ref[...])
    gates = _bdot(xn, w_in_ref[:, C_G:C_G + D_MODEL]) + b_merge_ref[:, 0:D_MODEL]
    mixed = _sigmoid(gates) * mixed

    q = (_bdot(xn, w_in_ref[:, C_Q:C_Q + Q_W]) * (HEAD_DIM ** -0.5 * LOG2E)).astype(jnp.bfloat16)
    k2 = _bdot(xn, w_kv2_ref[:, 0:KV2_W])
    v2 = _bdot(xn, w_kv2_ref[:, KV2_W:2 * KV2_W])
    lane_q = lax.broadcasted_iota(jnp.int32, (CHUNK, LANES), 1)
    lane_o = lax.broadcasted_iota(jnp.int32, (CHUNK, LANES), 1)
    key_lane = lax.broadcasted_iota(jnp.int32, (1, BAND), 1)
    row_grp = lax.broadcasted_iota(jnp.int32, (GROUP * CHUNK, 1), 0) // CHUNK
    for s in range(nseg):
        rows = slice(s * seg_len, (s + 1) * seg_len)
        kbuf_ref[s, HIST:HIST + seg_len, :] = k2[rows].astype(jnp.bfloat16)
        for kv in range(N_KV):
            vbuf_ref[s, HIST:HIST + seg_len, kv * V_STRIDE:kv * V_STRIDE + LANES] = (
                v2[rows, kv * LANES:(kv + 1) * LANES].astype(jnp.bfloat16))
        k_out_ref[s] = k2[s * seg_len + seg_len - keep:(s + 1) * seg_len]
        v_out_ref[s] = v2[s * seg_len + seg_len - keep:(s + 1) * seg_len]
        for c in range(n_chunks):
            q_c = q[s * seg_len + c * CHUNK:s * seg_len + (c + 1) * CHUNK]
            slabs = []
            for kv in range(N_KV):
                parts = []
                for g in range(GROUP):
                    col = kv * GROUP * HEAD_DIM + (g // 2) * LANES
                    slab = q_c[:, col:col + LANES]
                    keep_lo = (g % 2) == 0
                    sel = (lane_q < HEAD_DIM) if keep_lo else (lane_q >= HEAD_DIM)
                    parts.append(jnp.where(sel, slab, jnp.zeros_like(slab)))
                q_stack = jnp.concatenate(parts, axis=0)
                k_band = kbuf_ref[s, c * CHUNK:c * CHUNK + BAND, kv * LANES:(kv + 1) * LANES]
                v_band = vbuf_ref[s, c * CHUNK:c * CHUNK + BAND, kv * V_STRIDE:(kv + 1) * V_STRIDE]
                sc = lax.dot_general(q_stack, k_band, (((1,), (1,)), ((), ())),
                                     preferred_element_type=jnp.float32)
                sc = sc + bias_ref[kv]
                if mask_history:
                    first_valid = HIST - (step * n_chunks + c) * CHUNK
                    sc = jnp.where(key_lane >= first_valid, sc, NEG_INF)
                sink = jnp.zeros((GROUP * CHUNK, 1), jnp.float32)
                for g in range(GROUP):
                    sink = jnp.where(row_grp == g, sink_ref[kv * GROUP + g] * LOG2E, sink)
                m = jnp.maximum(jnp.max(sc, axis=-1, keepdims=True), sink)
                p = jnp.exp2(sc - m).astype(jnp.bfloat16)
                o = _bdot(p, v_band)
                denom = o[:, LANES:2 * LANES] + jnp.exp2(sink - m)
                o = o[:, 0:LANES] / denom
                for pair in range(GROUP // 2):
                    lo = o[(2 * pair) * CHUNK:(2 * pair + 1) * CHUNK]
                    hi = o[(2 * pair + 1) * CHUNK:(2 * pair + 2) * CHUNK]
                    slabs.append(jnp.where(lane_o < HEAD_DIM, lo, hi))
            attn_ref[s * seg_len + c * CHUNK:s * seg_len + (c + 1) * CHUNK, :] = (
                jnp.concatenate(slabs, axis=1).astype(jnp.bfloat16))
        kbuf_ref[s, 0:HIST, :] = kbuf_ref[s, seg_len:seg_len + HIST, :]
        vbuf_ref[s, 0:HIST, :] = vbuf_ref[s, seg_len:seg_len + HIST, :]

    attn = _bdot(attn_ref[...], w_attn_ref[...])
    gates = _bdot(xn, w_in_ref[:, C_G + D_MODEL:C_G + 2 * D_MODEL]) + b_merge_ref[:, D_MODEL:2 * D_MODEL]
    mixed = (mixed + _sigmoid(gates) * attn).astype(jnp.bfloat16)
    x1 = x + _bdot(mixed, w_out_ref[...])
    x1_ref[...] = x1

    xn2 = _rmsnorm(x1, norm_ffn_ref[...])
    xn2_ref[...] = _pack_bf16_pairs(xn2) if sparse_out else xn2.astype(jnp.bfloat16)
    logits = _bdot(xn2.astype(jnp.bfloat16), w_rt_ref[...]) + b_rt_ref[...]
    lane = lax.broadcasted_iota(jnp.int32, (m_rows, ROUTER_W), 1).astype(jnp.float32)
    far = jnp.float32(2 * ROUTER_W)
    is_group = (lane >= N_EXPERTS) & (lane < N_EXPERTS + N_GROUPS)
    gl = jnp.where(is_group, logits, NEG_INF)
    g_max = jnp.max(gl, axis=-1, keepdims=True)
    g_idx = jnp.min(jnp.where(gl == g_max, lane, far), axis=-1, keepdims=True) - N_EXPERTS
    g_w = 1.0 / jnp.sum(jnp.where(is_group, jnp.exp(gl - g_max), 0.0), axis=-1, keepdims=True)
    in_group = (lane < N_EXPERTS) & (jnp.floor(lane * (1.0 / EXPERTS_PER_GROUP)) == g_idx)
    el = jnp.where(in_group, logits, NEG_INF)
    e1 = jnp.max(el, axis=-1, keepdims=True)
    i1 = jnp.min(jnp.where(el == e1, lane, far), axis=-1, keepdims=True)
    el2 = jnp.where(lane == i1, NEG_INF, el)
    e2 = jnp.max(el2, axis=-1, keepdims=True)
    i2 = jnp.min(jnp.where(el2 == e2, lane, far), axis=-1, keepdims=True)
    t = jnp.exp(e2 - e1)
    w1 = g_w / (1.0 + t)
    w2 = w1 * t
    if sparse_out:
        e_lane = lane + g_idx * EXPERTS_PER_GROUP
        rec = jnp.where(e_lane == i1, w1, 0.0) + jnp.where(e_lane == i2, w2, 0.0)
        rec = jnp.where(lane < EXPERTS_PER_GROUP, rec, 0.0)
        comb_ref[...] = jnp.where(lane == GID_LANE, g_idx, rec)
    else:
        comb_ref[...] = jnp.where(lane == i1, w1, 0.0) + jnp.where(lane == i2, w2, 0.0)


def _const_spec(shape):
    zeros = (0,) * len(shape)
    return pl.BlockSpec(shape, lambda i: zeros, pipeline_mode=pl.Buffered(1))


def _mixer(x, conv0, h0, k0, v0, weights, *, nseg, seg_len, mask_history, sparse_out):
    n_tok = x.shape[0]
    m_rows = nseg * seg_len
    n_steps = n_tok // m_rows
    keep = min(seg_len, HIST)
    row_spec = lambda w: pl.BlockSpec((m_rows, w), lambda i: (i, 0))
    in_specs = [row_spec(D_MODEL), _const_spec(conv0.shape), _const_spec(h0.shape), _const_spec(k0.shape),
                _const_spec(v0.shape)]
    for w in weights:
        if w.ndim == 1:
            in_specs.append(pl.BlockSpec(memory_space=pltpu.SMEM))
        else:
            in_specs.append(_const_spec(w.shape))
    state_spec = lambda r, w: pl.BlockSpec((nseg, r, w), lambda i: (0, 0, 0))
    out_shape = (
        jax.ShapeDtypeStruct((n_tok, D_MODEL), jnp.float32),
        (jax.ShapeDtypeStruct((n_tok, PACK_W), jnp.uint32) if sparse_out
         else jax.ShapeDtypeStruct((n_tok, D_MODEL), jnp.bfloat16)),
        jax.ShapeDtypeStruct((n_tok, ROUTER_W), jnp.float32),
        jax.ShapeDtypeStruct((nseg, SUBLANES, D_RNN), jnp.float32),
        jax.ShapeDtypeStruct((nseg, SUBLANES, D_RNN), jnp.float32),
        jax.ShapeDtypeStruct((nseg, keep, KV2_W), jnp.float32),
        jax.ShapeDtypeStruct((nseg, keep, KV2_W), jnp.float32),
    )
    out_specs = (row_spec(D_MODEL), row_spec(PACK_W if sparse_out else D_MODEL), row_spec(ROUTER_W),
                 state_spec(SUBLANES, D_RNN), state_spec(SUBLANES, D_RNN),
                 state_spec(keep, KV2_W), state_spec(keep, KV2_W))
    scratch = [
        pltpu.VMEM((nseg, SUBLANES, D_RNN), jnp.float32),
        pltpu.VMEM((nseg, SUBLANES, D_RNN), jnp.float32),
        pltpu.VMEM((nseg, HIST + seg_len, KV2_W), jnp.bfloat16),
        pltpu.VMEM((nseg, HIST + seg_len, N_KV * V_STRIDE), jnp.bfloat16),
        pltpu.VMEM((D_RNN // LANES, m_rows, LANES), jnp.float32),
        pltpu.VMEM((D_RNN // LANES, m_rows, LANES), jnp.float32),
        pltpu.VMEM((m_rows, Q_W), jnp.bfloat16),
    ]
    return pl.pallas_call(
        functools.partial(_mixer_kernel, nseg, seg_len, mask_history, sparse_out),
        grid=(n_steps,),
        in_specs=in_specs,
        out_specs=out_specs,
        out_shape=out_shape,
        scratch_shapes=scratch,
        compiler_params=pltpu.CompilerParams(dimension_semantics=("arbitrary",), vmem_limit_bytes=VMEM_LIMIT),
        name="mixer_prompt" if mask_history else "mixer_sample",
    )(x, conv0, h0, k0, v0, *weights)


def _group_swiglu(x, cols, wg_ref, wu_ref, wd_ref):
    parts = []
    for e in range(EXPERTS_PER_GROUP):
        h = jax.nn.silu(_bdot(x, wg_ref[e])) * _bdot(x, wu_ref[e])
        parts.append((h * cols[e]).astype(jnp.bfloat16))
    return _bdot(jnp.concatenate(parts, axis=1), wd_ref[...])


def _moe_groups_kernel(xn2_ref, comb_ref, x1_ref, wg_ref, wu_ref, wd_ref, norm_ref, y_ref, acc_ref):
    g = pl.program_id(1)

    @pl.when(g == 0)
    def _():
        acc_ref[...] = jnp.zeros_like(acc_ref)

    comb = comb_ref[...]
    lane = lax.broadcasted_iota(jnp.int32, comb.shape, 1)
    cols = [jnp.sum(jnp.where(lane == g * EXPERTS_PER_GROUP + e, comb, 0.0), axis=-1, keepdims=True)
            for e in range(EXPERTS_PER_GROUP)]
    acc_ref[...] += _group_swiglu(xn2_ref[...], cols, wg_ref, wu_ref, wd_ref)

    @pl.when(g == N_GROUPS - 1)
    def _():
        y_ref[...] = _rmsnorm(x1_ref[...] + acc_ref[...], norm_ref[...])


def _group_weight_specs(group_of):
    return [pl.BlockSpec((EXPERTS_PER_GROUP, D_MODEL, D_EXPERT), lambda *a: (group_of(*a), 0, 0)),
            pl.BlockSpec((EXPERTS_PER_GROUP, D_MODEL, D_EXPERT), lambda *a: (group_of(*a), 0, 0)),
            pl.BlockSpec((None, GROUP_FF, D_MODEL), lambda *a: (group_of(*a), 0, 0))]


def _moe_groups(xn2, comb, x1, wg, wu, wd, norm_final, *, tile):
    n_tok = xn2.shape[0]
    row_spec = lambda w: pl.BlockSpec((tile, w), lambda i, g: (i, 0))
    return pl.pallas_call(
        _moe_groups_kernel,
        grid=(n_tok // tile, N_GROUPS),
        in_specs=[row_spec(D_MODEL), row_spec(ROUTER_W), row_spec(D_MODEL),
                  *_group_weight_specs(lambda i, g: g),
                  pl.BlockSpec((1, D_MODEL), lambda i, g: (0, 0))],
        out_specs=row_spec(D_MODEL),
        out_shape=jax.ShapeDtypeStruct((n_tok, D_MODEL), jnp.float32),
        scratch_shapes=[pltpu.VMEM((tile, D_MODEL), jnp.float32)],
        compiler_params=pltpu.CompilerParams(dimension_semantics=("arbitrary", "arbitrary"),
                                             vmem_limit_bytes=VMEM_LIMIT),
        name="moe_groups",
    )(xn2, comb, x1, wg, wu, wd, norm_final)


def _scatter_rows_kernel(tile, pos_ref, x_ref, rec_ref, xs_ref, recs_ref):
    i = pl.program_id(0)

    @pl.when(i == 0)
    def _():
        xs_ref[...] = jnp.zeros_like(xs_ref)
        recs_ref[...] = jnp.zeros_like(recs_ref)

    def body(j, carry):
        r0 = pl.multiple_of(j * SUBLANES, SUBLANES)
        xb = x_ref[pl.ds(r0, SUBLANES), :]
        rb = rec_ref[pl.ds(r0, SUBLANES), :]
        for k in range(SUBLANES):
            p = pos_ref[i * tile + j * SUBLANES + k]
            xs_ref[pl.ds(p, 1), :] = xb[k:k + 1, :]
            recs_ref[pl.ds(p, 1), :] = rb[k:k + 1, :]
        return carry

    lax.fori_loop(0, tile // SUBLANES, body, 0)


def _scatter_rows(pos, xn2p, rec, n_slots, *, tile):
    n_tok = xn2p.shape[0]
    return pl.pallas_call(
        functools.partial(_scatter_rows_kernel, tile),
        grid_spec=pltpu.PrefetchScalarGridSpec(
            num_scalar_prefetch=1,
            grid=(n_tok // tile,),
            in_specs=[pl.BlockSpec((tile, PACK_W), lambda i, pos: (i, 0)),
                      pl.BlockSpec((tile, ROUTER_W), lambda i, pos: (i, 0))],
            out_specs=[pl.BlockSpec((n_slots, PACK_W), lambda i, pos: (0, 0)),
                       pl.BlockSpec((n_slots, ROUTER_W), lambda i, pos: (0, 0))],
        ),
        out_shape=(jax.ShapeDtypeStruct((n_slots, PACK_W), jnp.uint32),
                   jax.ShapeDtypeStruct((n_slots, ROUTER_W), jnp.float32)),
        compiler_params=pltpu.CompilerParams(dimension_semantics=("arbitrary",), vmem_limit_bytes=VMEM_LIMIT),
        name="moe_scatter",
    )(pos, xn2p, rec)


def _moe_sorted_kernel(tg_ref, xs_ref, recs_ref, wg_ref, wu_ref, wd_ref, o_ref):
    x = _unpack_bf16_pairs(xs_ref[...]).astype(jnp.bfloat16)
    rec = recs_ref[...]
    cols = [rec[:, e:e + 1] for e in range(EXPERTS_PER_GROUP)]
    o_ref[...] = _pack_bf16_pairs(_group_swiglu(x, cols, wg_ref, wu_ref, wd_ref))


def _moe_sorted(tile_group, xs, recs, wg, wu, wd, *, tile):
    n_slots = xs.shape[0]
    return pl.pallas_call(
        _moe_sorted_kernel,
        grid_spec=pltpu.PrefetchScalarGridSpec(
            num_scalar_prefetch=1,
            grid=(n_slots // tile,),
            in_specs=[pl.BlockSpec((tile, PACK_W), lambda i, tg: (i, 0)),
                      pl.BlockSpec((tile, ROUTER_W), lambda i, tg: (i, 0)),
                      *_group_weight_specs(lambda i, tg: tg[i])],
            out_specs=pl.BlockSpec((tile, PACK_W), lambda i, tg: (i, 0)),
        ),
        out_shape=jax.ShapeDtypeStruct((n_slots, PACK_W), jnp.uint32),
        compiler_params=pltpu.CompilerParams(dimension_semantics=("arbitrary",), vmem_limit_bytes=VMEM_LIMIT),
        name="moe_sorted",
    )(tile_group, xs, recs, wg, wu, wd)


def _gather_norm_kernel(tile, pos_ref, os_ref, x1_ref, norm_ref, y_ref, buf_ref):
    i = pl.program_id(0)

    def body(j, carry):
        r0 = pl.multiple_of(j * SUBLANES, SUBLANES)
        rows = [os_ref[pl.ds(pos_ref[i * tile + j * SUBLANES + k], 1), :] for k in range(SUBLANES)]
        buf_ref[pl.ds(r0, SUBLANES), :] = jnp.concatenate(rows, axis=0)
        return carry

    lax.fori_loop(0, tile // SUBLANES, body, 0)
    y_ref[...] = _rmsnorm(x1_ref[...] + _unpack_bf16_pairs(buf_ref[...]), norm_ref[...])


def _gather_norm(pos, o_sorted, x1, norm_final, *, tile):
    n_tok = x1.shape[0]
    n_slots = o_sorted.shape[0]
    return pl.pallas_call(
        functools.partial(_gather_norm_kernel, tile),
        grid_spec=pltpu.PrefetchScalarGridSpec(
            num_scalar_prefetch=1,
            grid=(n_tok // tile,),
            in_specs=[pl.BlockSpec((n_slots, PACK_W), lambda i, pos: (0, 0), pipeline_mode=pl.Buffered(1)),
                      pl.BlockSpec((tile, D_MODEL), lambda i, pos: (i, 0)),
                      pl.BlockSpec((1, D_MODEL), lambda i, pos: (0, 0))],
            out_specs=pl.BlockSpec((tile, D_MODEL), lambda i, pos: (i, 0)),
            scratch_shapes=[pltpu.VMEM((tile, PACK_W), jnp.uint32)],
        ),
        out_shape=jax.ShapeDtypeStruct((n_tok, D_MODEL), jnp.float32),
        compiler_params=pltpu.CompilerParams(dimension_semantics=("arbitrary",), vmem_limit_bytes=VMEM_LIMIT),
        name="moe_gather_norm",
    )(pos, o_sorted, x1, norm_final)


def _sorted_slots(rec, tile):
    n_tok = rec.shape[0]
    gid = rec[:, GID_LANE].astype(jnp.int32)
    onehot = (gid[:, None] == jnp.arange(N_GROUPS, dtype=jnp.int32)[None, :]).astype(jnp.int32)
    csum = jnp.cumsum(onehot, axis=0)
    rank = jnp.sum((csum - onehot) * onehot, axis=1)
    padded = ((csum[-1] + tile - 1) // tile) * tile
    end = jnp.cumsum(padded)
    pos = jnp.sum(onehot * (end - padded)[None, :], axis=1) + rank
    n_tiles = n_tok // tile + N_GROUPS
    tile_start = jnp.arange(n_tiles, dtype=jnp.int32) * tile
    tile_group = jnp.minimum(jnp.sum((tile_start[:, None] >= end[None, :]).astype(jnp.int32), axis=1), N_GROUPS - 1)
    return pos.astype(jnp.int32), tile_group.astype(jnp.int32), n_tiles * tile


def _rel_bucket(rel):
    nb = REL_BUCKETS // 2
    n = -rel
    ret = jnp.where(n < 0, nb, 0)
    n = jnp.abs(n)
    max_exact = nb // 2
    nf = jnp.maximum(n, 1).astype(jnp.float32)
    large = max_exact + (jnp.log(nf / max_exact) / math.log(REL_MAX_DIST / max_exact) * (nb - max_exact)).astype(jnp.int32)
    large = jnp.minimum(large, nb - 1)
    return ret + jnp.where(n < max_exact, n, large)


def _band_bias(rel_table):
    lo = -(HIST + CHUNK - 1)
    rels = jnp.arange(lo, CHUNK, dtype=jnp.int32)
    by_rel = jnp.transpose(rel_table[_rel_bucket(rels)]).astype(jnp.float32)
    rows = [by_rel[:, (CHUNK - 1 - qi):(CHUNK - 1 - qi) + BAND] for qi in range(CHUNK)]
    return jnp.stack(rows, axis=1).reshape(N_KV, GROUP * CHUNK, BAND)


def _dup_heads(a):
    a = a.reshape(a.shape[:-1] + (N_KV, HEAD_DIM))
    return jnp.concatenate([a, a], axis=-1).reshape(a.shape[:-2] + (KV2_W,))


def _undup_heads(a):
    a = a.reshape(a.shape[:-1] + (N_KV, LANES))
    return a[..., :HEAD_DIM]


def _block_diag_group(w, j):
    per = RG_GROUP // LRU_BLOCK_W
    blk = w[j * per:(j + 1) * per]
    eye = jnp.eye(per, dtype=w.dtype)
    return jnp.einsum('nde,nm->ndme', blk, eye).reshape(RG_GROUP, RG_GROUP)


def kernel(x_prompt, x_sample, state_lru_h, state_lru_conv, cache_swa_k, cache_swa_v, norm_mix, w_in, b_merge, conv_w, conv_b, w_rg_a, b_rg_a, w_rg_x, b_rg_x, lru_lambda, attn_sink, rel_bias, w_lru_proj, w_attn_proj, w_out, norm_ffn, w_group, b_group, w_router, b_router, w_e_gate, w_e_up, w_e_down, norm_final):
    f32, bf16 = jnp.float32, jnp.bfloat16
    l = 0
    w = w_in[l]
    w_kv2 = jnp.concatenate([_dup_heads(w[:, C_K:C_K + KV_W]), _dup_heads(w[:, C_V:C_V + KV_W])], axis=1).astype(bf16)
    w_rg = jnp.stack([
        jnp.concatenate([_block_diag_group(w_rg_a[l], j), _block_diag_group(w_rg_x[l], j)], axis=1)
        for j in range(N_RG_GROUPS)]).astype(bf16)
    w_rt = jnp.concatenate([w_router[l], w_group[l],
                            jnp.zeros((D_MODEL, ROUTER_W - N_EXPERTS - N_GROUPS), f32)], axis=1).astype(bf16)
    b_rt = jnp.concatenate([b_router[l], b_group[l], jnp.zeros((ROUTER_W - N_EXPERTS - N_GROUPS,), f32)])[None, :]
    weights = (
        norm_mix[l][None, :], w.astype(bf16), w_kv2, b_merge[l][None, :], conv_w[l], conv_b[l][None, :], w_rg,
        b_rg_a[l][None, :], b_rg_x[l][None, :], lru_lambda[l][None, :], attn_sink[l], _band_bias(rel_bias) * LOG2E,
        w_lru_proj[l].astype(bf16), w_attn_proj[l].astype(bf16), w_out[l].astype(bf16), norm_ffn[l][None, :],
        w_rt, b_rt,
    )
    wg, wu = w_e_gate[l].astype(bf16), w_e_up[l].astype(bf16)
    wd = w_e_down[l].astype(bf16).reshape(N_GROUPS, GROUP_FF, D_MODEL)
    nf = norm_final[None, :]

    bp, tp, _ = x_prompt.shape
    assert bp == 1
    xp = x_prompt.reshape(tp, D_MODEL)
    zeros = lambda *s: jnp.zeros(s, f32)
    p_x1, p_xn2, p_comb, p_conv, p_h, p_k, p_v = _mixer(
        xp, zeros(1, SUBLANES, D_RNN), zeros(1, SUBLANES, D_RNN),
        jnp.zeros((1, HIST, KV2_W), bf16), jnp.zeros((1, HIST, KV2_W), bf16), weights,
        nseg=1, seg_len=256, mask_history=True, sparse_out=True)
    pos, tile_group, n_slots = _sorted_slots(p_comb, SORT_TILE)
    x_sorted, rec_sorted = _scatter_rows(pos, p_xn2, p_comb, n_slots, tile=SORT_TILE)
    o_sorted = _moe_sorted(tile_group, x_sorted, rec_sorted, wg, wu, wd, tile=SORT_TILE)
    y_p = _gather_norm(pos, o_sorted, p_x1, nf, tile=SORT_TILE).reshape(x_prompt.shape)

    bs, ts, _ = x_sample.shape
    xs = x_sample.reshape(bs * ts, D_MODEL)
    conv0 = jnp.pad(state_lru_conv[l], ((0, 0), (SUBLANES - (CONV_W - 1), 0), (0, 0)))
    h0 = jnp.broadcast_to(state_lru_h[l][:, None, :], (bs, SUBLANES, D_RNN))
    ck = cache_swa_k[l].reshape(bs, -1, KV_W)
    cv = cache_swa_v[l].reshape(bs, -1, KV_W)
    s_x1, s_xn2, s_comb, s_conv, s_h, s_k, s_v = _mixer(
        xs, conv0, h0, _dup_heads(ck).astype(bf16), _dup_heads(cv).astype(bf16), weights,
        nseg=bs, seg_len=ts, mask_history=False, sparse_out=False)
    y_s = _moe_groups(s_xn2, s_comb, s_x1, wg, wu, wd, nf, tile=bs * ts).reshape(x_sample.shape)

    tail = CONV_W - 1
    p_lru_h = p_h[:, 0, :][None]
    p_lru_conv = p_conv[:, SUBLANES - tail:, :][None]
    p_swa_k = _undup_heads(p_k)[None]
    p_swa_v = _undup_heads(p_v)[None]
    s_lru_h = s_h[:, 0, :][None]
    s_lru_conv = s_conv[:, SUBLANES - tail:, :][None]
    win = ck.shape[1]
    s_swa_k = jnp.concatenate([cache_swa_k[l][:, ts:], _undup_heads(s_k)], axis=1)[:, -win:][None]
    s_swa_v = jnp.concatenate([cache_swa_v[l][:, ts:], _undup_heads(s_v)], axis=1)[:, -win:][None]
    return (y_p, y_s, p_lru_h, p_lru_conv, p_swa_k, p_swa_v, s_lru_h, s_lru_conv, s_swa_k, s_swa_v)
```

```python
import functools
import math

import jax
import jax.numpy as jnp
from jax import lax
from jax.experimental import pallas as pl
from jax.experimental.pallas import tpu as pltpu

D_MODEL = 1024
CHUNK = 64
D_RNN = D_MODEL
CONV_W = 4
LRU_BLOCKS = 16
LRU_BLOCK_W = D_RNN // LRU_BLOCKS
LRU_C = 8.0
N_HEADS = 16
N_KV = 4
HEAD_DIM = 64
GROUP = N_HEADS // N_KV
WINDOW = 128
LOOKBACK_CHUNKS = -(-WINDOW // CHUNK)
HIST = LOOKBACK_CHUNKS * CHUNK
BAND = HIST + CHUNK
Q_W = N_HEADS * HEAD_DIM
KV_W = N_KV * HEAD_DIM
REL_BUCKETS = 32
REL_MAX_DIST = 128
N_GROUPS = 4
EXPERTS_PER_GROUP = 8
N_EXPERTS = N_GROUPS * EXPERTS_PER_GROUP
D_EXPERT = 256
EPS = 1e-6
NEG_INF = -1e30

LANES = 128
SUBLANES = 8
KV2_W = N_KV * LANES
RG_GROUP = 256
N_RG_GROUPS = D_RNN // RG_GROUP
ROUTER_W = LANES
VMEM_LIMIT = 56 * 1024 * 1024
PACK_W = D_MODEL // 2
GROUP_FF = EXPERTS_PER_GROUP * D_EXPERT
GID_LANE = EXPERTS_PER_GROUP
SORT_TILE = 512
PROMPT_TILE = 512

C_LX, C_LG, C_Q = 0, D_RNN, 2 * D_RNN
C_K = C_Q + Q_W
C_V = C_K + KV_W
C_G = C_V + KV_W
V_STRIDE = 2 * LANES
SCAN_ROWS = SUBLANES * SUBLANES
LOG2E = math.log2(math.e)


def _rmsnorm(x, g):
    return x * lax.rsqrt(jnp.mean(x * x, axis=-1, keepdims=True) + EPS) * g


def _bdot(a, b):
    return jnp.dot(a, b, preferred_element_type=jnp.float32)


def _sigmoid(x):
    return 0.5 * jnp.tanh(0.5 * x) + 0.5


def _pack_bf16_pairs(x):
    return pltpu.pack_elementwise([x[:, :PACK_W], x[:, PACK_W:]], packed_dtype=jnp.bfloat16)


def _unpack_bf16_pairs(p):
    lo = pltpu.unpack_elementwise(p, index=0, packed_dtype=jnp.bfloat16, unpacked_dtype=jnp.float32)
    hi = pltpu.unpack_elementwise(p, index=1, packed_dtype=jnp.bfloat16, unpacked_dtype=jnp.float32)
    return jnp.concatenate([lo, hi], axis=1)


def _mixer_kernel(nseg, seg_len, mask_history, sparse_out,
                  x_ref, conv0_ref, h0_ref, k0_ref, v0_ref,
                  norm_mix_ref, w_in_ref, w_kv2_ref, b_merge_ref, conv_w_ref, conv_b_ref, w_rg_ref, b_rg_a_ref, b_rg_x_ref,
                  lam_ref, sink_ref, bias_ref, w_lru_ref, w_attn_ref, w_out_ref, norm_ffn_ref, w_rt_ref, b_rt_ref,
                  x1_ref, xn2_ref, comb_ref, conv_out_ref, h_out_ref, k_out_ref, v_out_ref,
                  xp_ref, hc_ref, kbuf_ref, vbuf_ref, a_ref, b_ref, attn_ref):
    step = pl.program_id(0)
    m_rows = nseg * seg_len
    n_chunks = seg_len // CHUNK
    keep = min(seg_len, HIST)

    @pl.when(step == 0)
    def _():
        xp_ref[...] = conv0_ref[...]
        hc_ref[...] = h0_ref[...]
        kbuf_ref[:, 0:HIST, :] = k0_ref[...]
        vbuf_ref[:, :, :] = jnp.ones(vbuf_ref.shape, jnp.bfloat16)
        for kv in range(N_KV):
            vbuf_ref[:, 0:HIST, kv * V_STRIDE:kv * V_STRIDE + LANES] = v0_ref[:, :, kv * LANES:(kv + 1) * LANES]

    x = x_ref[...]
    xn = _rmsnorm(x, norm_mix_ref[...]).astype(jnp.bfloat16)

    lru_x = _bdot(xn, w_in_ref[:, C_LX:C_LX + D_RNN])
    conv_w = conv_w_ref[...]
    row = lax.broadcasted_iota(jnp.int32, (SUBLANES, D_RNN), 0)
    xc_parts = []
    for s in range(nseg):
        xs = lru_x[s * seg_len:(s + 1) * seg_len, :]
        prev = xp_ref[s]
        tail = xs[seg_len - SUBLANES:, :]
        acc = conv_b_ref[...] + xs * conv_w[CONV_W - 1:CONV_W, :]
        for k in range(1, CONV_W):
            shifted = pltpu.roll(xs, k, axis=0)
            first = jnp.where(row < k, pltpu.roll(prev, k, axis=0), shifted[0:SUBLANES])
            shifted = jnp.concatenate([first, shifted[SUBLANES:]], axis=0)
            acc = acc + shifted * conv_w[CONV_W - 1 - k:CONV_W - k, :]
        xc_parts.append(acc)
        conv_out_ref[s] = tail
        xp_ref[s] = tail
    xc = xc_parts[0] if nseg == 1 else jnp.concatenate(xc_parts, axis=0)
    xc_b = xc.astype(jnp.bfloat16)

    lam = lam_ref[...]
    log_sig = jnp.minimum(lam, 0.0) - jnp.log1p(jnp.exp(-jnp.abs(lam)))
    c8 = LRU_C * log_sig
    for j in range(N_RG_GROUPS):
        cs = slice(j * RG_GROUP, (j + 1) * RG_GROUP)
        pre = _bdot(xc_b[:, cs], w_rg_ref[j])
        r = _sigmoid(pre[:, :RG_GROUP] + b_rg_a_ref[:, cs])
        i = _sigmoid(pre[:, RG_GROUP:] + b_rg_x_ref[:, cs])
        log_a = c8[:, cs] * r
        a_val = jnp.exp(log_a)
        th = jnp.tanh(log_a)
        b_val = jnp.sqrt(-2.0 * th / (1.0 - th)) * i * xc[:, cs]
        for t in range(RG_GROUP // LANES):
            lt = j * (RG_GROUP // LANES) + t
            a_ref[lt] = a_val[:, t * LANES:(t + 1) * LANES]
            b_ref[lt] = b_val[:, t * LANES:(t + 1) * LANES]

    row_c = lax.broadcasted_iota(jnp.int32, (SUBLANES, LANES), 0)
    for s in range(nseg):
        for lt in range(D_RNN // LANES):
            cs = slice(lt * LANES, (lt + 1) * LANES)
            carry = hc_ref[s, :, cs]
            for blk in range(seg_len // SCAN_ROWS):
                r0 = s * seg_len + blk * SCAN_ROWS
                slab = lambda ref, i: ref[lt, pl.ds(r0 + i, SUBLANES, stride=SUBLANES), :]
                a_loc, b_loc = [slab(a_ref, 0)], [slab(b_ref, 0)]
                for i in range(1, SUBLANES):
                    a_i = slab(a_ref, i)
                    b_loc.append(a_i * b_loc[-1] + slab(b_ref, i))
                    a_loc.append(a_i * a_loc[-1])
                a_e, b_e = a_loc[-1], b_loc[-1]
                for d in (1, 2, 4):
                    m = row_c >= d
                    b_e = jnp.where(m, a_e * pltpu.roll(b_e, d, axis=0) + b_e, b_e)
                    a_e = jnp.where(m, a_e * pltpu.roll(a_e, d, axis=0), a_e)
                h_end = a_e * carry + b_e
                c_in = jnp.where(row_c == 0, carry, pltpu.roll(h_end, 1, axis=0))
                for i in range(SUBLANES):
                    b_ref[lt, pl.ds(r0 + i, SUBLANES, stride=SUBLANES), :] = a_loc[i] * c_in + b_loc[i]
                carry = jnp.broadcast_to(h_end[SUBLANES - 1:SUBLANES, :], (SUBLANES, LANES))
            hc_ref[s, :, cs] = carry
            h_out_ref[s, :, cs] = carry

    lru_gate = _bdot(xn, w_in_ref[:, C_LG:C_LG + D_RNN])
    h_all = jnp.concatenate([b_ref[lt] for lt in range(D_RNN // LANES)], axis=1)
    lru_y = (h_all * jax.nn.gelu(lru_gate)).astype(jnp.bfloat16)
    mixed = _bdot(lru_y, w_lru_ref[...])
    gates = _bdot(xn, w_in_ref[:, C_G:C_G + D_MODEL]) + b_merge_ref[:, 0:D_MODEL]
    mixed = _sigmoid(gates) * mixed

    q = (_bdot(xn, w_in_ref[:, C_Q:C_Q + Q_W]) * (HEAD_DIM ** -0.5 * LOG2E)).astype(jnp.bfloat16)
    k2 = _bdot(xn, w_kv2_ref[:, 0:KV2_W])
    v2 = _bdot(xn, w_kv2_ref[:, KV2_W:2 * KV2_W])
    lane_q = lax.broadcasted_iota(jnp.int32, (CHUNK, LANES), 1)
    lane_o = lax.broadcasted_iota(jnp.int32, (CHUNK, LANES), 1)
    key_lane = lax.broadcasted_iota(jnp.int32, (1, BAND), 1)
    row_grp = lax.broadcasted_iota(jnp.int32, (GROUP * CHUNK, 1), 0) // CHUNK
    for s in range(nseg):
        rows = slice(s * seg_len, (s + 1) * seg_len)
        kbuf_ref[s, HIST:HIST + seg_len, :] = k2[rows].astype(jnp.bfloat16)
        for kv in range(N_KV):
            vbuf_ref[s, HIST:HIST + seg_len, kv * V_STRIDE:kv * V_STRIDE + LANES] = (
                v2[rows, kv * LANES:(kv + 1) * LANES].astype(jnp.bfloat16))
        k_out_ref[s] = k2[s * seg_len + seg_len - keep:(s + 1) * seg_len]
        v_out_ref[s] = v2[s * seg_len + seg_len - keep:(s + 1) * seg_len]
        for c in range(n_chunks):
            q_c = q[s * seg_len + c * CHUNK:s * seg_len + (c + 1) * CHUNK]
            slabs = []
            for kv in range(N_KV):
                parts = []
                for g in range(GROUP):
                    col = kv * GROUP * HEAD_DIM + (g // 2) * LANES
                    slab = q_c[:, col:col + LANES]
                    keep_lo = (g % 2) == 0
                    sel = (lane_q < HEAD_DIM) if keep_lo else (lane_q >= HEAD_DIM)
                    parts.append(jnp.where(sel, slab, jnp.zeros_like(slab)))
                q_stack = jnp.concatenate(parts, axis=0)
                k_band = kbuf_ref[s, c * CHUNK:c * CHUNK + BAND, kv * LANES:(kv + 1) * LANES]
                v_band = vbuf_ref[s, c * CHUNK:c * CHUNK + BAND, kv * V_STRIDE:(kv + 1) * V_STRIDE]
                sc = lax.dot_general(q_stack, k_band, (((1,), (1,)), ((), ())),
                                     preferred_element_type=jnp.float32)
                sc = sc + bias_ref[kv]
                if mask_history:
                    first_valid = HIST - (step * n_chunks + c) * CHUNK
                    sc = jnp.where(key_lane >= first_valid, sc, NEG_INF)
                sink = jnp.zeros((GROUP * CHUNK, 1), jnp.float32)
                for g in range(GROUP):
                    sink = jnp.where(row_grp == g, sink_ref[kv * GROUP + g] * LOG2E, sink)
                m = jnp.maximum(jnp.max(sc, axis=-1, keepdims=True), sink)
                p = jnp.exp2(sc - m).astype(jnp.bfloat16)
                o = _bdot(p, v_band)
                denom = o[:, LANES:2 * LANES] + jnp.exp2(sink - m)
                o = o[:, 0:LANES] / denom
                for pair in range(GROUP // 2):
                    lo = o[(2 * pair) * CHUNK:(2 * pair + 1) * CHUNK]
                    hi = o[(2 * pair + 1) * CHUNK:(2 * pair + 2) * CHUNK]
                    slabs.append(jnp.where(lane_o < HEAD_DIM, lo, hi))
            attn_ref[s * seg_len + c * CHUNK:s * seg_len + (c + 1) * CHUNK, :] = (
                jnp.concatenate(slabs, axis=1).astype(jnp.bfloat16))
        kbuf_ref[s, 0:HIST, :] = kbuf_ref[s, seg_len:seg_len + HIST, :]
        vbuf_ref[s, 0:HIST, :] = vbuf_ref[s, seg_len:seg_len + HIST, :]

    attn = _bdot(attn_ref[...], w_attn_ref[...])
    gates = _bdot(xn, w_in_ref[:, C_G + D_MODEL:C_G + 2 * D_MODEL]) + b_merge_ref[:, D_MODEL:2 * D_MODEL]
    mixed = (mixed + _sigmoid(gates) * attn).astype(jnp.bfloat16)
    x1 = x + _bdot(mixed, w_out_ref[...])
    x1_ref[...] = x1

    xn2 = _rmsnorm(x1, norm_ffn_ref[...])
    xn2_ref[...] = _pack_bf16_pairs(xn2) if sparse_out else xn2.astype(jnp.bfloat16)
    logits = _bdot(xn2.astype(jnp.bfloat16), w_rt_ref[...]) + b_rt_ref[...]
    lane = lax.broadcasted_iota(jnp.int32, (m_rows, ROUTER_W), 1).astype(jnp.float32)
    far = jnp.float32(2 * ROUTER_W)
    is_group = (lane >= N_EXPERTS) & (lane < N_EXPERTS + N_GROUPS)
    gl = jnp.where(is_group, logits, NEG_INF)
    g_max = jnp.max(gl, axis=-1, keepdims=True)
    g_idx = jnp.min(jnp.where(gl == g_max, lane, far), axis=-1, keepdims=True) - N_EXPERTS
    g_w = 1.0 / jnp.sum(jnp.where(is_group, jnp.exp(gl - g_max), 0.0), axis=-1, keepdims=True)
    in_group = (lane < N_EXPERTS) & (jnp.floor(lane * (1.0 / EXPERTS_PER_GROUP)) == g_idx)
    el = jnp.where(in_group, logits, NEG_INF)
    e1 = jnp.max(el, axis=-1, keepdims=True)
    i1 = jnp.min(jnp.where(el == e1, lane, far), axis=-1, keepdims=True)
    el2 = jnp.where(lane == i1, NEG_INF, el)
    e2 = jnp.max(el2, axis=-1, keepdims=True)
    i2 = jnp.min(jnp.where(el2 == e2, lane, far), axis=-1, keepdims=True)
    t = jnp.exp(e2 - e1)
    w1 = g_w / (1.0 + t)
    w2 = w1 * t
    if sparse_out:
        e_lane = lane + g_idx * EXPERTS_PER_GROUP
        rec = jnp.where(e_lane == i1, w1, 0.0) + jnp.where(e_lane == i2, w2, 0.0)
        rec = jnp.where(lane < EXPERTS_PER_GROUP, rec, 0.0)
        comb_ref[...] = jnp.where(lane == GID_LANE, g_idx, rec)
    else:
        comb_ref[...] = jnp.where(lane == i1, w1, 0.0) + jnp.where(lane == i2, w2, 0.0)


def _const_spec(shape):
    zeros = (0,) * len(shape)
    return pl.BlockSpec(shape, lambda i: zeros, pipeline_mode=pl.Buffered(1))


def _mixer(x, conv0, h0, k0, v0, weights, *, nseg, seg_len, mask_history, sparse_out):
    n_tok = x.shape[0]
    m_rows = nseg * seg_len
    n_steps = n_tok // m_rows
    keep = min(seg_len, HIST)
    row_spec = lambda w: pl.BlockSpec((m_rows, w), lambda i: (i, 0))
    in_specs = [row_spec(D_MODEL), _const_spec(conv0.shape), _const_spec(h0.shape), _const_spec(k0.shape),
                _const_spec(v0.shape)]
    for w in weights:
        if w.ndim == 1:
            in_specs.append(pl.BlockSpec(memory_space=pltpu.SMEM))
        else:
            in_specs.append(_const_spec(w.shape))
    state_spec = lambda r, w: pl.BlockSpec((nseg, r, w), lambda i: (0, 0, 0))
    out_shape = (
        jax.ShapeDtypeStruct((n_tok, D_MODEL), jnp.float32),
        (jax.ShapeDtypeStruct((n_tok, PACK_W), jnp.uint32) if sparse_out
         else jax.ShapeDtypeStruct((n_tok, D_MODEL), jnp.bfloat16)),
        jax.ShapeDtypeStruct((n_tok, ROUTER_W), jnp.float32),
        jax.ShapeDtypeStruct((nseg, SUBLANES, D_RNN), jnp.float32),
        jax.ShapeDtypeStruct((nseg, SUBLANES, D_RNN), jnp.float32),
        jax.ShapeDtypeStruct((nseg, keep, KV2_W), jnp.float32),
        jax.ShapeDtypeStruct((nseg, keep, KV2_W), jnp.float32),
    )
    out_specs = (row_spec(D_MODEL), row_spec(PACK_W if sparse_out else D_MODEL), row_spec(ROUTER_W),
                 state_spec(SUBLANES, D_RNN), state_spec(SUBLANES, D_RNN),
                 state_spec(keep, KV2_W), state_spec(keep, KV2_W))
    scratch = [
        pltpu.VMEM((nseg, SUBLANES, D_RNN), jnp.float32),
        pltpu.VMEM((nseg, SUBLANES, D_RNN), jnp.float32),
        pltpu.VMEM((nseg, HIST + seg_len, KV2_W), jnp.bfloat16),
        pltpu.VMEM((nseg, HIST + seg_len, N_KV * V_STRIDE), jnp.bfloat16),
        pltpu.VMEM((D_RNN // LANES, m_rows, LANES), jnp.float32),
        pltpu.VMEM((D_RNN // LANES, m_rows, LANES), jnp.float32),
        pltpu.VMEM((m_rows, Q_W), jnp.bfloat16),
    ]
    return pl.pallas_call(
        functools.partial(_mixer_kernel, nseg, seg_len, mask_history, sparse_out),
        grid=(n_steps,),
        in_specs=in_specs,
        out_specs=out_specs,
        out_shape=out_shape,
        scratch_shapes=scratch,
        compiler_params=pltpu.CompilerParams(dimension_semantics=("arbitrary",), vmem_limit_bytes=VMEM_LIMIT),
        name="mixer_prompt" if mask_history else "mixer_sample",
    )(x, conv0, h0, k0, v0, *weights)


def _group_swiglu(x, cols, wg_ref, wu_ref, wd_ref):
    parts = []
    for e in range(EXPERTS_PER_GROUP):
        h = jax.nn.silu(_bdot(x, wg_ref[e])) * _bdot(x, wu_ref[e])
        parts.append((h * cols[e]).astype(jnp.bfloat16))
    return _bdot(jnp.concatenate(parts, axis=1), wd_ref[...])


def _moe_groups_kernel(xn2_ref, comb_ref, x1_ref, wg_ref, wu_ref, wd_ref, norm_ref, y_ref, acc_ref):
    g = pl.program_id(1)

    @pl.when(g == 0)
    def _():
        acc_ref[...] = jnp.zeros_like(acc_ref)

    comb = comb_ref[...]
    lane = lax.broadcasted_iota(jnp.int32, comb.shape, 1)
    cols = [jnp.sum(jnp.where(lane == g * EXPERTS_PER_GROUP + e, comb, 0.0), axis=-1, keepdims=True)
            for e in range(EXPERTS_PER_GROUP)]
    acc_ref[...] += _group_swiglu(xn2_ref[...], cols, wg_ref, wu_ref, wd_ref)

    @pl.when(g == N_GROUPS - 1)
    def _():
        y_ref[...] = _rmsnorm(x1_ref[...] + acc_ref[...], norm_ref[...])


def _group_weight_specs(group_of):
    return [pl.BlockSpec((EXPERTS_PER_GROUP, D_MODEL, D_EXPERT), lambda *a: (group_of(*a), 0, 0)),
            pl.BlockSpec((EXPERTS_PER_GROUP, D_MODEL, D_EXPERT), lambda *a: (group_of(*a), 0, 0)),
            pl.BlockSpec((None, GROUP_FF, D_MODEL), lambda *a: (group_of(*a), 0, 0))]


def _moe_groups(xn2, comb, x1, wg, wu, wd, norm_final, *, tile):
    n_tok = xn2.shape[0]
    row_spec = lambda w: pl.BlockSpec((tile, w), lambda i, g: (i, 0))
    return pl.pallas_call(
        _moe_groups_kernel,
        grid=(n_tok // tile, N_GROUPS),
        in_specs=[row_spec(D_MODEL), row_spec(ROUTER_W), row_spec(D_MODEL),
                  *_group_weight_specs(lambda i, g: g),
                  pl.BlockSpec((1, D_MODEL), lambda i, g: (0, 0))],
        out_specs=row_spec(D_MODEL),
        out_shape=jax.ShapeDtypeStruct((n_tok, D_MODEL), jnp.float32),
        scratch_shapes=[pltpu.VMEM((tile, D_MODEL), jnp.float32)],
        compiler_params=pltpu.CompilerParams(dimension_semantics=("arbitrary", "arbitrary"),
                                             vmem_limit_bytes=VMEM_LIMIT),
        name="moe_groups",
    )(xn2, comb, x1, wg, wu, wd, norm_final)


def _scatter_rows_kernel(tile, pos_ref, x_ref, rec_ref, xs_ref, recs_ref):
    i = pl.program_id(0)

    @pl.when(i == 0)
    def _():
        xs_ref[...] = jnp.zeros_like(xs_ref)
        recs_ref[...] = jnp.zeros_like(recs_ref)

    def body(j, carry):
        r0 = pl.multiple_of(j * SUBLANES, SUBLANES)
        xb = x_ref[pl.ds(r0, SUBLANES), :]
        rb = rec_ref[pl.ds(r0, SUBLANES), :]
        for k in range(SUBLANES):
            p = pos_ref[i * tile + j * SUBLANES + k]
            xs_ref[pl.ds(p, 1), :] = xb[k:k + 1, :]
            recs_ref[pl.ds(p, 1), :] = rb[k:k + 1, :]
        return carry

    lax.fori_loop(0, tile // SUBLANES, body, 0)


def _scatter_rows(pos, xn2p, rec, n_slots, *, tile):
    n_tok = xn2p.shape[0]
    return pl.pallas_call(
        functools.partial(_scatter_rows_kernel, tile),
        grid_spec=pltpu.PrefetchScalarGridSpec(
            num_scalar_prefetch=1,
            grid=(n_tok // tile,),
            in_specs=[pl.BlockSpec((tile, PACK_W), lambda i, pos: (i, 0)),
                      pl.BlockSpec((tile, ROUTER_W), lambda i, pos: (i, 0))],
            out_specs=[pl.BlockSpec((n_slots, PACK_W), lambda i, pos: (0, 0)),
                       pl.BlockSpec((n_slots, ROUTER_W), lambda i, pos: (0, 0))],
        ),
        out_shape=(jax.ShapeDtypeStruct((n_slots, PACK_W), jnp.uint32),
                   jax.ShapeDtypeStruct((n_slots, ROUTER_W), jnp.float32)),
        compiler_params=pltpu.CompilerParams(dimension_semantics=("arbitrary",), vmem_limit_bytes=VMEM_LIMIT),
        name="moe_scatter",
    )(pos, xn2p, rec)


def _moe_sorted_kernel(tg_ref, xs_ref, recs_ref, wg_ref, wu_ref, wd_ref, o_ref):
    x = _unpack_bf16_pairs(xs_ref[...]).astype(jnp.bfloat16)
    rec = recs_ref[...]
    cols = [rec[:, e:e + 1] for e in range(EXPERTS_PER_GROUP)]
    o_ref[...] = _pack_bf16_pairs(_group_swiglu(x, cols, wg_ref, wu_ref, wd_ref))


def _moe_sorted(tile_group, xs, recs, wg, wu, wd, *, tile):
    n_slots = xs.shape[0]
    return pl.pallas_call(
        _moe_sorted_kernel,
        grid_spec=pltpu.PrefetchScalarGridSpec(
            num_scalar_prefetch=1,
            grid=(n_slots // tile,),
            in_specs=[pl.BlockSpec((tile, PACK_W), lambda i, tg: (i, 0)),
                      pl.BlockSpec((tile, ROUTER_W), lambda i, tg: (i, 0)),
                      *_group_weight_specs(lambda i, tg: tg[i])],
            out_specs=pl.BlockSpec((tile, PACK_W), lambda i, tg: (i, 0)),
        ),
        out_shape=jax.ShapeDtypeStruct((n_slots, PACK_W), jnp.uint32),
        compiler_params=pltpu.CompilerParams(dimension_semantics=("arbitrary",), vmem_limit_bytes=VMEM_LIMIT),
        name="moe_sorted",
    )(tile_group, xs, recs, wg, wu, wd)


def _gather_norm_kernel(tile, pos_ref, os_ref, x1_ref, norm_ref, y_ref, buf_ref):
    i = pl.program_id(0)

    def body(j, carry):
        r0 = pl.multiple_of(j * SUBLANES, SUBLANES)
        rows = [os_ref[pl.ds(pos_ref[i * tile + j * SUBLANES + k], 1), :] for k in range(SUBLANES)]
        buf_ref[pl.ds(r0, SUBLANES), :] = jnp.concatenate(rows, axis=0)
        return carry

    lax.fori_loop(0, tile // SUBLANES, body, 0)
    y_ref[...] = _rmsnorm(x1_ref[...] + _unpack_bf16_pairs(buf_ref[...]), norm_ref[...])


def _gather_norm(pos, o_sorted, x1, norm_final, *, tile):
    n_tok = x1.shape[0]
    n_slots = o_sorted.shape[0]
    return pl.pallas_call(
        functools.partial(_gather_norm_kernel, tile),
        grid_spec=pltpu.PrefetchScalarGridSpec(
            num_scalar_prefetch=1,
            grid=(n_tok // tile,),
            in_specs=[pl.BlockSpec((n_slots, PACK_W), lambda i, pos: (0, 0), pipeline_mode=pl.Buffered(1)),
                      pl.BlockSpec((tile, D_MODEL), lambda i, pos: (i, 0)),
                      pl.BlockSpec((1, D_MODEL), lambda i, pos: (0, 0))],
            out_specs=pl.BlockSpec((tile, D_MODEL), lambda i, pos: (i, 0)),
            scratch_shapes=[pltpu.VMEM((tile, PACK_W), jnp.uint32)],
        ),
        out_shape=jax.ShapeDtypeStruct((n_tok, D_MODEL), jnp.float32),
        compiler_params=pltpu.CompilerParams(dimension_semantics=("arbitrary",), vmem_limit_bytes=VMEM_LIMIT),
        name="moe_gather_norm",
    )(pos, o_sorted, x1, norm_final)


def _sorted_slots(rec, tile):
    n_tok = rec.shape[0]
    gid = rec[:, GID_LANE].astype(jnp.int32)
    onehot = (gid[:, None] == jnp.arange(N_GROUPS, dtype=jnp.int32)[None, :]).astype(jnp.int32)
    csum = jnp.cumsum(onehot, axis=0)
    rank = jnp.sum((csum - onehot) * onehot, axis=1)
    padded = ((csum[-1] + tile - 1) // tile) * tile
    end = jnp.cumsum(padded)
    pos = jnp.sum(onehot * (end - padded)[None, :], axis=1) + rank
    n_tiles = n_tok // tile + N_GROUPS
    tile_start = jnp.arange(n_tiles, dtype=jnp.int32) * tile
    tile_group = jnp.minimum(jnp.sum((tile_start[:, None] >= end[None, :]).astype(jnp.int32), axis=1), N_GROUPS - 1)
    return pos.astype(jnp.int32), tile_group.astype(jnp.int32), n_tiles * tile


def _rel_bucket(rel):
    nb = REL_BUCKETS // 2
    n = -rel
    ret = jnp.where(n < 0, nb, 0)
    n = jnp.abs(n)
    max_exact = nb // 2
    nf = jnp.maximum(n, 1).astype(jnp.float32)
    large = max_exact + (jnp.log(nf / max_exact) / math.log(REL_MAX_DIST / max_exact) * (nb - max_exact)).astype(jnp.int32)
    large = jnp.minimum(large, nb - 1)
    return ret + jnp.where(n < max_exact, n, large)


def _band_bias(rel_table):
    lo = -(HIST + CHUNK - 1)
    rels = jnp.arange(lo, CHUNK, dtype=jnp.int32)
    by_rel = jnp.transpose(rel_table[_rel_bucket(rels)]).astype(jnp.float32)
    rows = [by_rel[:, (CHUNK - 1 - qi):(CHUNK - 1 - qi) + BAND] for qi in range(CHUNK)]
    return jnp.stack(rows, axis=1).reshape(N_KV, GROUP * CHUNK, BAND)


def _dup_heads(a):
    a = a.reshape(a.shape[:-1] + (N_KV, HEAD_DIM))
    return jnp.concatenate([a, a], axis=-1).reshape(a.shape[:-2] + (KV2_W,))


def _undup_heads(a):
    a = a.reshape(a.shape[:-1] + (N_KV, LANES))
    return a[..., :HEAD_DIM]


def _block_diag_group(w, j):
    per = RG_GROUP // LRU_BLOCK_W
    blk = w[j * per:(j + 1) * per]
    eye = jnp.eye(per, dtype=w.dtype)
    return jnp.einsum('nde,nm->ndme', blk, eye).reshape(RG_GROUP, RG_GROUP)


def kernel(x_prompt, x_sample, state_lru_h, state_lru_conv, cache_swa_k, cache_swa_v, norm_mix, w_in, b_merge, conv_w, conv_b, w_rg_a, b_rg_a, w_rg_x, b_rg_x, lru_lambda, attn_sink, rel_bias, w_lru_proj, w_attn_proj, w_out, norm_ffn, w_group, b_group, w_router, b_router, w_e_gate, w_e_up, w_e_down, norm_final):
    f32, bf16 = jnp.float32, jnp.bfloat16
    l = 0
    w = w_in[l]
    w_kv2 = jnp.concatenate([_dup_heads(w[:, C_K:C_K + KV_W]), _dup_heads(w[:, C_V:C_V + KV_W])], axis=1).astype(bf16)
    w_rg = jnp.stack([
        jnp.concatenate([_block_diag_group(w_rg_a[l], j), _block_diag_group(w_rg_x[l], j)], axis=1)
        for j in range(N_RG_GROUPS)]).astype(bf16)
    w_rt = jnp.concatenate([w_router[l], w_group[l],
                            jnp.zeros((D_MODEL, ROUTER_W - N_EXPERTS - N_GROUPS), f32)], axis=1).astype(bf16)
    b_rt = jnp.concatenate([b_router[l], b_group[l], jnp.zeros((ROUTER_W - N_EXPERTS - N_GROUPS,), f32)])[None, :]
    weights = (
        norm_mix[l][None, :], w.astype(bf16), w_kv2, b_merge[l][None, :], conv_w[l], conv_b[l][None, :], w_rg,
        b_rg_a[l][None, :], b_rg_x[l][None, :], lru_lambda[l][None, :], attn_sink[l], _band_bias(rel_bias) * LOG2E,
        w_lru_proj[l].astype(bf16), w_attn_proj[l].astype(bf16), w_out[l].astype(bf16), norm_ffn[l][None, :],
        w_rt, b_rt,
    )
    wg, wu = w_e_gate[l].astype(bf16), w_e_up[l].astype(bf16)
    wd = w_e_down[l].astype(bf16).reshape(N_GROUPS, GROUP_FF, D_MODEL)
    nf = norm_final[None, :]

    bp, tp, _ = x_prompt.shape
    assert bp == 1
    xp = x_prompt.reshape(tp, D_MODEL)
    zeros = lambda *s: jnp.zeros(s, f32)
    p_x1, p_xn2, p_comb, p_conv, p_h, p_k, p_v = _mixer(
        xp, zeros(1, SUBLANES, D_RNN), zeros(1, SUBLANES, D_RNN),
        jnp.zeros((1, HIST, KV2_W), bf16), jnp.zeros((1, HIST, KV2_W), bf16), weights,
        nseg=1, seg_len=PROMPT_TILE, mask_history=True, sparse_out=True)
    pos, tile_group, n_slots = _sorted_slots(p_comb, SORT_TILE)
    x_sorted, rec_sorted = _scatter_rows(pos, p_xn2, p_comb, n_slots, tile=SORT_TILE)
    o_sorted = _moe_sorted(tile_group, x_sorted, rec_sorted, wg, wu, wd, tile=SORT_TILE)
    y_p = _gather_norm(pos, o_sorted, p_x1, nf, tile=SORT_TILE).reshape(x_prompt.shape)

    bs, ts, _ = x_sample.shape
    xs = x_sample.reshape(bs * ts, D_MODEL)
    conv0 = jnp.pad(state_lru_conv[l], ((0, 0), (SUBLANES - (CONV_W - 1), 0), (0, 0)))
    h0 = jnp.broadcast_to(state_lru_h[l][:, None, :], (bs, SUBLANES, D_RNN))
    ck = cache_swa_k[l].reshape(bs, -1, KV_W)
    cv = cache_swa_v[l].reshape(bs, -1, KV_W)
    s_x1, s_xn2, s_comb, s_conv, s_h, s_k, s_v = _mixer(
        xs, conv0, h0, _dup_heads(ck).astype(bf16), _dup_heads(cv).astype(bf16), weights,
        nseg=bs, seg_len=ts, mask_history=False, sparse_out=False)
    y_s = _moe_groups(s_xn2, s_comb, s_x1, wg, wu, wd, nf, tile=bs * ts).reshape(x_sample.shape)

    tail = CONV_W - 1
    p_lru_h = p_h[:, 0, :][None]
    p_lru_conv = p_conv[:, SUBLANES - tail:, :][None]
    p_swa_k = _undup_heads(p_k)[None]
    p_swa_v = _undup_heads(p_v)[None]
    s_lru_h = s_h[:, 0, :][None]
    s_lru_conv = s_conv[:, SUBLANES - tail:, :][None]
    win = ck.shape[1]
    s_swa_k = jnp.concatenate([cache_swa_k[l][:, ts:], _undup_heads(s_k)], axis=1)[:, -win:][None]
    s_swa_v = jnp.concatenate([cache_swa_v[l][:, ts:], _undup_heads(s_v)], axis=1)[:, -win:][None]
    return (y_p, y_s, p_lru_h, p_lru_conv, p_swa_k, p_swa_v, s_lru_h, s_lru_conv, s_swa_k, s_swa_v)
```

```python
import functools
import math

import jax
import jax.numpy as jnp
import numpy as np
from jax import lax
from jax.experimental import pallas as pl
from jax.experimental.pallas import tpu as pltpu

D_MODEL = 1024
CHUNK = 64
D_RNN = D_MODEL
CONV_W = 4
LRU_BLOCKS = 16
LRU_BLOCK_W = D_RNN // LRU_BLOCKS
LRU_C = 8.0
N_HEADS = 16
N_KV = 4
HEAD_DIM = 64
GROUP = N_HEADS // N_KV
WINDOW = 128
LOOKBACK_CHUNKS = -(-WINDOW // CHUNK)
HIST = LOOKBACK_CHUNKS * CHUNK
BAND = HIST + CHUNK
Q_W = N_HEADS * HEAD_DIM
KV_W = N_KV * HEAD_DIM
REL_BUCKETS = 32
REL_MAX_DIST = 128
N_GROUPS = 4
EXPERTS_PER_GROUP = 8
N_EXPERTS = N_GROUPS * EXPERTS_PER_GROUP
D_EXPERT = 256
EPS = 1e-6
NEG_INF = -1e30

LANES = 128
SUBLANES = 8
KV2_W = N_KV * LANES
RG_GROUP = 256
N_RG_GROUPS = D_RNN // RG_GROUP
ROUTER_W = LANES
VMEM_LIMIT = 56 * 1024 * 1024
PACK_W = D_MODEL // 2
GROUP_FF = EXPERTS_PER_GROUP * D_EXPERT
GID_LANE = EXPERTS_PER_GROUP
CLS_LANE = GID_LANE + 1
PAIRS = [(lo, hi) for lo in range(EXPERTS_PER_GROUP) for hi in range(lo + 1, EXPERTS_PER_GROUP)]
CLS_PER_GROUP = 32
N_CLS = N_GROUPS * CLS_PER_GROUP
CLS_EXPERTS = np.zeros((N_CLS, EXPERTS_PER_GROUP), np.float32)
for _g in range(N_GROUPS):
    for _p, (_lo, _hi) in enumerate(PAIRS):
        CLS_EXPERTS[_g * CLS_PER_GROUP + _p, [_lo, _hi]] = 1.0
RANK_BLOCK = 128
SORT_TILE = 512
MOE_TILE = 256
PROMPT_TILE = 512

C_LX, C_LG, C_Q = 0, D_RNN, 2 * D_RNN
C_K = C_Q + Q_W
C_V = C_K + KV_W
C_G = C_V + KV_W
V_STRIDE = 2 * LANES
SCAN_ROWS = SUBLANES * SUBLANES
LOG2E = math.log2(math.e)


def _rmsnorm(x, g):
    return x * lax.rsqrt(jnp.mean(x * x, axis=-1, keepdims=True) + EPS) * g


def _bdot(a, b):
    return jnp.dot(a, b, preferred_element_type=jnp.float32)


def _sigmoid(x):
    return 0.5 * jnp.tanh(0.5 * x) + 0.5


def _pack_bf16_pairs(x):
    return pltpu.pack_elementwise([x[:, :PACK_W], x[:, PACK_W:]], packed_dtype=jnp.bfloat16)


def _unpack_bf16_pairs(p):
    lo = pltpu.unpack_elementwise(p, index=0, packed_dtype=jnp.bfloat16, unpacked_dtype=jnp.float32)
    hi = pltpu.unpack_elementwise(p, index=1, packed_dtype=jnp.bfloat16, unpacked_dtype=jnp.float32)
    return jnp.concatenate([lo, hi], axis=1)


def _mixer_kernel(nseg, seg_len, mask_history, sparse_out,
                  x_ref, conv0_ref, h0_ref, k0_ref, v0_ref,
                  norm_mix_ref, w_in_ref, w_kv2_ref, b_merge_ref, conv_w_ref, conv_b_ref, w_rg_ref, b_rg_a_ref, b_rg_x_ref,
                  lam_ref, sink_ref, bias_ref, w_lru_ref, w_attn_ref, w_out_ref, norm_ffn_ref, w_rt_ref, b_rt_ref,
                  x1_ref, xn2_ref, comb_ref, conv_out_ref, h_out_ref, k_out_ref, v_out_ref,
                  xp_ref, hc_ref, kbuf_ref, vbuf_ref, a_ref, b_ref, attn_ref):
    step = pl.program_id(0)
    m_rows = nseg * seg_len
    n_chunks = seg_len // CHUNK
    keep = min(seg_len, HIST)

    @pl.when(step == 0)
    def _():
        xp_ref[...] = conv0_ref[...]
        hc_ref[...] = h0_ref[...]
        kbuf_ref[:, 0:HIST, :] = k0_ref[...]
        vbuf_ref[:, :, :] = jnp.ones(vbuf_ref.shape, jnp.bfloat16)
        for kv in range(N_KV):
            vbuf_ref[:, 0:HIST, kv * V_STRIDE:kv * V_STRIDE + LANES] = v0_ref[:, :, kv * LANES:(kv + 1) * LANES]

    x = x_ref[...]
    xn = _rmsnorm(x, norm_mix_ref[...]).astype(jnp.bfloat16)

    lru_x = _bdot(xn, w_in_ref[:, C_LX:C_LX + D_RNN])
    conv_w = conv_w_ref[...]
    row = lax.broadcasted_iota(jnp.int32, (SUBLANES, D_RNN), 0)
    xc_parts = []
    for s in range(nseg):
        xs = lru_x[s * seg_len:(s + 1) * seg_len, :]
        prev = xp_ref[s]
        tail = xs[seg_len - SUBLANES:, :]
        acc = conv_b_ref[...] + xs * conv_w[CONV_W - 1:CONV_W, :]
        for k in range(1, CONV_W):
            shifted = pltpu.roll(xs, k, axis=0)
            first = jnp.where(row < k, pltpu.roll(prev, k, axis=0), shifted[0:SUBLANES])
            shifted = jnp.concatenate([first, shifted[SUBLANES:]], axis=0)
            acc = acc + shifted * conv_w[CONV_W - 1 - k:CONV_W - k, :]
        xc_parts.append(acc)
        conv_out_ref[s] = tail
        xp_ref[s] = tail
    xc = xc_parts[0] if nseg == 1 else jnp.concatenate(xc_parts, axis=0)
    xc_b = xc.astype(jnp.bfloat16)

    lam = lam_ref[...]
    log_sig = jnp.minimum(lam, 0.0) - jnp.log1p(jnp.exp(-jnp.abs(lam)))
    c8 = LRU_C * log_sig
    for j in range(N_RG_GROUPS):
        cs = slice(j * RG_GROUP, (j + 1) * RG_GROUP)
        pre = _bdot(xc_b[:, cs], w_rg_ref[j])
        r = _sigmoid(pre[:, :RG_GROUP] + b_rg_a_ref[:, cs])
        i = _sigmoid(pre[:, RG_GROUP:] + b_rg_x_ref[:, cs])
        log_a = c8[:, cs] * r
        a_val = jnp.exp(log_a)
        th = jnp.tanh(log_a)
        b_val = jnp.sqrt(-2.0 * th / (1.0 - th)) * i * xc[:, cs]
        for t in range(RG_GROUP // LANES):
            lt = j * (RG_GROUP // LANES) + t
            a_ref[lt] = a_val[:, t * LANES:(t + 1) * LANES]
            b_ref[lt] = b_val[:, t * LANES:(t + 1) * LANES]

    row_c = lax.broadcasted_iota(jnp.int32, (SUBLANES, LANES), 0)
    for s in range(nseg):
        for lt in range(D_RNN // LANES):
            cs = slice(lt * LANES, (lt + 1) * LANES)
            carry = hc_ref[s, :, cs]
            for blk in range(seg_len // SCAN_ROWS):
                r0 = s * seg_len + blk * SCAN_ROWS
                slab = lambda ref, i: ref[lt, pl.ds(r0 + i, SUBLANES, stride=SUBLANES), :]
                a_loc, b_loc = [slab(a_ref, 0)], [slab(b_ref, 0)]
                for i in range(1, SUBLANES):
                    a_i = slab(a_ref, i)
                    b_loc.append(a_i * b_loc[-1] + slab(b_ref, i))
                    a_loc.append(a_i * a_loc[-1])
                a_e, b_e = a_loc[-1], b_loc[-1]
                for d in (1, 2, 4):
                    m = row_c >= d
                    b_e = jnp.where(m, a_e * pltpu.roll(b_e, d, axis=0) + b_e, b_e)
                    a_e = jnp.where(m, a_e * pltpu.roll(a_e, d, axis=0), a_e)
                h_end = a_e * carry + b_e
                c_in = jnp.where(row_c == 0, carry, pltpu.roll(h_end, 1, axis=0))
                for i in range(SUBLANES):
                    b_ref[lt, pl.ds(r0 + i, SUBLANES, stride=SUBLANES), :] = a_loc[i] * c_in + b_loc[i]
                carry = jnp.broadcast_to(h_end[SUBLANES - 1:SUBLANES, :], (SUBLANES, LANES))
            hc_ref[s, :, cs] = carry
            h_out_ref[s, :, cs] = carry

    lru_gate = _bdot(xn, w_in_ref[:, C_LG:C_LG + D_RNN])
    h_all = jnp.concatenate([b_ref[lt] for lt in range(D_RNN // LANES)], axis=1)
    lru_y = (h_all * jax.nn.gelu(lru_gate)).astype(jnp.bfloat16)
    mixed = _bdot(lru_y, w_lru_ref[...])
    gates = _bdot(xn, w_in_ref[:, C_G:C_G + D_MODEL]) + b_merge_ref[:, 0:D_MODEL]
    mixed = _sigmoid(gates) * mixed

    q = (_bdot(xn, w_in_ref[:, C_Q:C_Q + Q_W]) * (HEAD_DIM ** -0.5 * LOG2E)).astype(jnp.bfloat16)
    k2 = _bdot(xn, w_kv2_ref[:, 0:KV2_W])
    v2 = _bdot(xn, w_kv2_ref[:, KV2_W:2 * KV2_W])
    lane_q = lax.broadcasted_iota(jnp.int32, (CHUNK, LANES), 1)
    lane_o = lax.broadcasted_iota(jnp.int32, (CHUNK, LANES), 1)
    key_lane = lax.broadcasted_iota(jnp.int32, (1, BAND), 1)
    row_grp = lax.broadcasted_iota(jnp.int32, (GROUP * CHUNK, 1), 0) // CHUNK
    for s in range(nseg):
        rows = slice(s * seg_len, (s + 1) * seg_len)
        kbuf_ref[s, HIST:HIST + seg_len, :] = k2[rows].astype(jnp.bfloat16)
        for kv in range(N_KV):
            vbuf_ref[s, HIST:HIST + seg_len, kv * V_STRIDE:kv * V_STRIDE + LANES] = (
                v2[rows, kv * LANES:(kv + 1) * LANES].astype(jnp.bfloat16))
        k_out_ref[s] = k2[s * seg_len + seg_len - keep:(s + 1) * seg_len]
        v_out_ref[s] = v2[s * seg_len + seg_len - keep:(s + 1) * seg_len]
        for c in range(n_chunks):
            q_c = q[s * seg_len + c * CHUNK:s * seg_len + (c + 1) * CHUNK]
            slabs = []
            for kv in range(N_KV):
                parts = []
                for g in range(GROUP):
                    col = kv * GROUP * HEAD_DIM + (g // 2) * LANES
                    slab = q_c[:, col:col + LANES]
                    keep_lo = (g % 2) == 0
                    sel = (lane_q < HEAD_DIM) if keep_lo else (lane_q >= HEAD_DIM)
                    parts.append(jnp.where(sel, slab, jnp.zeros_like(slab)))
                q_stack = jnp.concatenate(parts, axis=0)
                k_band = kbuf_ref[s, c * CHUNK:c * CHUNK + BAND, kv * LANES:(kv + 1) * LANES]
                v_band = vbuf_ref[s, c * CHUNK:c * CHUNK + BAND, kv * V_STRIDE:(kv + 1) * V_STRIDE]
                sc = lax.dot_general(q_stack, k_band, (((1,), (1,)), ((), ())),
                                     preferred_element_type=jnp.float32)
                sc = sc + bias_ref[kv]
                if mask_history:
                    first_valid = HIST - (step * n_chunks + c) * CHUNK
                    sc = jnp.where(key_lane >= first_valid, sc, NEG_INF)
                sink = jnp.zeros((GROUP * CHUNK, 1), jnp.float32)
                for g in range(GROUP):
                    sink = jnp.where(row_grp == g, sink_ref[kv * GROUP + g] * LOG2E, sink)
                m = jnp.maximum(jnp.max(sc, axis=-1, keepdims=True), sink)
                p = jnp.exp2(sc - m).astype(jnp.bfloat16)
                o = _bdot(p, v_band)
                denom = o[:, LANES:2 * LANES] + jnp.exp2(sink - m)
                o = o[:, 0:LANES] / denom
                for pair in range(GROUP // 2):
                    lo = o[(2 * pair) * CHUNK:(2 * pair + 1) * CHUNK]
                    hi = o[(2 * pair + 1) * CHUNK:(2 * pair + 2) * CHUNK]
                    slabs.append(jnp.where(lane_o < HEAD_DIM, lo, hi))
            attn_ref[s * seg_len + c * CHUNK:s * seg_len + (c + 1) * CHUNK, :] = (
                jnp.concatenate(slabs, axis=1).astype(jnp.bfloat16))
        kbuf_ref[s, 0:HIST, :] = kbuf_ref[s, seg_len:seg_len + HIST, :]
        vbuf_ref[s, 0:HIST, :] = vbuf_ref[s, seg_len:seg_len + HIST, :]

    attn = _bdot(attn_ref[...], w_attn_ref[...])
    gates = _bdot(xn, w_in_ref[:, C_G + D_MODEL:C_G + 2 * D_MODEL]) + b_merge_ref[:, D_MODEL:2 * D_MODEL]
    mixed = (mixed + _sigmoid(gates) * attn).astype(jnp.bfloat16)
    x1 = x + _bdot(mixed, w_out_ref[...])
    x1_ref[...] = x1

    xn2 = _rmsnorm(x1, norm_ffn_ref[...])
    xn2_ref[...] = _pack_bf16_pairs(xn2) if sparse_out else xn2.astype(jnp.bfloat16)
    logits = _bdot(xn2.astype(jnp.bfloat16), w_rt_ref[...]) + b_rt_ref[...]
    lane = lax.broadcasted_iota(jnp.int32, (m_rows, ROUTER_W), 1).astype(jnp.float32)
    far = jnp.float32(2 * ROUTER_W)
    is_group = (lane >= N_EXPERTS) & (lane < N_EXPERTS + N_GROUPS)
    gl = jnp.where(is_group, logits, NEG_INF)
    g_max = jnp.max(gl, axis=-1, keepdims=True)
    g_idx = jnp.min(jnp.where(gl == g_max, lane, far), axis=-1, keepdims=True) - N_EXPERTS
    g_w = 1.0 / jnp.sum(jnp.where(is_group, jnp.exp(gl - g_max), 0.0), axis=-1, keepdims=True)
    in_group = (lane < N_EXPERTS) & (jnp.floor(lane * (1.0 / EXPERTS_PER_GROUP)) == g_idx)
    el = jnp.where(in_group, logits, NEG_INF)
    e1 = jnp.max(el, axis=-1, keepdims=True)
    i1 = jnp.min(jnp.where(el == e1, lane, far), axis=-1, keepdims=True)
    el2 = jnp.where(lane == i1, NEG_INF, el)
    e2 = jnp.max(el2, axis=-1, keepdims=True)
    i2 = jnp.min(jnp.where(el2 == e2, lane, far), axis=-1, keepdims=True)
    t = jnp.exp(e2 - e1)
    w1 = g_w / (1.0 + t)
    w2 = w1 * t
    if sparse_out:
        e_lane = lane + g_idx * EXPERTS_PER_GROUP
        rec = jnp.where(e_lane == i1, w1, 0.0) + jnp.where(e_lane == i2, w2, 0.0)
        rec = jnp.where(lane < EXPERTS_PER_GROUP, rec, 0.0)
        l1 = i1 - g_idx * EXPERTS_PER_GROUP
        l2 = i2 - g_idx * EXPERTS_PER_GROUP
        lo, hi = jnp.minimum(l1, l2), jnp.maximum(l1, l2)
        cls = g_idx * CLS_PER_GROUP + lo * (2 * EXPERTS_PER_GROUP - 1 - lo) * 0.5 + (hi - lo - 1.0)
        rec = jnp.where(lane == GID_LANE, g_idx, rec)
        comb_ref[...] = jnp.where(lane == CLS_LANE, cls, rec)
    else:
        comb_ref[...] = jnp.where(lane == i1, w1, 0.0) + jnp.where(lane == i2, w2, 0.0)


def _const_spec(shape):
    zeros = (0,) * len(shape)
    return pl.BlockSpec(shape, lambda i: zeros, pipeline_mode=pl.Buffered(1))


def _mixer(x, conv0, h0, k0, v0, weights, *, nseg, seg_len, mask_history, sparse_out):
    n_tok = x.shape[0]
    m_rows = nseg * seg_len
    n_steps = n_tok // m_rows
    keep = min(seg_len, HIST)
    row_spec = lambda w: pl.BlockSpec((m_rows, w), lambda i: (i, 0))
    in_specs = [row_spec(D_MODEL), _const_spec(conv0.shape), _const_spec(h0.shape), _const_spec(k0.shape),
                _const_spec(v0.shape)]
    for w in weights:
        if w.ndim == 1:
            in_specs.append(pl.BlockSpec(memory_space=pltpu.SMEM))
        else:
            in_specs.append(_const_spec(w.shape))
    state_spec = lambda r, w: pl.BlockSpec((nseg, r, w), lambda i: (0, 0, 0))
    out_shape = (
        jax.ShapeDtypeStruct((n_tok, D_MODEL), jnp.float32),
        (jax.ShapeDtypeStruct((n_tok, PACK_W), jnp.uint32) if sparse_out
         else jax.ShapeDtypeStruct((n_tok, D_MODEL), jnp.bfloat16)),
        jax.ShapeDtypeStruct((n_tok, ROUTER_W), jnp.float32),
        jax.ShapeDtypeStruct((nseg, SUBLANES, D_RNN), jnp.float32),
        jax.ShapeDtypeStruct((nseg, SUBLANES, D_RNN), jnp.float32),
        jax.ShapeDtypeStruct((nseg, keep, KV2_W), jnp.float32),
        jax.ShapeDtypeStruct((nseg, keep, KV2_W), jnp.float32),
    )
    out_specs = (row_spec(D_MODEL), row_spec(PACK_W if sparse_out else D_MODEL), row_spec(ROUTER_W),
                 state_spec(SUBLANES, D_RNN), state_spec(SUBLANES, D_RNN),
                 state_spec(keep, KV2_W), state_spec(keep, KV2_W))
    scratch = [
        pltpu.VMEM((nseg, SUBLANES, D_RNN), jnp.float32),
        pltpu.VMEM((nseg, SUBLANES, D_RNN), jnp.float32),
        pltpu.VMEM((nseg, HIST + seg_len, KV2_W), jnp.bfloat16),
        pltpu.VMEM((nseg, HIST + seg_len, N_KV * V_STRIDE), jnp.bfloat16),
        pltpu.VMEM((D_RNN // LANES, m_rows, LANES), jnp.float32),
        pltpu.VMEM((D_RNN // LANES, m_rows, LANES), jnp.float32),
        pltpu.VMEM((m_rows, Q_W), jnp.bfloat16),
    ]
    return pl.pallas_call(
        functools.partial(_mixer_kernel, nseg, seg_len, mask_history, sparse_out),
        grid=(n_steps,),
        in_specs=in_specs,
        out_specs=out_specs,
        out_shape=out_shape,
        scratch_shapes=scratch,
        compiler_params=pltpu.CompilerParams(dimension_semantics=("arbitrary",), vmem_limit_bytes=VMEM_LIMIT),
        name="mixer_prompt" if mask_history else "mixer_sample",
    )(x, conv0, h0, k0, v0, *weights)


def _group_swiglu(x, cols, wg_ref, wu_ref, wd_ref):
    parts = []
    for e in range(EXPERTS_PER_GROUP):
        h = jax.nn.silu(_bdot(x, wg_ref[e])) * _bdot(x, wu_ref[e])
        parts.append((h * cols[e]).astype(jnp.bfloat16))
    return _bdot(jnp.concatenate(parts, axis=1), wd_ref[...])


def _moe_groups_kernel(xn2_ref, comb_ref, x1_ref, wg_ref, wu_ref, wd_ref, norm_ref, y_ref, acc_ref):
    g = pl.program_id(1)

    @pl.when(g == 0)
    def _():
        acc_ref[...] = jnp.zeros_like(acc_ref)

    comb = comb_ref[...]
    lane = lax.broadcasted_iota(jnp.int32, comb.shape, 1)
    cols = [jnp.sum(jnp.where(lane == g * EXPERTS_PER_GROUP + e, comb, 0.0), axis=-1, keepdims=True)
            for e in range(EXPERTS_PER_GROUP)]
    acc_ref[...] += _group_swiglu(xn2_ref[...], cols, wg_ref, wu_ref, wd_ref)

    @pl.when(g == N_GROUPS - 1)
    def _():
        y_ref[...] = _rmsnorm(x1_ref[...] + acc_ref[...], norm_ref[...])


def _group_weight_specs(group_of):
    return [pl.BlockSpec((EXPERTS_PER_GROUP, D_MODEL, D_EXPERT), lambda *a: (group_of(*a), 0, 0)),
            pl.BlockSpec((EXPERTS_PER_GROUP, D_MODEL, D_EXPERT), lambda *a: (group_of(*a), 0, 0)),
            pl.BlockSpec((None, GROUP_FF, D_MODEL), lambda *a: (group_of(*a), 0, 0))]


def _moe_groups(xn2, comb, x1, wg, wu, wd, norm_final, *, tile):
    n_tok = xn2.shape[0]
    row_spec = lambda w: pl.BlockSpec((tile, w), lambda i, g: (i, 0))
    return pl.pallas_call(
        _moe_groups_kernel,
        grid=(n_tok // tile, N_GROUPS),
        in_specs=[row_spec(D_MODEL), row_spec(ROUTER_W), row_spec(D_MODEL),
                  *_group_weight_specs(lambda i, g: g),
                  pl.BlockSpec((1, D_MODEL), lambda i, g: (0, 0))],
        out_specs=row_spec(D_MODEL),
        out_shape=jax.ShapeDtypeStruct((n_tok, D_MODEL), jnp.float32),
        scratch_shapes=[pltpu.VMEM((tile, D_MODEL), jnp.float32)],
        compiler_params=pltpu.CompilerParams(dimension_semantics=("arbitrary", "arbitrary"),
                                             vmem_limit_bytes=VMEM_LIMIT),
        name="moe_groups",
    )(xn2, comb, x1, wg, wu, wd, norm_final)


def _scatter_rows_kernel(tile, pos_ref, x_ref, rec_ref, xs_ref, recs_ref):
    i = pl.program_id(0)

    @pl.when(i == 0)
    def _():
        xs_ref[...] = jnp.zeros_like(xs_ref)
        recs_ref[...] = jnp.zeros_like(recs_ref)

    def body(j, carry):
        r0 = pl.multiple_of(j * SUBLANES, SUBLANES)
        xb = x_ref[pl.ds(r0, SUBLANES), :]
        rb = rec_ref[pl.ds(r0, SUBLANES), :]
        for k in range(SUBLANES):
            p = pos_ref[i * tile + j * SUBLANES + k]
            xs_ref[pl.ds(p, 1), :] = xb[k:k + 1, :]
            recs_ref[pl.ds(p, 1), :] = rb[k:k + 1, :]
        return carry

    lax.fori_loop(0, tile // SUBLANES, body, 0)


def _scatter_rows(pos, xn2p, rec, n_slots, *, tile):
    n_tok = xn2p.shape[0]
    return pl.pallas_call(
        functools.partial(_scatter_rows_kernel, tile),
        grid_spec=pltpu.PrefetchScalarGridSpec(
            num_scalar_prefetch=1,
            grid=(n_tok // tile,),
            in_specs=[pl.BlockSpec((tile, PACK_W), lambda i, pos: (i, 0)),
                      pl.BlockSpec((tile, ROUTER_W), lambda i, pos: (i, 0))],
            out_specs=[pl.BlockSpec((n_slots, PACK_W), lambda i, pos: (0, 0)),
                       pl.BlockSpec((n_slots, ROUTER_W), lambda i, pos: (0, 0))],
        ),
        out_shape=(jax.ShapeDtypeStruct((n_slots, PACK_W), jnp.uint32),
                   jax.ShapeDtypeStruct((n_slots, ROUTER_W), jnp.float32)),
        compiler_params=pltpu.CompilerParams(dimension_semantics=("arbitrary",), vmem_limit_bytes=VMEM_LIMIT),
        name="moe_scatter",
    )(pos, xn2p, rec)


def _moe_sorted_kernel(tg_ref, used_ref, xs_ref, recs_ref, wg_ref, wu_ref, wd_ref, o_ref, acc_ref):
    i = pl.program_id(0)
    acc_ref[...] = jnp.zeros_like(acc_ref)
    for e in range(EXPERTS_PER_GROUP):
        @pl.when(used_ref[i * EXPERTS_PER_GROUP + e] != 0)
        def _(e=e):
            x = _unpack_bf16_pairs(xs_ref[...]).astype(jnp.bfloat16)
            h = jax.nn.silu(_bdot(x, wg_ref[e])) * _bdot(x, wu_ref[e])
            h = (h * recs_ref[:, e:e + 1]).astype(jnp.bfloat16)
            acc_ref[...] += _bdot(h, wd_ref[e * D_EXPERT:(e + 1) * D_EXPERT, :])
    o_ref[...] = _pack_bf16_pairs(acc_ref[...])


def _moe_sorted(tile_group, tile_used, xs, recs, wg, wu, wd, *, tile):
    n_slots = xs.shape[0]
    return pl.pallas_call(
        _moe_sorted_kernel,
        grid_spec=pltpu.PrefetchScalarGridSpec(
            num_scalar_prefetch=2,
            grid=(n_slots // tile,),
            in_specs=[pl.BlockSpec((tile, PACK_W), lambda i, tg, tu: (i, 0)),
                      pl.BlockSpec((tile, ROUTER_W), lambda i, tg, tu: (i, 0)),
                      *_group_weight_specs(lambda i, tg, tu: tg[i])],
            out_specs=pl.BlockSpec((tile, PACK_W), lambda i, tg, tu: (i, 0)),
            scratch_shapes=[pltpu.VMEM((tile, D_MODEL), jnp.float32)],
        ),
        out_shape=jax.ShapeDtypeStruct((n_slots, PACK_W), jnp.uint32),
        compiler_params=pltpu.CompilerParams(dimension_semantics=("arbitrary",), vmem_limit_bytes=VMEM_LIMIT),
        name="moe_sorted",
    )(tile_group, tile_used, xs, recs, wg, wu, wd)


def _gather_norm_kernel(tile, pos_ref, os_ref, x1_ref, norm_ref, y_ref, buf_ref):
    i = pl.program_id(0)

    def body(j, carry):
        r0 = pl.multiple_of(j * SUBLANES, SUBLANES)
        rows = [os_ref[pl.ds(pos_ref[i * tile + j * SUBLANES + k], 1), :] for k in range(SUBLANES)]
        buf_ref[pl.ds(r0, SUBLANES), :] = jnp.concatenate(rows, axis=0)
        return carry

    lax.fori_loop(0, tile // SUBLANES, body, 0)
    y_ref[...] = _rmsnorm(x1_ref[...] + _unpack_bf16_pairs(buf_ref[...]), norm_ref[...])


def _gather_norm(pos, o_sorted, x1, norm_final, *, tile):
    n_tok = x1.shape[0]
    n_slots = o_sorted.shape[0]
    return pl.pallas_call(
        functools.partial(_gather_norm_kernel, tile),
        grid_spec=pltpu.PrefetchScalarGridSpec(
            num_scalar_prefetch=1,
            grid=(n_tok // tile,),
            in_specs=[pl.BlockSpec((n_slots, PACK_W), lambda i, pos: (0, 0), pipeline_mode=pl.Buffered(1)),
                      pl.BlockSpec((tile, D_MODEL), lambda i, pos: (i, 0)),
                      pl.BlockSpec((1, D_MODEL), lambda i, pos: (0, 0))],
            out_specs=pl.BlockSpec((tile, D_MODEL), lambda i, pos: (i, 0)),
            scratch_shapes=[pltpu.VMEM((tile, PACK_W), jnp.uint32)],
        ),
        out_shape=jax.ShapeDtypeStruct((n_tok, D_MODEL), jnp.float32),
        compiler_params=pltpu.CompilerParams(dimension_semantics=("arbitrary",), vmem_limit_bytes=VMEM_LIMIT),
        name="moe_gather_norm",
    )(pos, o_sorted, x1, norm_final)


def _sorted_slots(rec, tile):
    f32 = jnp.float32
    n_tok = rec.shape[0]
    n_blk = n_tok // RANK_BLOCK
    cls = rec[:, CLS_LANE].astype(jnp.int32)
    onehot = (cls[:, None] == jnp.arange(N_CLS, dtype=jnp.int32)[None, :]).astype(f32)
    blocks = onehot.reshape(n_blk, RANK_BLOCK, N_CLS)
    strict_lower = jnp.tril(jnp.ones((RANK_BLOCK, RANK_BLOCK), f32), -1)
    within = jnp.einsum('ts,bsc->btc', strict_lower, blocks)
    per_block = jnp.sum(blocks, axis=1)
    before = jnp.cumsum(per_block, axis=0) - per_block
    rank = jnp.sum((within + before[:, None, :]) * blocks, axis=-1).reshape(n_tok)
    counts = jnp.sum(per_block, axis=0)
    by_group = counts.reshape(N_GROUPS, CLS_PER_GROUP)
    padded = jnp.ceil(jnp.sum(by_group, axis=1) / tile) * tile
    group_end = jnp.cumsum(padded)
    cls_start = ((group_end - padded)[:, None] + jnp.cumsum(by_group, axis=1) - by_group).reshape(N_CLS)
    pos = jnp.sum(onehot * cls_start[None, :], axis=1) + rank
    n_tiles = n_tok // tile + N_GROUPS
    tile_start = jnp.arange(n_tiles, dtype=f32) * tile
    tile_group = jnp.minimum(jnp.sum((tile_start[:, None] >= group_end[None, :]).astype(jnp.int32), axis=1),
                             N_GROUPS - 1)
    overlaps = ((cls_start[None, :] < tile_start[:, None] + tile)
                & (cls_start[None, :] + counts[None, :] > tile_start[:, None]) & (counts[None, :] > 0))
    tile_used = (jnp.einsum('tc,ce->te', overlaps.astype(f32), jnp.asarray(CLS_EXPERTS)) > 0).astype(jnp.int32)
    return pos.astype(jnp.int32), tile_group.astype(jnp.int32), tile_used.reshape(-1), n_tiles * tile


def _rel_bucket(rel):
    nb = REL_BUCKETS // 2
    n = -rel
    ret = jnp.where(n < 0, nb, 0)
    n = jnp.abs(n)
    max_exact = nb // 2
    nf = jnp.maximum(n, 1).astype(jnp.float32)
    large = max_exact + (jnp.log(nf / max_exact) / math.log(REL_MAX_DIST / max_exact) * (nb - max_exact)).astype(jnp.int32)
    large = jnp.minimum(large, nb - 1)
    return ret + jnp.where(n < max_exact, n, large)


def _band_bias(rel_table):
    lo = -(HIST + CHUNK - 1)
    rels = jnp.arange(lo, CHUNK, dtype=jnp.int32)
    by_rel = jnp.transpose(rel_table[_rel_bucket(rels)]).astype(jnp.float32)
    rows = [by_rel[:, (CHUNK - 1 - qi):(CHUNK - 1 - qi) + BAND] for qi in range(CHUNK)]
    return jnp.stack(rows, axis=1).reshape(N_KV, GROUP * CHUNK, BAND)


def _dup_heads(a):
    a = a.reshape(a.shape[:-1] + (N_KV, HEAD_DIM))
    return jnp.concatenate([a, a], axis=-1).reshape(a.shape[:-2] + (KV2_W,))


def _undup_heads(a):
    a = a.reshape(a.shape[:-1] + (N_KV, LANES))
    return a[..., :HEAD_DIM]


def _block_diag_group(w, j):
    per = RG_GROUP // LRU_BLOCK_W
    blk = w[j * per:(j + 1) * per]
    eye = jnp.eye(per, dtype=w.dtype)
    return jnp.einsum('nde,nm->ndme', blk, eye).reshape(RG_GROUP, RG_GROUP)


def kernel(x_prompt, x_sample, state_lru_h, state_lru_conv, cache_swa_k, cache_swa_v, norm_mix, w_in, b_merge, conv_w, conv_b, w_rg_a, b_rg_a, w_rg_x, b_rg_x, lru_lambda, attn_sink, rel_bias, w_lru_proj, w_attn_proj, w_out, norm_ffn, w_group, b_group, w_router, b_router, w_e_gate, w_e_up, w_e_down, norm_final):
    f32, bf16 = jnp.float32, jnp.bfloat16
    l = 0
    w = w_in[l]
    w_kv2 = jnp.concatenate([_dup_heads(w[:, C_K:C_K + KV_W]), _dup_heads(w[:, C_V:C_V + KV_W])], axis=1).astype(bf16)
    w_rg = jnp.stack([
        jnp.concatenate([_block_diag_group(w_rg_a[l], j), _block_diag_group(w_rg_x[l], j)], axis=1)
        for j in range(N_RG_GROUPS)]).astype(bf16)
    w_rt = jnp.concatenate([w_router[l], w_group[l],
                            jnp.zeros((D_MODEL, ROUTER_W - N_EXPERTS - N_GROUPS), f32)], axis=1).astype(bf16)
    b_rt = jnp.concatenate([b_router[l], b_group[l], jnp.zeros((ROUTER_W - N_EXPERTS - N_GROUPS,), f32)])[None, :]
    weights = (
        norm_mix[l][None, :], w.astype(bf16), w_kv2, b_merge[l][None, :], conv_w[l], conv_b[l][None, :], w_rg,
        b_rg_a[l][None, :], b_rg_x[l][None, :], lru_lambda[l][None, :], attn_sink[l], _band_bias(rel_bias) * LOG2E,
        w_lru_proj[l].astype(bf16), w_attn_proj[l].astype(bf16), w_out[l].astype(bf16), norm_ffn[l][None, :],
        w_rt, b_rt,
    )
    wg, wu = w_e_gate[l].astype(bf16), w_e_up[l].astype(bf16)
    wd = w_e_down[l].astype(bf16).reshape(N_GROUPS, GROUP_FF, D_MODEL)
    nf = norm_final[None, :]

    bp, tp, _ = x_prompt.shape
    assert bp == 1
    xp = x_prompt.reshape(tp, D_MODEL)
    zeros = lambda *s: jnp.zeros(s, f32)
    p_x1, p_xn2, p_comb, p_conv, p_h, p_k, p_v = _mixer(
        xp, zeros(1, SUBLANES, D_RNN), zeros(1, SUBLANES, D_RNN),
        jnp.zeros((1, HIST, KV2_W), bf16), jnp.zeros((1, HIST, KV2_W), bf16), weights,
        nseg=1, seg_len=PROMPT_TILE, mask_history=True, sparse_out=True)
    pos, tile_group, tile_used, n_slots = _sorted_slots(p_comb, MOE_TILE)
    x_sorted, rec_sorted = _scatter_rows(pos, p_xn2, p_comb, n_slots, tile=SORT_TILE)
    o_sorted = _moe_sorted(tile_group, tile_used, x_sorted, rec_sorted, wg, wu, wd, tile=MOE_TILE)
    y_p = _gather_norm(pos, o_sorted, p_x1, nf, tile=SORT_TILE).reshape(x_prompt.shape)

    bs, ts, _ = x_sample.shape
    xs = x_sample.reshape(bs * ts, D_MODEL)
    conv0 = jnp.pad(state_lru_conv[l], ((0, 0), (SUBLANES - (CONV_W - 1), 0), (0, 0)))
    h0 = jnp.broadcast_to(state_lru_h[l][:, None, :], (bs, SUBLANES, D_RNN))
    ck = cache_swa_k[l].reshape(bs, -1, KV_W)
    cv = cache_swa_v[l].reshape(bs, -1, KV_W)
    s_x1, s_xn2, s_comb, s_conv, s_h, s_k, s_v = _mixer(
        xs, conv0, h0, _dup_heads(ck).astype(bf16), _dup_heads(cv).astype(bf16), weights,
        nseg=bs, seg_len=ts, mask_history=False, sparse_out=False)
    y_s = _moe_groups(s_xn2, s_comb, s_x1, wg, wu, wd, nf, tile=bs * ts).reshape(x_sample.shape)

    tail = CONV_W - 1
    p_lru_h = p_h[:, 0, :][None]
    p_lru_conv = p_conv[:, SUBLANES - tail:, :][None]
    p_swa_k = _undup_heads(p_k)[None]
    p_swa_v = _undup_heads(p_v)[None]
    s_lru_h = s_h[:, 0, :][None]
    s_lru_conv = s_conv[:, SUBLANES - tail:, :][None]
    win = ck.shape[1]
    s_swa_k = jnp.concatenate([cache_swa_k[l][:, ts:], _undup_heads(s_k)], axis=1)[:, -win:][None]
    s_swa_v = jnp.concatenate([cache_swa_v[l][:, ts:], _undup_heads(s_v)], axis=1)[:, -win:][None]
    return (y_p, y_s, p_lru_h, p_lru_conv, p_swa_k, p_swa_v, s_lru_h, s_lru_conv, s_swa_k, s_swa_v)
```

```python
import functools
import math

import jax
import jax.numpy as jnp
from jax import lax
from jax.experimental import pallas as pl
from jax.experimental.pallas import tpu as pltpu

D_MODEL = 1024
CHUNK = 64
D_RNN = D_MODEL
CONV_W = 4
LRU_BLOCKS = 16
LRU_BLOCK_W = D_RNN // LRU_BLOCKS
LRU_C = 8.0
N_HEADS = 16
N_KV = 4
HEAD_DIM = 64
GROUP = N_HEADS // N_KV
WINDOW = 128
LOOKBACK_CHUNKS = -(-WINDOW // CHUNK)
HIST = LOOKBACK_CHUNKS * CHUNK
BAND = HIST + CHUNK
Q_W = N_HEADS * HEAD_DIM
KV_W = N_KV * HEAD_DIM
REL_BUCKETS = 32
REL_MAX_DIST = 128
N_GROUPS = 4
EXPERTS_PER_GROUP = 8
N_EXPERTS = N_GROUPS * EXPERTS_PER_GROUP
D_EXPERT = 256
EPS = 1e-6
NEG_INF = -1e30

LANES = 128
SUBLANES = 8
KV2_W = N_KV * LANES
RG_GROUP = 256
N_RG_GROUPS = D_RNN // RG_GROUP
ROUTER_W = LANES
VMEM_LIMIT = 60 * 1024 * 1024
PACK_W = D_MODEL // 2
GROUP_FF = EXPERTS_PER_GROUP * D_EXPERT
GID_LANE = EXPERTS_PER_GROUP
SORT_TILE = 512
PROMPT_TILE = 512

C_LX, C_LG, C_Q = 0, D_RNN, 2 * D_RNN
C_K = C_Q + Q_W
C_V = C_K + KV_W
C_G = C_V + KV_W
P_LG, P_G = 0, D_RNN
P_KV = P_G + 2 * D_MODEL
P_W = P_KV + 2 * KV2_W
PROJ_BLOCK = 256
V_STRIDE = 2 * LANES
SCAN_ROWS = SUBLANES * SUBLANES
LOG2E = math.log2(math.e)


def _rmsnorm(x, g):
    return x * lax.rsqrt(jnp.mean(x * x, axis=-1, keepdims=True) + EPS) * g


def _bdot(a, b):
    return jnp.dot(a, b, preferred_element_type=jnp.float32)


def _sigmoid(x):
    return 0.5 * jnp.tanh(0.5 * x) + 0.5


def _pack_bf16_pairs(x):
    return pltpu.pack_elementwise([x[:, :PACK_W], x[:, PACK_W:]], packed_dtype=jnp.bfloat16)


def _unpack_bf16_pairs(p):
    lo = pltpu.unpack_elementwise(p, index=0, packed_dtype=jnp.bfloat16, unpacked_dtype=jnp.float32)
    hi = pltpu.unpack_elementwise(p, index=1, packed_dtype=jnp.bfloat16, unpacked_dtype=jnp.float32)
    return jnp.concatenate([lo, hi], axis=1)


def _mixer_kernel(nseg, seg_len, mask_history, sparse_out,
                  x_ref, conv0_ref, h0_ref, k0_ref, v0_ref,
                  norm_mix_ref, w_in_ref, w_kv2_ref, b_merge_ref, conv_w_ref, conv_b_ref, w_rg_ref, b_rg_a_ref, b_rg_x_ref,
                  lam_ref, sink_ref, bias_ref, w_lru_ref, w_attn_ref, w_out_ref, norm_ffn_ref, w_rt_ref, b_rt_ref,
                  x1_ref, xn2_ref, comb_ref, conv_out_ref, h_out_ref, k_out_ref, v_out_ref,
                  xp_ref, hc_ref, kbuf_ref, vbuf_ref, a_ref, b_ref, attn_ref, xn_ref, pf_ref, pq_ref):
    step = pl.program_id(0)
    m_rows = nseg * seg_len
    n_chunks = seg_len // CHUNK
    keep = min(seg_len, HIST)

    @pl.when(step == 0)
    def _():
        xp_ref[...] = conv0_ref[...]
        hc_ref[...] = h0_ref[...]
        kbuf_ref[:, 0:HIST, :] = k0_ref[...]
        vbuf_ref[:, :, :] = jnp.ones(vbuf_ref.shape, jnp.bfloat16)
        for kv in range(N_KV):
            vbuf_ref[:, 0:HIST, kv * V_STRIDE:kv * V_STRIDE + LANES] = v0_ref[:, :, kv * LANES:(kv + 1) * LANES]

    x = x_ref[...]
    xn_ref[...] = _rmsnorm(x, norm_mix_ref[...]).astype(jnp.bfloat16)

    tasks = []

    def f32_block(src_ref, c_src, c_dst):
        def run():
            pf_ref[:, c_dst:c_dst + PROJ_BLOCK] = _bdot(xn_ref[...], src_ref[:, c_src:c_src + PROJ_BLOCK])
        return run

    def q_block(c):
        def run():
            pq_ref[:, c:c + PROJ_BLOCK] = (_bdot(xn_ref[...], w_in_ref[:, C_Q + c:C_Q + c + PROJ_BLOCK])
                                           * (HEAD_DIM ** -0.5 * LOG2E)).astype(jnp.bfloat16)
        return run

    for c in range(0, D_RNN, PROJ_BLOCK):
        tasks.append(f32_block(w_in_ref, C_LG + c, P_LG + c))
    for c in range(0, 2 * D_MODEL, PROJ_BLOCK):
        tasks.append(f32_block(w_in_ref, C_G + c, P_G + c))
    for c in range(0, 2 * KV2_W, PROJ_BLOCK):
        tasks.append(f32_block(w_kv2_ref, c, P_KV + c))
    for c in range(0, Q_W, PROJ_BLOCK):
        tasks.append(q_block(c))

    def pump(n):
        for _ in range(min(n, len(tasks))):
            tasks.pop(0)()

    lru_x = _bdot(xn_ref[...], w_in_ref[:, C_LX:C_LX + D_RNN])
    conv_w = conv_w_ref[...]
    row = lax.broadcasted_iota(jnp.int32, (SUBLANES, D_RNN), 0)
    xc_parts = []
    for s in range(nseg):
        xs = lru_x[s * seg_len:(s + 1) * seg_len, :]
        prev = xp_ref[s]
        tail = xs[seg_len - SUBLANES:, :]
        acc = conv_b_ref[...] + xs * conv_w[CONV_W - 1:CONV_W, :]
        pump(1)
        for k in range(1, CONV_W):
            shifted = pltpu.roll(xs, k, axis=0)
            first = jnp.where(row < k, pltpu.roll(prev, k, axis=0), shifted[0:SUBLANES])
            shifted = jnp.concatenate([first, shifted[SUBLANES:]], axis=0)
            acc = acc + shifted * conv_w[CONV_W - 1 - k:CONV_W - k, :]
            pump(1)
        xc_parts.append(acc)
        conv_out_ref[s] = tail
        xp_ref[s] = tail
    xc = xc_parts[0] if nseg == 1 else jnp.concatenate(xc_parts, axis=0)
    xc_b = xc.astype(jnp.bfloat16)

    lam = lam_ref[...]
    log_sig = jnp.minimum(lam, 0.0) - jnp.log1p(jnp.exp(-jnp.abs(lam)))
    c8 = LRU_C * log_sig
    for j in range(N_RG_GROUPS):
        cs = slice(j * RG_GROUP, (j + 1) * RG_GROUP)
        pre = _bdot(xc_b[:, cs], w_rg_ref[j])
        r = _sigmoid(pre[:, :RG_GROUP] + b_rg_a_ref[:, cs])
        i = _sigmoid(pre[:, RG_GROUP:] + b_rg_x_ref[:, cs])
        log_a = c8[:, cs] * r
        a_val = jnp.exp(log_a)
        th = jnp.tanh(log_a)
        b_val = jnp.sqrt(-2.0 * th / (1.0 - th)) * i * xc[:, cs]
        for t in range(RG_GROUP // LANES):
            lt = j * (RG_GROUP // LANES) + t
            a_ref[lt] = a_val[:, t * LANES:(t + 1) * LANES]
            b_ref[lt] = b_val[:, t * LANES:(t + 1) * LANES]
        pump(2)

    row_c = lax.broadcasted_iota(jnp.int32, (SUBLANES, LANES), 0)
    for s in range(nseg):
        for lt in range(D_RNN // LANES):
            cs = slice(lt * LANES, (lt + 1) * LANES)
            carry = hc_ref[s, :, cs]
            for blk in range(seg_len // SCAN_ROWS):
                r0 = s * seg_len + blk * SCAN_ROWS
                slab = lambda ref, i: ref[lt, pl.ds(r0 + i, SUBLANES, stride=SUBLANES), :]
                a_loc, b_loc = [slab(a_ref, 0)], [slab(b_ref, 0)]
                for i in range(1, SUBLANES):
                    a_i = slab(a_ref, i)
                    b_loc.append(a_i * b_loc[-1] + slab(b_ref, i))
                    a_loc.append(a_i * a_loc[-1])
                a_e, b_e = a_loc[-1], b_loc[-1]
                for d in (1, 2, 4):
                    m = row_c >= d
                    b_e = jnp.where(m, a_e * pltpu.roll(b_e, d, axis=0) + b_e, b_e)
                    a_e = jnp.where(m, a_e * pltpu.roll(a_e, d, axis=0), a_e)
                h_end = a_e * carry + b_e
                c_in = jnp.where(row_c == 0, carry, pltpu.roll(h_end, 1, axis=0))
                for i in range(SUBLANES):
                    b_ref[lt, pl.ds(r0 + i, SUBLANES, stride=SUBLANES), :] = a_loc[i] * c_in + b_loc[i]
                carry = jnp.broadcast_to(h_end[SUBLANES - 1:SUBLANES, :], (SUBLANES, LANES))
            hc_ref[s, :, cs] = carry
            h_out_ref[s, :, cs] = carry
            pump(1)
    pump(len(tasks))

    lru_gate = pf_ref[:, P_LG:P_LG + D_RNN]
    h_all = jnp.concatenate([b_ref[lt] for lt in range(D_RNN // LANES)], axis=1)
    lru_y = (h_all * jax.nn.gelu(lru_gate)).astype(jnp.bfloat16)
    mixed = _bdot(lru_y, w_lru_ref[...])
    gates = pf_ref[:, P_G:P_G + D_MODEL] + b_merge_ref[:, 0:D_MODEL]
    mixed = _sigmoid(gates) * mixed

    q = pq_ref[...]
    k2 = pf_ref[:, P_KV:P_KV + KV2_W]
    v2 = pf_ref[:, P_KV + KV2_W:P_KV + 2 * KV2_W]
    lane_q = lax.broadcasted_iota(jnp.int32, (CHUNK, LANES), 1)
    lane_o = lax.broadcasted_iota(jnp.int32, (CHUNK, LANES), 1)
    key_lane = lax.broadcasted_iota(jnp.int32, (1, BAND), 1)
    row_grp = lax.broadcasted_iota(jnp.int32, (GROUP * CHUNK, 1), 0) // CHUNK
    for s in range(nseg):
        rows = slice(s * seg_len, (s + 1) * seg_len)
        kbuf_ref[s, HIST:HIST + seg_len, :] = k2[rows].astype(jnp.bfloat16)
        for kv in range(N_KV):
            vbuf_ref[s, HIST:HIST + seg_len, kv * V_STRIDE:kv * V_STRIDE + LANES] = (
                v2[rows, kv * LANES:(kv + 1) * LANES].astype(jnp.bfloat16))
        k_out_ref[s] = k2[s * seg_len + seg_len - keep:(s + 1) * seg_len]
        v_out_ref[s] = v2[s * seg_len + seg_len - keep:(s + 1) * seg_len]
        for c in range(n_chunks):
            q_c = q[s * seg_len + c * CHUNK:s * seg_len + (c + 1) * CHUNK]
            slabs = []
            for kv in range(N_KV):
                parts = []
                for g in range(GROUP):
                    col = kv * GROUP * HEAD_DIM + (g // 2) * LANES
                    slab = q_c[:, col:col + LANES]
                    keep_lo = (g % 2) == 0
                    sel = (lane_q < HEAD_DIM) if keep_lo else (lane_q >= HEAD_DIM)
                    parts.append(jnp.where(sel, slab, jnp.zeros_like(slab)))
                q_stack = jnp.concatenate(parts, axis=0)
                k_band = kbuf_ref[s, c * CHUNK:c * CHUNK + BAND, kv * LANES:(kv + 1) * LANES]
                v_band = vbuf_ref[s, c * CHUNK:c * CHUNK + BAND, kv * V_STRIDE:(kv + 1) * V_STRIDE]
                sc = lax.dot_general(q_stack, k_band, (((1,), (1,)), ((), ())),
                                     preferred_element_type=jnp.float32)
                sc = sc + bias_ref[kv]
                if mask_history:
                    first_valid = HIST - (step * n_chunks + c) * CHUNK
                    sc = jnp.where(key_lane >= first_valid, sc, NEG_INF)
                sink = jnp.zeros((GROUP * CHUNK, 1), jnp.float32)
                for g in range(GROUP):
                    sink = jnp.where(row_grp == g, sink_ref[kv * GROUP + g] * LOG2E, sink)
                m = jnp.maximum(jnp.max(sc, axis=-1, keepdims=True), sink)
                p = jnp.exp2(sc - m).astype(jnp.bfloat16)
                o = _bdot(p, v_band)
                denom = o[:, LANES:2 * LANES] + jnp.exp2(sink - m)
                o = o[:, 0:LANES] / denom
                for pair in range(GROUP // 2):
                    lo = o[(2 * pair) * CHUNK:(2 * pair + 1) * CHUNK]
                    hi = o[(2 * pair + 1) * CHUNK:(2 * pair + 2) * CHUNK]
                    slabs.append(jnp.where(lane_o < HEAD_DIM, lo, hi))
            attn_ref[s * seg_len + c * CHUNK:s * seg_len + (c + 1) * CHUNK, :] = (
                jnp.concatenate(slabs, axis=1).astype(jnp.bfloat16))
        kbuf_ref[s, 0:HIST, :] = kbuf_ref[s, seg_len:seg_len + HIST, :]
        vbuf_ref[s, 0:HIST, :] = vbuf_ref[s, seg_len:seg_len + HIST, :]

    attn = _bdot(attn_ref[...], w_attn_ref[...])
    gates = pf_ref[:, P_G + D_MODEL:P_G + 2 * D_MODEL] + b_merge_ref[:, D_MODEL:2 * D_MODEL]
    mixed = (mixed + _sigmoid(gates) * attn).astype(jnp.bfloat16)
    x1 = x + _bdot(mixed, w_out_ref[...])
    x1_ref[...] = x1

    xn2 = _rmsnorm(x1, norm_ffn_ref[...])
    xn2_ref[...] = _pack_bf16_pairs(xn2) if sparse_out else xn2.astype(jnp.bfloat16)
    logits = _bdot(xn2.astype(jnp.bfloat16), w_rt_ref[...]) + b_rt_ref[...]
    lane = lax.broadcasted_iota(jnp.int32, (m_rows, ROUTER_W), 1).astype(jnp.float32)
    far = jnp.float32(2 * ROUTER_W)
    is_group = (lane >= N_EXPERTS) & (lane < N_EXPERTS + N_GROUPS)
    gl = jnp.where(is_group, logits, NEG_INF)
    g_max = jnp.max(gl, axis=-1, keepdims=True)
    g_idx = jnp.min(jnp.where(gl == g_max, lane, far), axis=-1, keepdims=True) - N_EXPERTS
    g_w = 1.0 / jnp.sum(jnp.where(is_group, jnp.exp(gl - g_max), 0.0), axis=-1, keepdims=True)
    in_group = (lane < N_EXPERTS) & (jnp.floor(lane * (1.0 / EXPERTS_PER_GROUP)) == g_idx)
    el = jnp.where(in_group, logits, NEG_INF)
    e1 = jnp.max(el, axis=-1, keepdims=True)
    i1 = jnp.min(jnp.where(el == e1, lane, far), axis=-1, keepdims=True)
    el2 = jnp.where(lane == i1, NEG_INF, el)
    e2 = jnp.max(el2, axis=-1, keepdims=True)
    i2 = jnp.min(jnp.where(el2 == e2, lane, far), axis=-1, keepdims=True)
    t = jnp.exp(e2 - e1)
    w1 = g_w / (1.0 + t)
    w2 = w1 * t
    if sparse_out:
        e_lane = lane + g_idx * EXPERTS_PER_GROUP
        rec = jnp.where(e_lane == i1, w1, 0.0) + jnp.where(e_lane == i2, w2, 0.0)
        rec = jnp.where(lane < EXPERTS_PER_GROUP, rec, 0.0)
        comb_ref[...] = jnp.where(lane == GID_LANE, g_idx, rec)
    else:
        comb_ref[...] = jnp.where(lane == i1, w1, 0.0) + jnp.where(lane == i2, w2, 0.0)


def _const_spec(shape):
    zeros = (0,) * len(shape)
    return pl.BlockSpec(shape, lambda i: zeros, pipeline_mode=pl.Buffered(1))


def _mixer(x, conv0, h0, k0, v0, weights, *, nseg, seg_len, mask_history, sparse_out):
    n_tok = x.shape[0]
    m_rows = nseg * seg_len
    n_steps = n_tok // m_rows
    keep = min(seg_len, HIST)
    if n_steps == 1:
        row_spec = lambda w: pl.BlockSpec((m_rows, w), lambda i: (i, 0), pipeline_mode=pl.Buffered(1))
    else:
        row_spec = lambda w: pl.BlockSpec((m_rows, w), lambda i: (i, 0))
    in_specs = [row_spec(D_MODEL), _const_spec(conv0.shape), _const_spec(h0.shape), _const_spec(k0.shape),
                _const_spec(v0.shape)]
    for w in weights:
        if w.ndim == 1:
            in_specs.append(pl.BlockSpec(memory_space=pltpu.SMEM))
        else:
            in_specs.append(_const_spec(w.shape))
    state_spec = lambda r, w: pl.BlockSpec((nseg, r, w), lambda i: (0, 0, 0))
    out_shape = (
        jax.ShapeDtypeStruct((n_tok, D_MODEL), jnp.float32),
        (jax.ShapeDtypeStruct((n_tok, PACK_W), jnp.uint32) if sparse_out
         else jax.ShapeDtypeStruct((n_tok, D_MODEL), jnp.bfloat16)),
        jax.ShapeDtypeStruct((n_tok, ROUTER_W), jnp.float32),
        jax.ShapeDtypeStruct((nseg, SUBLANES, D_RNN), jnp.float32),
        jax.ShapeDtypeStruct((nseg, SUBLANES, D_RNN), jnp.float32),
        jax.ShapeDtypeStruct((nseg, keep, KV2_W), jnp.float32),
        jax.ShapeDtypeStruct((nseg, keep, KV2_W), jnp.float32),
    )
    out_specs = (row_spec(D_MODEL), row_spec(PACK_W if sparse_out else D_MODEL), row_spec(ROUTER_W),
                 state_spec(SUBLANES, D_RNN), state_spec(SUBLANES, D_RNN),
                 state_spec(keep, KV2_W), state_spec(keep, KV2_W))
    scratch = [
        pltpu.VMEM((nseg, SUBLANES, D_RNN), jnp.float32),
        pltpu.VMEM((nseg, SUBLANES, D_RNN), jnp.float32),
        pltpu.VMEM((nseg, HIST + seg_len, KV2_W), jnp.bfloat16),
        pltpu.VMEM((nseg, HIST + seg_len, N_KV * V_STRIDE), jnp.bfloat16),
        pltpu.VMEM((D_RNN // LANES, m_rows, LANES), jnp.float32),
        pltpu.VMEM((D_RNN // LANES, m_rows, LANES), jnp.float32),
        pltpu.VMEM((m_rows, Q_W), jnp.bfloat16),
        pltpu.VMEM((m_rows, D_MODEL), jnp.bfloat16),
        pltpu.VMEM((m_rows, P_W), jnp.float32),
        pltpu.VMEM((m_rows, Q_W), jnp.bfloat16),
    ]
    return pl.pallas_call(
        functools.partial(_mixer_kernel, nseg, seg_len, mask_history, sparse_out),
        grid=(n_steps,),
        in_specs=in_specs,
        out_specs=out_specs,
        out_shape=out_shape,
        scratch_shapes=scratch,
        compiler_params=pltpu.CompilerParams(dimension_semantics=("arbitrary",), vmem_limit_bytes=VMEM_LIMIT),
        name="mixer_prompt" if mask_history else "mixer_sample",
    )(x, conv0, h0, k0, v0, *weights)


def _group_swiglu(x, cols, wg_ref, wu_ref, wd_ref):
    parts = []
    for e in range(EXPERTS_PER_GROUP):
        h = jax.nn.silu(_bdot(x, wg_ref[e])) * _bdot(x, wu_ref[e])
        parts.append((h * cols[e]).astype(jnp.bfloat16))
    return _bdot(jnp.concatenate(parts, axis=1), wd_ref[...])


def _moe_groups_kernel(xn2_ref, comb_ref, x1_ref, wg_ref, wu_ref, wd_ref, norm_ref, y_ref, acc_ref):
    g = pl.program_id(1)

    @pl.when(g == 0)
    def _():
        acc_ref[...] = jnp.zeros_like(acc_ref)

    comb = comb_ref[...]
    lane = lax.broadcasted_iota(jnp.int32, comb.shape, 1)
    cols = [jnp.sum(jnp.where(lane == g * EXPERTS_PER_GROUP + e, comb, 0.0), axis=-1, keepdims=True)
            for e in range(EXPERTS_PER_GROUP)]
    acc_ref[...] += _group_swiglu(xn2_ref[...], cols, wg_ref, wu_ref, wd_ref)

    @pl.when(g == N_GROUPS - 1)
    def _():
        y_ref[...] = _rmsnorm(x1_ref[...] + acc_ref[...], norm_ref[...])


def _group_weight_specs(group_of):
    return [pl.BlockSpec((EXPERTS_PER_GROUP, D_MODEL, D_EXPERT), lambda *a: (group_of(*a), 0, 0)),
            pl.BlockSpec((EXPERTS_PER_GROUP, D_MODEL, D_EXPERT), lambda *a: (group_of(*a), 0, 0)),
            pl.BlockSpec((None, GROUP_FF, D_MODEL), lambda *a: (group_of(*a), 0, 0))]


def _moe_groups(xn2, comb, x1, wg, wu, wd, norm_final, *, tile):
    n_tok = xn2.shape[0]
    row_spec = lambda w: pl.BlockSpec((tile, w), lambda i, g: (i, 0))
    return pl.pallas_call(
        _moe_groups_kernel,
        grid=(n_tok // tile, N_GROUPS),
        in_specs=[row_spec(D_MODEL), row_spec(ROUTER_W), row_spec(D_MODEL),
                  *_group_weight_specs(lambda i, g: g),
                  pl.BlockSpec((1, D_MODEL), lambda i, g: (0, 0))],
        out_specs=row_spec(D_MODEL),
        out_shape=jax.ShapeDtypeStruct((n_tok, D_MODEL), jnp.float32),
        scratch_shapes=[pltpu.VMEM((tile, D_MODEL), jnp.float32)],
        compiler_params=pltpu.CompilerParams(dimension_semantics=("arbitrary", "arbitrary"),
                                             vmem_limit_bytes=VMEM_LIMIT),
        name="moe_groups",
    )(xn2, comb, x1, wg, wu, wd, norm_final)


def _scatter_rows_kernel(tile, pos_ref, x_ref, rec_ref, xs_ref, recs_ref):
    i = pl.program_id(0)

    @pl.when(i == 0)
    def _():
        xs_ref[...] = jnp.zeros_like(xs_ref)
        recs_ref[...] = jnp.zeros_like(recs_ref)

    def body(j, carry):
        r0 = pl.multiple_of(j * SUBLANES, SUBLANES)
        xb = x_ref[pl.ds(r0, SUBLANES), :]
        rb = rec_ref[pl.ds(r0, SUBLANES), :]
        for k in range(SUBLANES):
            p = pos_ref[i * tile + j * SUBLANES + k]
            xs_ref[pl.ds(p, 1), :] = xb[k:k + 1, :]
            recs_ref[pl.ds(p, 1), :] = rb[k:k + 1, :]
        return carry

    lax.fori_loop(0, tile // SUBLANES, body, 0)


def _scatter_rows(pos, xn2p, rec, n_slots, *, tile):
    n_tok = xn2p.shape[0]
    return pl.pallas_call(
        functools.partial(_scatter_rows_kernel, tile),
        grid_spec=pltpu.PrefetchScalarGridSpec(
            num_scalar_prefetch=1,
            grid=(n_tok // tile,),
            in_specs=[pl.BlockSpec((tile, PACK_W), lambda i, pos: (i, 0)),
                      pl.BlockSpec((tile, ROUTER_W), lambda i, pos: (i, 0))],
            out_specs=[pl.BlockSpec((n_slots, PACK_W), lambda i, pos: (0, 0)),
                       pl.BlockSpec((n_slots, ROUTER_W), lambda i, pos: (0, 0))],
        ),
        out_shape=(jax.ShapeDtypeStruct((n_slots, PACK_W), jnp.uint32),
                   jax.ShapeDtypeStruct((n_slots, ROUTER_W), jnp.float32)),
        compiler_params=pltpu.CompilerParams(dimension_semantics=("arbitrary",), vmem_limit_bytes=VMEM_LIMIT),
        name="moe_scatter",
    )(pos, xn2p, rec)


def _moe_sorted_kernel(tg_ref, xs_ref, recs_ref, wg_ref, wu_ref, wd_ref, o_ref):
    x = _unpack_bf16_pairs(xs_ref[...]).astype(jnp.bfloat16)
    rec = recs_ref[...]
    cols = [rec[:, e:e + 1] for e in range(EXPERTS_PER_GROUP)]
    o_ref[...] = _pack_bf16_pairs(_group_swiglu(x, cols, wg_ref, wu_ref, wd_ref))


def _moe_sorted(tile_group, xs, recs, wg, wu, wd, *, tile):
    n_slots = xs.shape[0]
    return pl.pallas_call(
        _moe_sorted_kernel,
        grid_spec=pltpu.PrefetchScalarGridSpec(
            num_scalar_prefetch=1,
            grid=(n_slots // tile,),
            in_specs=[pl.BlockSpec((tile, PACK_W), lambda i, tg: (i, 0)),
                      pl.BlockSpec((tile, ROUTER_W), lambda i, tg: (i, 0)),
                      *_group_weight_specs(lambda i, tg: tg[i])],
            out_specs=pl.BlockSpec((tile, PACK_W), lambda i, tg: (i, 0)),
        ),
        out_shape=jax.ShapeDtypeStruct((n_slots, PACK_W), jnp.uint32),
        compiler_params=pltpu.CompilerParams(dimension_semantics=("arbitrary",), vmem_limit_bytes=VMEM_LIMIT),
        name="moe_sorted",
    )(tile_group, xs, recs, wg, wu, wd)


def _gather_norm_kernel(tile, pos_ref, os_ref, x1_ref, norm_ref, y_ref, buf_ref):
    i = pl.program_id(0)

    def body(j, carry):
        r0 = pl.multiple_of(j * SUBLANES, SUBLANES)
        rows = [os_ref[pl.ds(pos_ref[i * tile + j * SUBLANES + k], 1), :] for k in range(SUBLANES)]
        buf_ref[pl.ds(r0, SUBLANES), :] = jnp.concatenate(rows, axis=0)
        return carry

    lax.fori_loop(0, tile // SUBLANES, body, 0)
    y_ref[...] = _rmsnorm(x1_ref[...] + _unpack_bf16_pairs(buf_ref[...]), norm_ref[...])


def _gather_norm(pos, o_sorted, x1, norm_final, *, tile):
    n_tok = x1.shape[0]
    n_slots = o_sorted.shape[0]
    return pl.pallas_call(
        functools.partial(_gather_norm_kernel, tile),
        grid_spec=pltpu.PrefetchScalarGridSpec(
            num_scalar_prefetch=1,
            grid=(n_tok // tile,),
            in_specs=[pl.BlockSpec((n_slots, PACK_W), lambda i, pos: (0, 0), pipeline_mode=pl.Buffered(1)),
                      pl.BlockSpec((tile, D_MODEL), lambda i, pos: (i, 0)),
                      pl.BlockSpec((1, D_MODEL), lambda i, pos: (0, 0))],
            out_specs=pl.BlockSpec((tile, D_MODEL), lambda i, pos: (i, 0)),
            scratch_shapes=[pltpu.VMEM((tile, PACK_W), jnp.uint32)],
        ),
        out_shape=jax.ShapeDtypeStruct((n_tok, D_MODEL), jnp.float32),
        compiler_params=pltpu.CompilerParams(dimension_semantics=("arbitrary",), vmem_limit_bytes=VMEM_LIMIT),
        name="moe_gather_norm",
    )(pos, o_sorted, x1, norm_final)


def _sorted_slots(rec, tile):
    n_tok = rec.shape[0]
    gid = rec[:, GID_LANE].astype(jnp.int32)
    onehot = (gid[:, None] == jnp.arange(N_GROUPS, dtype=jnp.int32)[None, :]).astype(jnp.int32)
    csum = jnp.cumsum(onehot, axis=0)
    rank = jnp.sum((csum - onehot) * onehot, axis=1)
    padded = ((csum[-1] + tile - 1) // tile) * tile
    end = jnp.cumsum(padded)
    pos = jnp.sum(onehot * (end - padded)[None, :], axis=1) + rank
    n_tiles = n_tok // tile + N_GROUPS
    tile_start = jnp.arange(n_tiles, dtype=jnp.int32) * tile
    tile_group = jnp.minimum(jnp.sum((tile_start[:, None] >= end[None, :]).astype(jnp.int32), axis=1), N_GROUPS - 1)
    return pos.astype(jnp.int32), tile_group.astype(jnp.int32), n_tiles * tile


def _rel_bucket(rel):
    nb = REL_BUCKETS // 2
    n = -rel
    ret = jnp.where(n < 0, nb, 0)
    n = jnp.abs(n)
    max_exact = nb // 2
    nf = jnp.maximum(n, 1).astype(jnp.float32)
    large = max_exact + (jnp.log(nf / max_exact) / math.log(REL_MAX_DIST / max_exact) * (nb - max_exact)).astype(jnp.int32)
    large = jnp.minimum(large, nb - 1)
    return ret + jnp.where(n < max_exact, n, large)


def _band_bias(rel_table):
    lo = -(HIST + CHUNK - 1)
    rels = jnp.arange(lo, CHUNK, dtype=jnp.int32)
    by_rel = jnp.transpose(rel_table[_rel_bucket(rels)]).astype(jnp.float32)
    rows = [by_rel[:, (CHUNK - 1 - qi):(CHUNK - 1 - qi) + BAND] for qi in range(CHUNK)]
    return jnp.stack(rows, axis=1).reshape(N_KV, GROUP * CHUNK, BAND)


def _dup_heads(a):
    a = a.reshape(a.shape[:-1] + (N_KV, HEAD_DIM))
    return jnp.concatenate([a, a], axis=-1).reshape(a.shape[:-2] + (KV2_W,))


def _undup_heads(a):
    a = a.reshape(a.shape[:-1] + (N_KV, LANES))
    return a[..., :HEAD_DIM]


def _block_diag_group(w, j):
    per = RG_GROUP // LRU_BLOCK_W
    blk = w[j * per:(j + 1) * per]
    eye = jnp.eye(per, dtype=w.dtype)
    return jnp.einsum('nde,nm->ndme', blk, eye).reshape(RG_GROUP, RG_GROUP)


def kernel(x_prompt, x_sample, state_lru_h, state_lru_conv, cache_swa_k, cache_swa_v, norm_mix, w_in, b_merge, conv_w, conv_b, w_rg_a, b_rg_a, w_rg_x, b_rg_x, lru_lambda, attn_sink, rel_bias, w_lru_proj, w_attn_proj, w_out, norm_ffn, w_group, b_group, w_router, b_router, w_e_gate, w_e_up, w_e_down, norm_final):
    f32, bf16 = jnp.float32, jnp.bfloat16
    l = 0
    w = w_in[l]
    w_kv2 = jnp.concatenate([_dup_heads(w[:, C_K:C_K + KV_W]), _dup_heads(w[:, C_V:C_V + KV_W])], axis=1).astype(bf16)
    w_rg = jnp.stack([
        jnp.concatenate([_block_diag_group(w_rg_a[l], j), _block_diag_group(w_rg_x[l], j)], axis=1)
        for j in range(N_RG_GROUPS)]).astype(bf16)
    w_rt = jnp.concatenate([w_router[l], w_group[l],
                            jnp.zeros((D_MODEL, ROUTER_W - N_EXPERTS - N_GROUPS), f32)], axis=1).astype(bf16)
    b_rt = jnp.concatenate([b_router[l], b_group[l], jnp.zeros((ROUTER_W - N_EXPERTS - N_GROUPS,), f32)])[None, :]
    weights = (
        norm_mix[l][None, :], w.astype(bf16), w_kv2, b_merge[l][None, :], conv_w[l], conv_b[l][None, :], w_rg,
        b_rg_a[l][None, :], b_rg_x[l][None, :], lru_lambda[l][None, :], attn_sink[l], _band_bias(rel_bias) * LOG2E,
        w_lru_proj[l].astype(bf16), w_attn_proj[l].astype(bf16), w_out[l].astype(bf16), norm_ffn[l][None, :],
        w_rt, b_rt,
    )
    wg, wu = w_e_gate[l].astype(bf16), w_e_up[l].astype(bf16)
    wd = w_e_down[l].astype(bf16).reshape(N_GROUPS, GROUP_FF, D_MODEL)
    nf = norm_final[None, :]

    bp, tp, _ = x_prompt.shape
    assert bp == 1
    xp = x_prompt.reshape(tp, D_MODEL)
    zeros = lambda *s: jnp.zeros(s, f32)
    p_x1, p_xn2, p_comb, p_conv, p_h, p_k, p_v = _mixer(
        xp, zeros(1, SUBLANES, D_RNN), zeros(1, SUBLANES, D_RNN),
        jnp.zeros((1, HIST, KV2_W), bf16), jnp.zeros((1, HIST, KV2_W), bf16), weights,
        nseg=1, seg_len=PROMPT_TILE, mask_history=True, sparse_out=True)
    pos, tile_group, n_slots = _sorted_slots(p_comb, SORT_TILE)
    x_sorted, rec_sorted = _scatter_rows(pos, p_xn2, p_comb, n_slots, tile=SORT_TILE)
    o_sorted = _moe_sorted(tile_group, x_sorted, rec_sorted, wg, wu, wd, tile=SORT_TILE)
    y_p = _gather_norm(pos, o_sorted, p_x1, nf, tile=SORT_TILE).reshape(x_prompt.shape)

    bs, ts, _ = x_sample.shape
    xs = x_sample.reshape(bs * ts, D_MODEL)
    conv0 = jnp.pad(state_lru_conv[l], ((0, 0), (SUBLANES - (CONV_W - 1), 0), (0, 0)))
    h0 = jnp.broadcast_to(state_lru_h[l][:, None, :], (bs, SUBLANES, D_RNN))
    ck = cache_swa_k[l].reshape(bs, -1, KV_W)
    cv = cache_swa_v[l].reshape(bs, -1, KV_W)
    s_x1, s_xn2, s_comb, s_conv, s_h, s_k, s_v = _mixer(
        xs, conv0, h0, _dup_heads(ck).astype(bf16), _dup_heads(cv).astype(bf16), weights,
        nseg=bs, seg_len=ts, mask_history=False, sparse_out=False)
    y_s = _moe_groups(s_xn2, s_comb, s_x1, wg, wu, wd, nf, tile=bs * ts).reshape(x_sample.shape)

    tail = CONV_W - 1
    p_lru_h = p_h[:, 0, :][None]
    p_lru_conv = p_conv[:, SUBLANES - tail:, :][None]
    p_swa_k = _undup_heads(p_k)[None]
    p_swa_v = _undup_heads(p_v)[None]
    s_lru_h = s_h[:, 0, :][None]
    s_lru_conv = s_conv[:, SUBLANES - tail:, :][None]
    win = ck.shape[1]
    s_swa_k = jnp.concatenate([cache_swa_k[l][:, ts:], _undup_heads(s_k)], axis=1)[:, -win:][None]
    s_swa_v = jnp.concatenate([cache_swa_v[l][:, ts:], _undup_heads(s_v)], axis=1)[:, -win:][None]
    return (y_p, y_s, p_lru_h, p_lru_conv, p_swa_k, p_swa_v, s_lru_h, s_lru_conv, s_swa_k, s_swa_v)
```

```python
import functools
import math

import jax
import jax.numpy as jnp
from jax import lax
from jax.experimental import pallas as pl
from jax.experimental.pallas import tpu as pltpu

D_MODEL = 1024
CHUNK = 64
D_RNN = D_MODEL
CONV_W = 4
LRU_BLOCKS = 16
LRU_BLOCK_W = D_RNN // LRU_BLOCKS
LRU_C = 8.0
N_HEADS = 16
N_KV = 4
HEAD_DIM = 64
GROUP = N_HEADS // N_KV
WINDOW = 128
LOOKBACK_CHUNKS = -(-WINDOW // CHUNK)
HIST = LOOKBACK_CHUNKS * CHUNK
BAND = HIST + CHUNK
Q_W = N_HEADS * HEAD_DIM
KV_W = N_KV * HEAD_DIM
REL_BUCKETS = 32
REL_MAX_DIST = 128
N_GROUPS = 4
EXPERTS_PER_GROUP = 8
N_EXPERTS = N_GROUPS * EXPERTS_PER_GROUP
D_EXPERT = 256
EPS = 1e-6
NEG_INF = -1e30

LANES = 128
SUBLANES = 8
KV2_W = N_KV * LANES
RG_GROUP = 256
N_RG_GROUPS = D_RNN // RG_GROUP
ROUTER_W = LANES
VMEM_LIMIT = 60 * 1024 * 1024
PACK_W = D_MODEL // 2
GROUP_FF = EXPERTS_PER_GROUP * D_EXPERT
GID_LANE = EXPERTS_PER_GROUP
SORT_TILE = 512
PROMPT_TILE = 512

C_LX, C_LG, C_Q = 0, D_RNN, 2 * D_RNN
C_K = C_Q + Q_W
C_V = C_K + KV_W
C_G = C_V + KV_W
P_LG, P_G = 0, D_RNN
P_KV = P_G + 2 * D_MODEL
P_W = P_KV + 2 * KV_W
PROJ_BLOCK = 256
V_STRIDE = 2 * LANES
SCAN_ROWS = SUBLANES * SUBLANES
LOG2E = math.log2(math.e)


def _rmsnorm(x, g):
    return x * lax.rsqrt(jnp.mean(x * x, axis=-1, keepdims=True) + EPS) * g


def _bdot(a, b):
    return jnp.dot(a, b, preferred_element_type=jnp.float32)


def _sigmoid(x):
    return 0.5 * jnp.tanh(0.5 * x) + 0.5


def _pack_bf16_pairs(x):
    return pltpu.pack_elementwise([x[:, :PACK_W], x[:, PACK_W:]], packed_dtype=jnp.bfloat16)


def _unpack_bf16_pairs(p):
    lo = pltpu.unpack_elementwise(p, index=0, packed_dtype=jnp.bfloat16, unpacked_dtype=jnp.float32)
    hi = pltpu.unpack_elementwise(p, index=1, packed_dtype=jnp.bfloat16, unpacked_dtype=jnp.float32)
    return jnp.concatenate([lo, hi], axis=1)


def _mixer_kernel(nseg, seg_len, mask_history, sparse_out,
                  x_ref, conv0_ref, h0_ref, k0_ref, v0_ref,
                  norm_mix_ref, w_in_ref, b_merge_ref, conv_w_ref, conv_b_ref, w_rg_ref, b_rg_a_ref, b_rg_x_ref,
                  lam_ref, sink_ref, bias_ref, w_lru_ref, w_attn_ref, w_out_ref, norm_ffn_ref, w_rt_ref, b_rt_ref,
                  x1_ref, xn2_ref, comb_ref, conv_out_ref, h_out_ref, k_out_ref, v_out_ref,
                  xp_ref, hc_ref, kbuf_ref, vbuf_ref, a_ref, b_ref, attn_ref, xn_ref, pf_ref, pq_ref):
    step = pl.program_id(0)
    m_rows = nseg * seg_len
    n_chunks = seg_len // CHUNK
    keep = min(seg_len, HIST)

    @pl.when(step == 0)
    def _():
        xp_ref[...] = conv0_ref[...]
        hc_ref[...] = h0_ref[...]
        kbuf_ref[:, 0:HIST, :] = k0_ref[...]
        vbuf_ref[:, :, :] = jnp.ones(vbuf_ref.shape, jnp.bfloat16)
        for kv in range(N_KV):
            vbuf_ref[:, 0:HIST, kv * V_STRIDE:kv * V_STRIDE + LANES] = v0_ref[:, :, kv * LANES:(kv + 1) * LANES]

    x = x_ref[...]
    xn_ref[...] = _rmsnorm(x, norm_mix_ref[...]).astype(jnp.bfloat16)

    tasks = []

    def f32_block(src_ref, c_src, c_dst):
        def run():
            pf_ref[:, c_dst:c_dst + PROJ_BLOCK] = _bdot(xn_ref[...], src_ref[:, c_src:c_src + PROJ_BLOCK])
        return run

    def q_block(c):
        def run():
            pq_ref[:, c:c + PROJ_BLOCK] = (_bdot(xn_ref[...], w_in_ref[:, C_Q + c:C_Q + c + PROJ_BLOCK])
                                           * (HEAD_DIM ** -0.5 * LOG2E)).astype(jnp.bfloat16)
        return run

    for c in range(0, D_RNN, PROJ_BLOCK):
        tasks.append(f32_block(w_in_ref, C_LG + c, P_LG + c))
    for c in range(0, 2 * D_MODEL, PROJ_BLOCK):
        tasks.append(f32_block(w_in_ref, C_G + c, P_G + c))
    for c in range(0, 2 * KV_W, PROJ_BLOCK):
        tasks.append(f32_block(w_in_ref, C_K + c, P_KV + c))
    for c in range(0, Q_W, PROJ_BLOCK):
        tasks.append(q_block(c))

    def pump(n):
        for _ in range(min(n, len(tasks))):
            tasks.pop(0)()

    lru_x = _bdot(xn_ref[...], w_in_ref[:, C_LX:C_LX + D_RNN])
    conv_w = conv_w_ref[...]
    row = lax.broadcasted_iota(jnp.int32, (SUBLANES, D_RNN), 0)
    xc_parts = []
    for s in range(nseg):
        xs = lru_x[s * seg_len:(s + 1) * seg_len, :]
        prev = xp_ref[s]
        tail = xs[seg_len - SUBLANES:, :]
        acc = conv_b_ref[...] + xs * conv_w[CONV_W - 1:CONV_W, :]
        pump(1)
        for k in range(1, CONV_W):
            shifted = pltpu.roll(xs, k, axis=0)
            first = jnp.where(row < k, pltpu.roll(prev, k, axis=0), shifted[0:SUBLANES])
            shifted = jnp.concatenate([first, shifted[SUBLANES:]], axis=0)
            acc = acc + shifted * conv_w[CONV_W - 1 - k:CONV_W - k, :]
            pump(1)
        xc_parts.append(acc)
        conv_out_ref[s] = tail
        xp_ref[s] = tail
    xc = xc_parts[0] if nseg == 1 else jnp.concatenate(xc_parts, axis=0)
    xc_b = xc.astype(jnp.bfloat16)

    lam = lam_ref[...]
    log_sig = jnp.minimum(lam, 0.0) - jnp.log1p(jnp.exp(-jnp.abs(lam)))
    c8 = LRU_C * log_sig
    for j in range(N_RG_GROUPS):
        cs = slice(j * RG_GROUP, (j + 1) * RG_GROUP)
        pre = _bdot(xc_b[:, cs], w_rg_ref[j])
        r = _sigmoid(pre[:, :RG_GROUP] + b_rg_a_ref[:, cs])
        i = _sigmoid(pre[:, RG_GROUP:] + b_rg_x_ref[:, cs])
        log_a = c8[:, cs] * r
        a_val = jnp.exp(log_a)
        th = jnp.tanh(log_a)
        b_val = jnp.sqrt(-2.0 * th / (1.0 - th)) * i * xc[:, cs]
        for t in range(RG_GROUP // LANES):
            lt = j * (RG_GROUP // LANES) + t
            a_ref[lt] = a_val[:, t * LANES:(t + 1) * LANES]
            b_ref[lt] = b_val[:, t * LANES:(t + 1) * LANES]
        pump(2)

    row_c = lax.broadcasted_iota(jnp.int32, (SUBLANES, LANES), 0)
    for s in range(nseg):
        for lt in range(D_RNN // LANES):
            cs = slice(lt * LANES, (lt + 1) * LANES)
            carry = hc_ref[s, :, cs]
            for blk in range(seg_len // SCAN_ROWS):
                r0 = s * seg_len + blk * SCAN_ROWS
                slab = lambda ref, i: ref[lt, pl.ds(r0 + i, SUBLANES, stride=SUBLANES), :]
                a_loc, b_loc = [slab(a_ref, 0)], [slab(b_ref, 0)]
                for i in range(1, SUBLANES):
                    a_i = slab(a_ref, i)
                    b_loc.append(a_i * b_loc[-1] + slab(b_ref, i))
                    a_loc.append(a_i * a_loc[-1])
                a_e, b_e = a_loc[-1], b_loc[-1]
                for d in (1, 2, 4):
                    m = row_c >= d
                    b_e = jnp.where(m, a_e * pltpu.roll(b_e, d, axis=0) + b_e, b_e)
                    a_e = jnp.where(m, a_e * pltpu.roll(a_e, d, axis=0), a_e)
                h_end = a_e * carry + b_e
                c_in = jnp.where(row_c == 0, carry, pltpu.roll(h_end, 1, axis=0))
                for i in range(SUBLANES):
                    b_ref[lt, pl.ds(r0 + i, SUBLANES, stride=SUBLANES), :] = a_loc[i] * c_in + b_loc[i]
                carry = jnp.broadcast_to(h_end[SUBLANES - 1:SUBLANES, :], (SUBLANES, LANES))
            hc_ref[s, :, cs] = carry
            h_out_ref[s, :, cs] = carry
            pump(1)
    pump(len(tasks))

    lru_gate = pf_ref[:, P_LG:P_LG + D_RNN]
    h_all = jnp.concatenate([b_ref[lt] for lt in range(D_RNN // LANES)], axis=1)
    lru_y = (h_all * jax.nn.gelu(lru_gate)).astype(jnp.bfloat16)
    mixed = _bdot(lru_y, w_lru_ref[...])
    gates = pf_ref[:, P_G:P_G + D_MODEL] + b_merge_ref[:, 0:D_MODEL]
    mixed = _sigmoid(gates) * mixed

    q = pq_ref[...]
    k = pf_ref[:, P_KV:P_KV + KV_W]
    v = pf_ref[:, P_KV + KV_W:P_KV + 2 * KV_W]
    lane_m = lax.broadcasted_iota(jnp.int32, (m_rows, LANES), 1)

    def dup_head(a, kv):
        src = a[:, (kv // 2) * LANES:(kv // 2 + 1) * LANES]
        swapped = pltpu.roll(src, HEAD_DIM, axis=1)
        first, second = (src, swapped) if kv % 2 == 0 else (swapped, src)
        return jnp.where(lane_m < HEAD_DIM, first, second).astype(jnp.bfloat16)

    k_dup = [dup_head(k, kv) for kv in range(N_KV)]
    v_dup = [dup_head(v, kv) for kv in range(N_KV)]

    lane_q = lax.broadcasted_iota(jnp.int32, (CHUNK, LANES), 1)
    lane_o = lax.broadcasted_iota(jnp.int32, (CHUNK, LANES), 1)
    key_lane = lax.broadcasted_iota(jnp.int32, (1, BAND), 1)
    row_grp = lax.broadcasted_iota(jnp.int32, (GROUP * CHUNK, 1), 0) // CHUNK
    for s in range(nseg):
        rows = slice(s * seg_len, (s + 1) * seg_len)
        for kv in range(N_KV):
            kbuf_ref[s, HIST:HIST + seg_len, kv * LANES:(kv + 1) * LANES] = k_dup[kv][rows]
            vbuf_ref[s, HIST:HIST + seg_len, kv * V_STRIDE:kv * V_STRIDE + LANES] = v_dup[kv][rows]
        k_out_ref[s] = k[s * seg_len + seg_len - keep:(s + 1) * seg_len]
        v_out_ref[s] = v[s * seg_len + seg_len - keep:(s + 1) * seg_len]
        for c in range(n_chunks):
            q_c = q[s * seg_len + c * CHUNK:s * seg_len + (c + 1) * CHUNK]
            slabs = []
            for kv in range(N_KV):
                parts = []
                for g in range(GROUP):
                    col = kv * GROUP * HEAD_DIM + (g // 2) * LANES
                    slab = q_c[:, col:col + LANES]
                    keep_lo = (g % 2) == 0
                    sel = (lane_q < HEAD_DIM) if keep_lo else (lane_q >= HEAD_DIM)
                    parts.append(jnp.where(sel, slab, jnp.zeros_like(slab)))
                q_stack = jnp.concatenate(parts, axis=0)
                k_band = kbuf_ref[s, c * CHUNK:c * CHUNK + BAND, kv * LANES:(kv + 1) * LANES]
                v_band = vbuf_ref[s, c * CHUNK:c * CHUNK + BAND, kv * V_STRIDE:(kv + 1) * V_STRIDE]
                sc = lax.dot_general(q_stack, k_band, (((1,), (1,)), ((), ())),
                                     preferred_element_type=jnp.float32)
                sc = sc + bias_ref[kv]
                if mask_history:
                    first_valid = HIST - (step * n_chunks + c) * CHUNK
                    sc = jnp.where(key_lane >= first_valid, sc, NEG_INF)
                sink = jnp.zeros((GROUP * CHUNK, 1), jnp.float32)
                for g in range(GROUP):
                    sink = jnp.where(row_grp == g, sink_ref[kv * GROUP + g] * LOG2E, sink)
                m = jnp.maximum(jnp.max(sc, axis=-1, keepdims=True), sink)
                p = jnp.exp2(sc - m).astype(jnp.bfloat16)
                o = _bdot(p, v_band)
                denom = o[:, LANES:2 * LANES] + jnp.exp2(sink - m)
                o = o[:, 0:LANES] / denom
                for pair in range(GROUP // 2):
                    lo = o[(2 * pair) * CHUNK:(2 * pair + 1) * CHUNK]
                    hi = o[(2 * pair + 1) * CHUNK:(2 * pair + 2) * CHUNK]
                    slabs.append(jnp.where(lane_o < HEAD_DIM, lo, hi))
            attn_ref[s * seg_len + c * CHUNK:s * seg_len + (c + 1) * CHUNK, :] = (
                jnp.concatenate(slabs, axis=1).astype(jnp.bfloat16))
        kbuf_ref[s, 0:HIST, :] = kbuf_ref[s, seg_len:seg_len + HIST, :]
        vbuf_ref[s, 0:HIST, :] = vbuf_ref[s, seg_len:seg_len + HIST, :]

    attn = _bdot(attn_ref[...], w_attn_ref[...])
    gates = pf_ref[:, P_G + D_MODEL:P_G + 2 * D_MODEL] + b_merge_ref[:, D_MODEL:2 * D_MODEL]
    mixed = (mixed + _sigmoid(gates) * attn).astype(jnp.bfloat16)
    x1 = x + _bdot(mixed, w_out_ref[...])
    x1_ref[...] = x1

    xn2 = _rmsnorm(x1, norm_ffn_ref[...])
    xn2_ref[...] = _pack_bf16_pairs(xn2) if sparse_out else xn2.astype(jnp.bfloat16)
    logits = _bdot(xn2.astype(jnp.bfloat16), w_rt_ref[...]) + b_rt_ref[...]
    lane = lax.broadcasted_iota(jnp.int32, (m_rows, ROUTER_W), 1).astype(jnp.float32)
    far = jnp.float32(2 * ROUTER_W)
    is_group = (lane >= N_EXPERTS) & (lane < N_EXPERTS + N_GROUPS)
    gl = jnp.where(is_group, logits, NEG_INF)
    g_max = jnp.max(gl, axis=-1, keepdims=True)
    g_idx = jnp.min(jnp.where(gl == g_max, lane, far), axis=-1, keepdims=True) - N_EXPERTS
    g_w = 1.0 / jnp.sum(jnp.where(is_group, jnp.exp(gl - g_max), 0.0), axis=-1, keepdims=True)
    in_group = (lane < N_EXPERTS) & (jnp.floor(lane * (1.0 / EXPERTS_PER_GROUP)) == g_idx)
    el = jnp.where(in_group, logits, NEG_INF)
    e1 = jnp.max(el, axis=-1, keepdims=True)
    i1 = jnp.min(jnp.where(el == e1, lane, far), axis=-1, keepdims=True)
    el2 = jnp.where(lane == i1, NEG_INF, el)
    e2 = jnp.max(el2, axis=-1, keepdims=True)
    i2 = jnp.min(jnp.where(el2 == e2, lane, far), axis=-1, keepdims=True)
    t = jnp.exp(e2 - e1)
    w1 = g_w / (1.0 + t)
    w2 = w1 * t
    if sparse_out:
        e_lane = lane + g_idx * EXPERTS_PER_GROUP
        rec = jnp.where(e_lane == i1, w1, 0.0) + jnp.where(e_lane == i2, w2, 0.0)
        rec = jnp.where(lane < EXPERTS_PER_GROUP, rec, 0.0)
        comb_ref[...] = jnp.where(lane == GID_LANE, g_idx, rec)
    else:
        comb_ref[...] = jnp.where(lane == i1, w1, 0.0) + jnp.where(lane == i2, w2, 0.0)


def _const_spec(shape):
    zeros = (0,) * len(shape)
    return pl.BlockSpec(shape, lambda i: zeros, pipeline_mode=pl.Buffered(1))


def _mixer(x, conv0, h0, k0, v0, weights, *, nseg, seg_len, mask_history, sparse_out):
    n_tok = x.shape[0]
    m_rows = nseg * seg_len
    n_steps = n_tok // m_rows
    keep = min(seg_len, HIST)
    if n_steps == 1:
        row_spec = lambda w: pl.BlockSpec((m_rows, w), lambda i: (i, 0), pipeline_mode=pl.Buffered(1))
    else:
        row_spec = lambda w: pl.BlockSpec((m_rows, w), lambda i: (i, 0))
    in_specs = [row_spec(D_MODEL), _const_spec(conv0.shape), _const_spec(h0.shape), _const_spec(k0.shape),
                _const_spec(v0.shape)]
    for w in weights:
        if w.ndim == 1:
            in_specs.append(pl.BlockSpec(memory_space=pltpu.SMEM))
        else:
            in_specs.append(_const_spec(w.shape))
    state_spec = lambda r, w: pl.BlockSpec((nseg, r, w), lambda i: (0, 0, 0))
    out_shape = (
        jax.ShapeDtypeStruct((n_tok, D_MODEL), jnp.float32),
        (jax.ShapeDtypeStruct((n_tok, PACK_W), jnp.uint32) if sparse_out
         else jax.ShapeDtypeStruct((n_tok, D_MODEL), jnp.bfloat16)),
        jax.ShapeDtypeStruct((n_tok, ROUTER_W), jnp.float32),
        jax.ShapeDtypeStruct((nseg, SUBLANES, D_RNN), jnp.float32),
        jax.ShapeDtypeStruct((nseg, SUBLANES, D_RNN), jnp.float32),
        jax.ShapeDtypeStruct((nseg, keep, KV_W), jnp.float32),
        jax.ShapeDtypeStruct((nseg, keep, KV_W), jnp.float32),
    )
    out_specs = (row_spec(D_MODEL), row_spec(PACK_W if sparse_out else D_MODEL), row_spec(ROUTER_W),
                 state_spec(SUBLANES, D_RNN), state_spec(SUBLANES, D_RNN),
                 state_spec(keep, KV_W), state_spec(keep, KV_W))
    scratch = [
        pltpu.VMEM((nseg, SUBLANES, D_RNN), jnp.float32),
        pltpu.VMEM((nseg, SUBLANES, D_RNN), jnp.float32),
        pltpu.VMEM((nseg, HIST + seg_len, KV2_W), jnp.bfloat16),
        pltpu.VMEM((nseg, HIST + seg_len, N_KV * V_STRIDE), jnp.bfloat16),
        pltpu.VMEM((D_RNN // LANES, m_rows, LANES), jnp.float32),
        pltpu.VMEM((D_RNN // LANES, m_rows, LANES), jnp.float32),
        pltpu.VMEM((m_rows, Q_W), jnp.bfloat16),
        pltpu.VMEM((m_rows, D_MODEL), jnp.bfloat16),
        pltpu.VMEM((m_rows, P_W), jnp.float32),
        pltpu.VMEM((m_rows, Q_W), jnp.bfloat16),
    ]
    return pl.pallas_call(
        functools.partial(_mixer_kernel, nseg, seg_len, mask_history, sparse_out),
        grid=(n_steps,),
        in_specs=in_specs,
        out_specs=out_specs,
        out_shape=out_shape,
        scratch_shapes=scratch,
        compiler_params=pltpu.CompilerParams(dimension_semantics=("arbitrary",), vmem_limit_bytes=VMEM_LIMIT),
        name="mixer_prompt" if mask_history else "mixer_sample",
    )(x, conv0, h0, k0, v0, *weights)


def _group_swiglu(x, cols, wg_ref, wu_ref, wd_ref):
    parts = []
    for e in range(EXPERTS_PER_GROUP):
        h = jax.nn.silu(_bdot(x, wg_ref[e])) * _bdot(x, wu_ref[e])
        parts.append((h * cols[e]).astype(jnp.bfloat16))
    return _bdot(jnp.concatenate(parts, axis=1), wd_ref[...])


def _moe_groups_kernel(xn2_ref, comb_ref, x1_ref, wg_ref, wu_ref, wd_ref, norm_ref, y_ref, acc_ref):
    g = pl.program_id(1)

    @pl.when(g == 0)
    def _():
        acc_ref[...] = jnp.zeros_like(acc_ref)

    comb = comb_ref[...]
    lane = lax.broadcasted_iota(jnp.int32, comb.shape, 1)
    cols = [jnp.sum(jnp.where(lane == g * EXPERTS_PER_GROUP + e, comb, 0.0), axis=-1, keepdims=True)
            for e in range(EXPERTS_PER_GROUP)]
    acc_ref[...] += _group_swiglu(xn2_ref[...], cols, wg_ref, wu_ref, wd_ref)

    @pl.when(g == N_GROUPS - 1)
    def _():
        y_ref[...] = _rmsnorm(x1_ref[...] + acc_ref[...], norm_ref[...])


def _group_weight_specs(group_of):
    return [pl.BlockSpec((EXPERTS_PER_GROUP, D_MODEL, D_EXPERT), lambda *a: (group_of(*a), 0, 0)),
            pl.BlockSpec((EXPERTS_PER_GROUP, D_MODEL, D_EXPERT), lambda *a: (group_of(*a), 0, 0)),
            pl.BlockSpec((None, GROUP_FF, D_MODEL), lambda *a: (group_of(*a), 0, 0))]


def _moe_groups(xn2, comb, x1, wg, wu, wd, norm_final, *, tile):
    n_tok = xn2.shape[0]
    row_spec = lambda w: pl.BlockSpec((tile, w), lambda i, g: (i, 0))
    return pl.pallas_call(
        _moe_groups_kernel,
        grid=(n_tok // tile, N_GROUPS),
        in_specs=[row_spec(D_MODEL), row_spec(ROUTER_W), row_spec(D_MODEL),
                  *_group_weight_specs(lambda i, g: g),
                  pl.BlockSpec((1, D_MODEL), lambda i, g: (0, 0))],
        out_specs=row_spec(D_MODEL),
        out_shape=jax.ShapeDtypeStruct((n_tok, D_MODEL), jnp.float32),
        scratch_shapes=[pltpu.VMEM((tile, D_MODEL), jnp.float32)],
        compiler_params=pltpu.CompilerParams(dimension_semantics=("arbitrary", "arbitrary"),
                                             vmem_limit_bytes=VMEM_LIMIT),
        name="moe_groups",
    )(xn2, comb, x1, wg, wu, wd, norm_final)


def _scatter_rows_kernel(tile, pos_ref, x_ref, rec_ref, xs_ref, recs_ref):
    i = pl.program_id(0)

    @pl.when(i == 0)
    def _():
        xs_ref[...] = jnp.zeros_like(xs_ref)
        recs_ref[...] = jnp.zeros_like(recs_ref)

    def body(j, carry):
        r0 = pl.multiple_of(j * SUBLANES, SUBLANES)
        xb = x_ref[pl.ds(r0, SUBLANES), :]
        rb = rec_ref[pl.ds(r0, SUBLANES), :]
        for k in range(SUBLANES):
            p = pos_ref[i * tile + j * SUBLANES + k]
            xs_ref[pl.ds(p, 1), :] = xb[k:k + 1, :]
            recs_ref[pl.ds(p, 1), :] = rb[k:k + 1, :]
        return carry

    lax.fori_loop(0, tile // SUBLANES, body, 0)


def _scatter_rows(pos, xn2p, rec, n_slots, *, tile):
    n_tok = xn2p.shape[0]
    return pl.pallas_call(
        functools.partial(_scatter_rows_kernel, tile),
        grid_spec=pltpu.PrefetchScalarGridSpec(
            num_scalar_prefetch=1,
            grid=(n_tok // tile,),
            in_specs=[pl.BlockSpec((tile, PACK_W), lambda i, pos: (i, 0)),
                      pl.BlockSpec((tile, ROUTER_W), lambda i, pos: (i, 0))],
            out_specs=[pl.BlockSpec((n_slots, PACK_W), lambda i, pos: (0, 0)),
                       pl.BlockSpec((n_slots, ROUTER_W), lambda i, pos: (0, 0))],
        ),
        out_shape=(jax.ShapeDtypeStruct((n_slots, PACK_W), jnp.uint32),
                   jax.ShapeDtypeStruct((n_slots, ROUTER_W), jnp.float32)),
        compiler_params=pltpu.CompilerParams(dimension_semantics=("arbitrary",), vmem_limit_bytes=VMEM_LIMIT),
        name="moe_scatter",
    )(pos, xn2p, rec)


def _moe_sorted_kernel(tg_ref, xs_ref, recs_ref, wg_ref, wu_ref, wd_ref, o_ref):
    x = _unpack_bf16_pairs(xs_ref[...]).astype(jnp.bfloat16)
    rec = recs_ref[...]
    cols = [rec[:, e:e + 1] for e in range(EXPERTS_PER_GROUP)]
    o_ref[...] = _pack_bf16_pairs(_group_swiglu(x, cols, wg_ref, wu_ref, wd_ref))


def _moe_sorted(tile_group, xs, recs, wg, wu, wd, *, tile):
    n_slots = xs.shape[0]
    return pl.pallas_call(
        _moe_sorted_kernel,
        grid_spec=pltpu.PrefetchScalarGridSpec(
            num_scalar_prefetch=1,
            grid=(n_slots // tile,),
            in_specs=[pl.BlockSpec((tile, PACK_W), lambda i, tg: (i, 0)),
                      pl.BlockSpec((tile, ROUTER_W), lambda i, tg: (i, 0)),
                      *_group_weight_specs(lambda i, tg: tg[i])],
            out_specs=pl.BlockSpec((tile, PACK_W), lambda i, tg: (i, 0)),
        ),
        out_shape=jax.ShapeDtypeStruct((n_slots, PACK_W), jnp.uint32),
        compiler_params=pltpu.CompilerParams(dimension_semantics=("arbitrary",), vmem_limit_bytes=VMEM_LIMIT),
        name="moe_sorted",
    )(tile_group, xs, recs, wg, wu, wd)


def _gather_norm_kernel(tile, pos_ref, os_ref, x1_ref, norm_ref, y_ref, buf_ref):
    i = pl.program_id(0)

    def body(j, carry):
        r0 = pl.multiple_of(j * SUBLANES, SUBLANES)
        rows = [os_ref[pl.ds(pos_ref[i * tile + j * SUBLANES + k], 1), :] for k in range(SUBLANES)]
        buf_ref[pl.ds(r0, SUBLANES), :] = jnp.concatenate(rows, axis=0)
        return carry

    lax.fori_loop(0, tile // SUBLANES, body, 0)
    y_ref[...] = _rmsnorm(x1_ref[...] + _unpack_bf16_pairs(buf_ref[...]), norm_ref[...])


def _gather_norm(pos, o_sorted, x1, norm_final, *, tile):
    n_tok = x1.shape[0]
    n_slots = o_sorted.shape[0]
    return pl.pallas_call(
        functools.partial(_gather_norm_kernel, tile),
        grid_spec=pltpu.PrefetchScalarGridSpec(
            num_scalar_prefetch=1,
            grid=(n_tok // tile,),
            in_specs=[pl.BlockSpec((n_slots, PACK_W), lambda i, pos: (0, 0), pipeline_mode=pl.Buffered(1)),
                      pl.BlockSpec((tile, D_MODEL), lambda i, pos: (i, 0)),
                      pl.BlockSpec((1, D_MODEL), lambda i, pos: (0, 0))],
            out_specs=pl.BlockSpec((tile, D_MODEL), lambda i, pos: (i, 0)),
            scratch_shapes=[pltpu.VMEM((tile, PACK_W), jnp.uint32)],
        ),
        out_shape=jax.ShapeDtypeStruct((n_tok, D_MODEL), jnp.float32),
        compiler_params=pltpu.CompilerParams(dimension_semantics=("arbitrary",), vmem_limit_bytes=VMEM_LIMIT),
        name="moe_gather_norm",
    )(pos, o_sorted, x1, norm_final)


def _sorted_slots(rec, tile):
    n_tok = rec.shape[0]
    gid = rec[:, GID_LANE].astype(jnp.int32)
    onehot = (gid[:, None] == jnp.arange(N_GROUPS, dtype=jnp.int32)[None, :]).astype(jnp.int32)
    csum = jnp.cumsum(onehot, axis=0)
    rank = jnp.sum((csum - onehot) * onehot, axis=1)
    padded = ((csum[-1] + tile - 1) // tile) * tile
    end = jnp.cumsum(padded)
    pos = jnp.sum(onehot * (end - padded)[None, :], axis=1) + rank
    n_tiles = n_tok // tile + N_GROUPS
    tile_start = jnp.arange(n_tiles, dtype=jnp.int32) * tile
    tile_group = jnp.minimum(jnp.sum((tile_start[:, None] >= end[None, :]).astype(jnp.int32), axis=1), N_GROUPS - 1)
    return pos.astype(jnp.int32), tile_group.astype(jnp.int32), n_tiles * tile


def _rel_bucket(rel):
    nb = REL_BUCKETS // 2
    n = -rel
    ret = jnp.where(n < 0, nb, 0)
    n = jnp.abs(n)
    max_exact = nb // 2
    nf = jnp.maximum(n, 1).astype(jnp.float32)
    large = max_exact + (jnp.log(nf / max_exact) / math.log(REL_MAX_DIST / max_exact) * (nb - max_exact)).astype(jnp.int32)
    large = jnp.minimum(large, nb - 1)
    return ret + jnp.where(n < max_exact, n, large)


def _band_bias(rel_table):
    lo = -(HIST + CHUNK - 1)
    rels = jnp.arange(lo, CHUNK, dtype=jnp.int32)
    by_rel = jnp.transpose(rel_table[_rel_bucket(rels)]).astype(jnp.float32)
    rows = [by_rel[:, (CHUNK - 1 - qi):(CHUNK - 1 - qi) + BAND] for qi in range(CHUNK)]
    return jnp.stack(rows, axis=1).reshape(N_KV, GROUP * CHUNK, BAND)


def _dup_heads(a):
    a = a.reshape(a.shape[:-1] + (N_KV, HEAD_DIM))
    return jnp.concatenate([a, a], axis=-1).reshape(a.shape[:-2] + (KV2_W,))


def _block_diag_group(w, j):
    per = RG_GROUP // LRU_BLOCK_W
    blk = w[j * per:(j + 1) * per]
    eye = jnp.eye(per, dtype=w.dtype)
    return jnp.einsum('nde,nm->ndme', blk, eye).reshape(RG_GROUP, RG_GROUP)


def kernel(x_prompt, x_sample, state_lru_h, state_lru_conv, cache_swa_k, cache_swa_v, norm_mix, w_in, b_merge, conv_w, conv_b, w_rg_a, b_rg_a, w_rg_x, b_rg_x, lru_lambda, attn_sink, rel_bias, w_lru_proj, w_attn_proj, w_out, norm_ffn, w_group, b_group, w_router, b_router, w_e_gate, w_e_up, w_e_down, norm_final):
    f32, bf16 = jnp.float32, jnp.bfloat16
    l = 0
    w = w_in[l]
    w_rg = jnp.stack([
        jnp.concatenate([_block_diag_group(w_rg_a[l], j), _block_diag_group(w_rg_x[l], j)], axis=1)
        for j in range(N_RG_GROUPS)]).astype(bf16)
    w_rt = jnp.concatenate([w_router[l], w_group[l],
                            jnp.zeros((D_MODEL, ROUTER_W - N_EXPERTS - N_GROUPS), f32)], axis=1).astype(bf16)
    b_rt = jnp.concatenate([b_router[l], b_group[l], jnp.zeros((ROUTER_W - N_EXPERTS - N_GROUPS,), f32)])[None, :]
    weights = (
        norm_mix[l][None, :], w.astype(bf16), b_merge[l][None, :], conv_w[l], conv_b[l][None, :], w_rg,
        b_rg_a[l][None, :], b_rg_x[l][None, :], lru_lambda[l][None, :], attn_sink[l], _band_bias(rel_bias) * LOG2E,
        w_lru_proj[l].astype(bf16), w_attn_proj[l].astype(bf16), w_out[l].astype(bf16), norm_ffn[l][None, :],
        w_rt, b_rt,
    )
    wg, wu = w_e_gate[l].astype(bf16), w_e_up[l].astype(bf16)
    wd = w_e_down[l].astype(bf16).reshape(N_GROUPS, GROUP_FF, D_MODEL)
    nf = norm_final[None, :]

    bp, tp, _ = x_prompt.shape
    assert bp == 1
    xp = x_prompt.reshape(tp, D_MODEL)
    zeros = lambda *s: jnp.zeros(s, f32)
    p_x1, p_xn2, p_comb, p_conv, p_h, p_k, p_v = _mixer(
        xp, zeros(1, SUBLANES, D_RNN), zeros(1, SUBLANES, D_RNN),
        jnp.zeros((1, HIST, KV2_W), bf16), jnp.zeros((1, HIST, KV2_W), bf16), weights,
        nseg=1, seg_len=PROMPT_TILE, mask_history=True, sparse_out=True)
    pos, tile_group, n_slots = _sorted_slots(p_comb, SORT_TILE)
    x_sorted, rec_sorted = _scatter_rows(pos, p_xn2, p_comb, n_slots, tile=SORT_TILE)
    o_sorted = _moe_sorted(tile_group, x_sorted, rec_sorted, wg, wu, wd, tile=SORT_TILE)
    y_p = _gather_norm(pos, o_sorted, p_x1, nf, tile=SORT_TILE).reshape(x_prompt.shape)

    bs, ts, _ = x_sample.shape
    xs = x_sample.reshape(bs * ts, D_MODEL)
    conv0 = jnp.pad(state_lru_conv[l], ((0, 0), (SUBLANES - (CONV_W - 1), 0), (0, 0)))
    h0 = jnp.broadcast_to(state_lru_h[l][:, None, :], (bs, SUBLANES, D_RNN))
    ck = cache_swa_k[l].reshape(bs, -1, KV_W)
    cv = cache_swa_v[l].reshape(bs, -1, KV_W)
    s_x1, s_xn2, s_comb, s_conv, s_h, s_k, s_v = _mixer(
        xs, conv0, h0, _dup_heads(ck).astype(bf16), _dup_heads(cv).astype(bf16), weights,
        nseg=bs, seg_len=ts, mask_history=False, sparse_out=False)
    y_s = _moe_groups(s_xn2, s_comb, s_x1, wg, wu, wd, nf, tile=bs * ts).reshape(x_sample.shape)

    tail = CONV_W - 1
    p_lru_h = p_h[:, 0, :][None]
    p_lru_conv = p_conv[:, SUBLANES - tail:, :][None]
    heads = lambda a: a.reshape(a.shape[:-1] + (N_KV, HEAD_DIM))
    p_swa_k = heads(p_k)[None]
    p_swa_v = heads(p_v)[None]
    s_lru_h = s_h[:, 0, :][None]
    s_lru_conv = s_conv[:, SUBLANES - tail:, :][None]
    win = ck.shape[1]
    s_swa_k = jnp.concatenate([cache_swa_k[l][:, ts:], heads(s_k)], axis=1)[:, -win:][None]
    s_swa_v = jnp.concatenate([cache_swa_v[l][:, ts:], heads(s_v)], axis=1)[:, -win:][None]
    return (y_p, y_s, p_lru_h, p_lru_conv, p_swa_k, p_swa_v, s_lru_h, s_lru_conv, s_swa_k, s_swa_v)
```

```python
import functools
import math

import jax
import jax.numpy as jnp
from jax import lax
from jax.experimental import pallas as pl
from jax.experimental.pallas import tpu as pltpu

D_MODEL = 1024
CHUNK = 64
D_RNN = D_MODEL
CONV_W = 4
LRU_BLOCKS = 16
LRU_BLOCK_W = D_RNN // LRU_BLOCKS
LRU_C = 8.0
N_HEADS = 16
N_KV = 4
HEAD_DIM = 64
GROUP = N_HEADS // N_KV
WINDOW = 128
LOOKBACK_CHUNKS = -(-WINDOW // CHUNK)
HIST = LOOKBACK_CHUNKS * CHUNK
BAND = HIST + CHUNK
Q_W = N_HEADS * HEAD_DIM
KV_W = N_KV * HEAD_DIM
REL_BUCKETS = 32
REL_MAX_DIST = 128
N_GROUPS = 4
EXPERTS_PER_GROUP = 8
N_EXPERTS = N_GROUPS * EXPERTS_PER_GROUP
D_EXPERT = 256
EPS = 1e-6
NEG_INF = -1e30

LANES = 128
SUBLANES = 8
KV2_W = N_KV * LANES
RG_GROUP = 256
N_RG_GROUPS = D_RNN // RG_GROUP
ROUTER_W = LANES
VMEM_LIMIT = 60 * 1024 * 1024
PACK_W = D_MODEL // 2
GROUP_FF = EXPERTS_PER_GROUP * D_EXPERT
GID_LANE = EXPERTS_PER_GROUP
SORT_TILE = 512
PROMPT_TILE = 512

C_LX, C_LG, C_Q = 0, D_RNN, 2 * D_RNN
C_K = C_Q + Q_W
C_V = C_K + KV_W
C_G = C_V + KV_W
P_LG, P_G = 0, D_RNN
P_KV = P_G + 2 * D_MODEL
P_W = P_KV + 2 * KV_W
PROJ_BLOCK = 256
V_STRIDE = 2 * LANES
SCAN_ROWS = SUBLANES * SUBLANES
LOG2E = math.log2(math.e)
REL_SPAN = BAND + CHUNK


def _rmsnorm(x, g):
    return x * lax.rsqrt(jnp.mean(x * x, axis=-1, keepdims=True) + EPS) * g


def _bdot(a, b):
    return jnp.dot(a, b, preferred_element_type=jnp.float32)


def _sigmoid(x):
    return 0.5 * jnp.tanh(0.5 * x) + 0.5


def _pack_bf16_pairs(x):
    return pltpu.pack_elementwise([x[:, :PACK_W], x[:, PACK_W:]], packed_dtype=jnp.bfloat16)


def _unpack_bf16_pairs(p):
    lo = pltpu.unpack_elementwise(p, index=0, packed_dtype=jnp.bfloat16, unpacked_dtype=jnp.float32)
    hi = pltpu.unpack_elementwise(p, index=1, packed_dtype=jnp.bfloat16, unpacked_dtype=jnp.float32)
    return jnp.concatenate([lo, hi], axis=1)


def _mixer_kernel(nseg, seg_len, mask_history, sparse_out,
                  x_ref, conv0_ref, h0_ref, k0_ref, v0_ref,
                  norm_mix_ref, w_in_ref, b_merge_ref, conv_w_ref, conv_b_ref, w_rg_ref, b_rg_a_ref, b_rg_x_ref,
                  lam_ref, sink_ref, byrel_ref, w_lru_ref, w_attn_ref, w_out_ref, norm_ffn_ref, w_rt_ref, b_rt_ref,
                  x1_ref, xn2_ref, comb_ref, conv_out_ref, h_out_ref, k_out_ref, v_out_ref,
                  xp_ref, hc_ref, kbuf_ref, vbuf_ref, a_ref, b_ref, attn_ref, xn_ref, pf_ref, pq_ref, bias_ref):
    step = pl.program_id(0)
    m_rows = nseg * seg_len
    n_chunks = seg_len // CHUNK
    keep = min(seg_len, HIST)

    @pl.when(step == 0)
    def _():
        xp_ref[...] = conv0_ref[...]
        hc_ref[...] = h0_ref[...]
        kbuf_ref[:, 0:HIST, :] = k0_ref[...]
        vbuf_ref[:, :, :] = jnp.ones(vbuf_ref.shape, jnp.bfloat16)
        for kv in range(N_KV):
            vbuf_ref[:, 0:HIST, kv * V_STRIDE:kv * V_STRIDE + LANES] = v0_ref[:, :, kv * LANES:(kv + 1) * LANES]
        for h in range(N_HEADS):
            tiled = jnp.broadcast_to(byrel_ref[h:h + 1, :], (CHUNK, REL_SPAN))
            block = pltpu.roll(tiled, REL_SPAN - (CHUNK - 1), axis=1, stride=1, stride_axis=0)
            bias_ref[h // GROUP, (h % GROUP) * CHUNK:(h % GROUP + 1) * CHUNK, :] = block[:, 0:BAND]

    x = x_ref[...]
    xn_ref[...] = _rmsnorm(x, norm_mix_ref[...]).astype(jnp.bfloat16)

    tasks = []

    def f32_block(src_ref, c_src, c_dst):
        def run():
            pf_ref[:, c_dst:c_dst + PROJ_BLOCK] = _bdot(xn_ref[...], src_ref[:, c_src:c_src + PROJ_BLOCK])
        return run

    def q_block(c):
        def run():
            pq_ref[:, c:c + PROJ_BLOCK] = (_bdot(xn_ref[...], w_in_ref[:, C_Q + c:C_Q + c + PROJ_BLOCK])
                                           * (HEAD_DIM ** -0.5 * LOG2E)).astype(jnp.bfloat16)
        return run

    for c in range(0, D_RNN, PROJ_BLOCK):
        tasks.append(f32_block(w_in_ref, C_LG + c, P_LG + c))
    for c in range(0, 2 * D_MODEL, PROJ_BLOCK):
        tasks.append(f32_block(w_in_ref, C_G + c, P_G + c))
    for c in range(0, 2 * KV_W, PROJ_BLOCK):
        tasks.append(f32_block(w_in_ref, C_K + c, P_KV + c))
    for c in range(0, Q_W, PROJ_BLOCK):
        tasks.append(q_block(c))

    def pump(n):
        for _ in range(min(n, len(tasks))):
            tasks.pop(0)()

    lru_x = _bdot(xn_ref[...], w_in_ref[:, C_LX:C_LX + D_RNN])
    conv_w = conv_w_ref[...]
    row = lax.broadcasted_iota(jnp.int32, (SUBLANES, D_RNN), 0)
    xc_parts = []
    for s in range(nseg):
        xs = lru_x[s * seg_len:(s + 1) * seg_len, :]
        prev = xp_ref[s]
        tail = xs[seg_len - SUBLANES:, :]
        acc = conv_b_ref[...] + xs * conv_w[CONV_W - 1:CONV_W, :]
        pump(1)
        for k in range(1, CONV_W):
            shifted = pltpu.roll(xs, k, axis=0)
            first = jnp.where(row < k, pltpu.roll(prev, k, axis=0), shifted[0:SUBLANES])
            shifted = jnp.concatenate([first, shifted[SUBLANES:]], axis=0)
            acc = acc + shifted * conv_w[CONV_W - 1 - k:CONV_W - k, :]
            pump(1)
        xc_parts.append(acc)
        conv_out_ref[s] = tail[SUBLANES - (CONV_W - 1):, :]
        xp_ref[s] = tail
    xc = xc_parts[0] if nseg == 1 else jnp.concatenate(xc_parts, axis=0)
    xc_b = xc.astype(jnp.bfloat16)

    lam = lam_ref[...]
    log_sig = jnp.minimum(lam, 0.0) - jnp.log1p(jnp.exp(-jnp.abs(lam)))
    c8 = LRU_C * log_sig
    for j in range(N_RG_GROUPS):
        cs = slice(j * RG_GROUP, (j + 1) * RG_GROUP)
        pre = _bdot(xc_b[:, cs], w_rg_ref[j])
        r = _sigmoid(pre[:, :RG_GROUP] + b_rg_a_ref[:, cs])
        i = _sigmoid(pre[:, RG_GROUP:] + b_rg_x_ref[:, cs])
        log_a = c8[:, cs] * r
        a_val = jnp.exp(log_a)
        th = jnp.tanh(log_a)
        b_val = jnp.sqrt(-2.0 * th / (1.0 - th)) * i * xc[:, cs]
        for t in range(RG_GROUP // LANES):
            lt = j * (RG_GROUP // LANES) + t
            a_ref[lt] = a_val[:, t * LANES:(t + 1) * LANES]
            b_ref[lt] = b_val[:, t * LANES:(t + 1) * LANES]
        pump(2)

    row_c = lax.broadcasted_iota(jnp.int32, (SUBLANES, LANES), 0)
    for s in range(nseg):
        for lt in range(D_RNN // LANES):
            cs = slice(lt * LANES, (lt + 1) * LANES)
            carry = hc_ref[s, :, cs]
            for blk in range(seg_len // SCAN_ROWS):
                r0 = s * seg_len + blk * SCAN_ROWS
                slab = lambda ref, i: ref[lt, pl.ds(r0 + i, SUBLANES, stride=SUBLANES), :]
                a_loc, b_loc = [slab(a_ref, 0)], [slab(b_ref, 0)]
                for i in range(1, SUBLANES):
                    a_i = slab(a_ref, i)
                    b_loc.append(a_i * b_loc[-1] + slab(b_ref, i))
                    a_loc.append(a_i * a_loc[-1])
                a_e, b_e = a_loc[-1], b_loc[-1]
                for d in (1, 2, 4):
                    m = row_c >= d
                    b_e = jnp.where(m, a_e * pltpu.roll(b_e, d, axis=0) + b_e, b_e)
                    a_e = jnp.where(m, a_e * pltpu.roll(a_e, d, axis=0), a_e)
                h_end = a_e * carry + b_e
                c_in = jnp.where(row_c == 0, carry, pltpu.roll(h_end, 1, axis=0))
                for i in range(SUBLANES):
                    b_ref[lt, pl.ds(r0 + i, SUBLANES, stride=SUBLANES), :] = a_loc[i] * c_in + b_loc[i]
                carry = jnp.broadcast_to(h_end[SUBLANES - 1:SUBLANES, :], (SUBLANES, LANES))
            hc_ref[s, :, cs] = carry
            h_out_ref[s:s + 1, cs] = carry[0:1, :]
            pump(1)
    pump(len(tasks))

    lru_gate = pf_ref[:, P_LG:P_LG + D_RNN]
    h_all = jnp.concatenate([b_ref[lt] for lt in range(D_RNN // LANES)], axis=1)
    lru_y = (h_all * jax.nn.gelu(lru_gate)).astype(jnp.bfloat16)
    mixed = _bdot(lru_y, w_lru_ref[...])
    gates = pf_ref[:, P_G:P_G + D_MODEL] + b_merge_ref[:, 0:D_MODEL]
    mixed = _sigmoid(gates) * mixed

    q = pq_ref[...]
    k = pf_ref[:, P_KV:P_KV + KV_W]
    v = pf_ref[:, P_KV + KV_W:P_KV + 2 * KV_W]
    lane_m = lax.broadcasted_iota(jnp.int32, (m_rows, LANES), 1)

    def dup_head(a, kv):
        src = a[:, (kv // 2) * LANES:(kv // 2 + 1) * LANES]
        swapped = pltpu.roll(src, HEAD_DIM, axis=1)
        first, second = (src, swapped) if kv % 2 == 0 else (swapped, src)
        return jnp.where(lane_m < HEAD_DIM, first, second).astype(jnp.bfloat16)

    k_dup = [dup_head(k, kv) for kv in range(N_KV)]
    v_dup = [dup_head(v, kv) for kv in range(N_KV)]

    lane_q = lax.broadcasted_iota(jnp.int32, (CHUNK, LANES), 1)
    lane_o = lax.broadcasted_iota(jnp.int32, (CHUNK, LANES), 1)
    key_lane = lax.broadcasted_iota(jnp.int32, (1, BAND), 1)
    row_grp = lax.broadcasted_iota(jnp.int32, (GROUP * CHUNK, 1), 0) // CHUNK
    for s in range(nseg):
        rows = slice(s * seg_len, (s + 1) * seg_len)
        for kv in range(N_KV):
            kbuf_ref[s, HIST:HIST + seg_len, kv * LANES:(kv + 1) * LANES] = k_dup[kv][rows]
            vbuf_ref[s, HIST:HIST + seg_len, kv * V_STRIDE:kv * V_STRIDE + LANES] = v_dup[kv][rows]
        k_out_ref[s] = k[s * seg_len + seg_len - keep:(s + 1) * seg_len]
        v_out_ref[s] = v[s * seg_len + seg_len - keep:(s + 1) * seg_len]
        for c in range(n_chunks):
            q_c = q[s * seg_len + c * CHUNK:s * seg_len + (c + 1) * CHUNK]
            slabs = []
            for kv in range(N_KV):
                parts = []
                for g in range(GROUP):
                    col = kv * GROUP * HEAD_DIM + (g // 2) * LANES
                    slab = q_c[:, col:col + LANES]
                    keep_lo = (g % 2) == 0
                    sel = (lane_q < HEAD_DIM) if keep_lo else (lane_q >= HEAD_DIM)
                    parts.append(jnp.where(sel, slab, jnp.zeros_like(slab)))
                q_stack = jnp.concatenate(parts, axis=0)
                k_band = kbuf_ref[s, c * CHUNK:c * CHUNK + BAND, kv * LANES:(kv + 1) * LANES]
                v_band = vbuf_ref[s, c * CHUNK:c * CHUNK + BAND, kv * V_STRIDE:(kv + 1) * V_STRIDE]
                sc = lax.dot_general(q_stack, k_band, (((1,), (1,)), ((), ())),
                                     preferred_element_type=jnp.float32)
                sc = sc + bias_ref[kv]
                if mask_history:
                    first_valid = HIST - (step * n_chunks + c) * CHUNK
                    sc = jnp.where(key_lane >= first_valid, sc, NEG_INF)
                sink = jnp.zeros((GROUP * CHUNK, 1), jnp.float32)
                for g in range(GROUP):
                    sink = jnp.where(row_grp == g, sink_ref[kv * GROUP + g] * LOG2E, sink)
                m = jnp.maximum(jnp.max(sc, axis=-1, keepdims=True), sink)
                p = jnp.exp2(sc - m).astype(jnp.bfloat16)
                o = _bdot(p, v_band)
                denom = o[:, LANES:2 * LANES] + jnp.exp2(sink - m)
                o = o[:, 0:LANES] / denom
                for pair in range(GROUP // 2):
                    lo = o[(2 * pair) * CHUNK:(2 * pair + 1) * CHUNK]
                    hi = o[(2 * pair + 1) * CHUNK:(2 * pair + 2) * CHUNK]
                    slabs.append(jnp.where(lane_o < HEAD_DIM, lo, hi))
            attn_ref[s * seg_len + c * CHUNK:s * seg_len + (c + 1) * CHUNK, :] = (
                jnp.concatenate(slabs, axis=1).astype(jnp.bfloat16))
        kbuf_ref[s, 0:HIST, :] = kbuf_ref[s, seg_len:seg_len + HIST, :]
        vbuf_ref[s, 0:HIST, :] = vbuf_ref[s, seg_len:seg_len + HIST, :]

    attn = _bdot(attn_ref[...], w_attn_ref[...])
    gates = pf_ref[:, P_G + D_MODEL:P_G + 2 * D_MODEL] + b_merge_ref[:, D_MODEL:2 * D_MODEL]
    mixed = (mixed + _sigmoid(gates) * attn).astype(jnp.bfloat16)
    x1 = x + _bdot(mixed, w_out_ref[...])
    x1_ref[...] = x1

    xn2 = _rmsnorm(x1, norm_ffn_ref[...])
    xn2_ref[...] = _pack_bf16_pairs(xn2) if sparse_out else xn2.astype(jnp.bfloat16)
    logits = _bdot(xn2.astype(jnp.bfloat16), w_rt_ref[...]) + b_rt_ref[...]
    lane = lax.broadcasted_iota(jnp.int32, (m_rows, ROUTER_W), 1).astype(jnp.float32)
    far = jnp.float32(2 * ROUTER_W)
    is_group = (lane >= N_EXPERTS) & (lane < N_EXPERTS + N_GROUPS)
    gl = jnp.where(is_group, logits, NEG_INF)
    g_max = jnp.max(gl, axis=-1, keepdims=True)
    g_idx = jnp.min(jnp.where(gl == g_max, lane, far), axis=-1, keepdims=True) - N_EXPERTS
    g_w = 1.0 / jnp.sum(jnp.where(is_group, jnp.exp(gl - g_max), 0.0), axis=-1, keepdims=True)
    in_group = (lane < N_EXPERTS) & (jnp.floor(lane * (1.0 / EXPERTS_PER_GROUP)) == g_idx)
    el = jnp.where(in_group, logits, NEG_INF)
    e1 = jnp.max(el, axis=-1, keepdims=True)
    i1 = jnp.min(jnp.where(el == e1, lane, far), axis=-1, keepdims=True)
    el2 = jnp.where(lane == i1, NEG_INF, el)
    e2 = jnp.max(el2, axis=-1, keepdims=True)
    i2 = jnp.min(jnp.where(el2 == e2, lane, far), axis=-1, keepdims=True)
    t = jnp.exp(e2 - e1)
    w1 = g_w / (1.0 + t)
    w2 = w1 * t
    if sparse_out:
        e_lane = lane + g_idx * EXPERTS_PER_GROUP
        rec = jnp.where(e_lane == i1, w1, 0.0) + jnp.where(e_lane == i2, w2, 0.0)
        rec = jnp.where(lane < EXPERTS_PER_GROUP, rec, 0.0)
        comb_ref[...] = jnp.where(lane == GID_LANE, g_idx, rec)
    else:
        comb_ref[...] = jnp.where(lane == i1, w1, 0.0) + jnp.where(lane == i2, w2, 0.0)


def _const_spec(shape):
    zeros = (0,) * len(shape)
    return pl.BlockSpec(shape, lambda i: zeros, pipeline_mode=pl.Buffered(1))


def _mixer(x, conv0, h0, k0, v0, weights, *, nseg, seg_len, mask_history, sparse_out):
    n_tok = x.shape[0]
    m_rows = nseg * seg_len
    n_steps = n_tok // m_rows
    keep = min(seg_len, HIST)
    if n_steps == 1:
        row_spec = lambda w: pl.BlockSpec((m_rows, w), lambda i: (i, 0), pipeline_mode=pl.Buffered(1))
    else:
        row_spec = lambda w: pl.BlockSpec((m_rows, w), lambda i: (i, 0))
    in_specs = [row_spec(D_MODEL), _const_spec(conv0.shape), _const_spec(h0.shape), _const_spec(k0.shape),
                _const_spec(v0.shape)]
    for w in weights:
        if w.ndim == 1:
            in_specs.append(pl.BlockSpec(memory_space=pltpu.SMEM))
        else:
            in_specs.append(_const_spec(w.shape))
    state_spec = lambda r, w: pl.BlockSpec((nseg, r, w), lambda i: (0, 0, 0))
    out_shape = (
        jax.ShapeDtypeStruct((n_tok, D_MODEL), jnp.float32),
        (jax.ShapeDtypeStruct((n_tok, PACK_W), jnp.uint32) if sparse_out
         else jax.ShapeDtypeStruct((n_tok, D_MODEL), jnp.bfloat16)),
        jax.ShapeDtypeStruct((n_tok, ROUTER_W), jnp.float32),
        jax.ShapeDtypeStruct((nseg, CONV_W - 1, D_RNN), jnp.float32),
        jax.ShapeDtypeStruct((nseg, D_RNN), jnp.float32),
        jax.ShapeDtypeStruct((nseg, keep, KV_W), jnp.float32),
        jax.ShapeDtypeStruct((nseg, keep, KV_W), jnp.float32),
    )
    out_specs = (row_spec(D_MODEL), row_spec(PACK_W if sparse_out else D_MODEL), row_spec(ROUTER_W),
                 state_spec(CONV_W - 1, D_RNN), pl.BlockSpec((nseg, D_RNN), lambda i: (0, 0)),
                 state_spec(keep, KV_W), state_spec(keep, KV_W))
    scratch = [
        pltpu.VMEM((nseg, SUBLANES, D_RNN), jnp.float32),
        pltpu.VMEM((nseg, SUBLANES, D_RNN), jnp.float32),
        pltpu.VMEM((nseg, HIST + seg_len, KV2_W), jnp.bfloat16),
        pltpu.VMEM((nseg, HIST + seg_len, N_KV * V_STRIDE), jnp.bfloat16),
        pltpu.VMEM((D_RNN // LANES, m_rows, LANES), jnp.float32),
        pltpu.VMEM((D_RNN // LANES, m_rows, LANES), jnp.float32),
        pltpu.VMEM((m_rows, Q_W), jnp.bfloat16),
        pltpu.VMEM((m_rows, D_MODEL), jnp.bfloat16),
        pltpu.VMEM((m_rows, P_W), jnp.float32),
        pltpu.VMEM((m_rows, Q_W), jnp.bfloat16),
        pltpu.VMEM((N_KV, GROUP * CHUNK, BAND), jnp.float32),
    ]
    return pl.pallas_call(
        functools.partial(_mixer_kernel, nseg, seg_len, mask_history, sparse_out),
        grid=(n_steps,),
        in_specs=in_specs,
        out_specs=out_specs,
        out_shape=out_shape,
        scratch_shapes=scratch,
        compiler_params=pltpu.CompilerParams(dimension_semantics=("arbitrary",), vmem_limit_bytes=VMEM_LIMIT),
        name="mixer_prompt" if mask_history else "mixer_sample",
    )(x, conv0, h0, k0, v0, *weights)


def _group_swiglu(x, cols, wg_ref, wu_ref, wd_ref):
    parts = []
    for e in range(EXPERTS_PER_GROUP):
        h = jax.nn.silu(_bdot(x, wg_ref[e])) * _bdot(x, wu_ref[e])
        parts.append((h * cols[e]).astype(jnp.bfloat16))
    return _bdot(jnp.concatenate(parts, axis=1), wd_ref[...])


def _moe_groups_kernel(xn2_ref, comb_ref, x1_ref, wg_ref, wu_ref, wd_ref, norm_ref, y_ref, acc_ref):
    g = pl.program_id(1)

    @pl.when(g == 0)
    def _():
        acc_ref[...] = jnp.zeros_like(acc_ref)

    comb = comb_ref[...]
    lane = lax.broadcasted_iota(jnp.int32, comb.shape, 1)
    cols = [jnp.sum(jnp.where(lane == g * EXPERTS_PER_GROUP + e, comb, 0.0), axis=-1, keepdims=True)
            for e in range(EXPERTS_PER_GROUP)]
    acc_ref[...] += _group_swiglu(xn2_ref[...], cols, wg_ref, wu_ref, wd_ref)

    @pl.when(g == N_GROUPS - 1)
    def _():
        y_ref[...] = _rmsnorm(x1_ref[...] + acc_ref[...], norm_ref[...])


def _group_weight_specs(group_of):
    return [pl.BlockSpec((EXPERTS_PER_GROUP, D_MODEL, D_EXPERT), lambda *a: (group_of(*a), 0, 0)),
            pl.BlockSpec((EXPERTS_PER_GROUP, D_MODEL, D_EXPERT), lambda *a: (group_of(*a), 0, 0)),
            pl.BlockSpec((None, GROUP_FF, D_MODEL), lambda *a: (group_of(*a), 0, 0))]


def _moe_groups(xn2, comb, x1, wg, wu, wd, norm_final, *, tile):
    n_tok = xn2.shape[0]
    row_spec = lambda w: pl.BlockSpec((tile, w), lambda i, g: (i, 0))
    return pl.pallas_call(
        _moe_groups_kernel,
        grid=(n_tok // tile, N_GROUPS),
        in_specs=[row_spec(D_MODEL), row_spec(ROUTER_W), row_spec(D_MODEL),
                  *_group_weight_specs(lambda i, g: g),
                  pl.BlockSpec((1, D_MODEL), lambda i, g: (0, 0))],
        out_specs=row_spec(D_MODEL),
        out_shape=jax.ShapeDtypeStruct((n_tok, D_MODEL), jnp.float32),
        scratch_shapes=[pltpu.VMEM((tile, D_MODEL), jnp.float32)],
        compiler_params=pltpu.CompilerParams(dimension_semantics=("arbitrary", "arbitrary"),
                                             vmem_limit_bytes=VMEM_LIMIT),
        name="moe_groups",
    )(xn2, comb, x1, wg, wu, wd, norm_final)


def _scatter_rows_kernel(tile, pos_ref, x_ref, rec_ref, xs_ref, recs_ref):
    i = pl.program_id(0)

    @pl.when(i == 0)
    def _():
        xs_ref[...] = jnp.zeros_like(xs_ref)
        recs_ref[...] = jnp.zeros_like(recs_ref)

    def body(j, carry):
        r0 = pl.multiple_of(j * SUBLANES, SUBLANES)
        xb = x_ref[pl.ds(r0, SUBLANES), :]
        rb = rec_ref[pl.ds(r0, SUBLANES), :]
        for k in range(SUBLANES):
            p = pos_ref[i * tile + j * SUBLANES + k]
            xs_ref[pl.ds(p, 1), :] = xb[k:k + 1, :]
            recs_ref[pl.ds(p, 1), :] = rb[k:k + 1, :]
        return carry

    lax.fori_loop(0, tile // SUBLANES, body, 0)


def _scatter_rows(pos, xn2p, rec, n_slots, *, tile):
    n_tok = xn2p.shape[0]
    return pl.pallas_call(
        functools.partial(_scatter_rows_kernel, tile),
        grid_spec=pltpu.PrefetchScalarGridSpec(
            num_scalar_prefetch=1,
            grid=(n_tok // tile,),
            in_specs=[pl.BlockSpec((tile, PACK_W), lambda i, pos: (i, 0)),
                      pl.BlockSpec((tile, ROUTER_W), lambda i, pos: (i, 0))],
            out_specs=[pl.BlockSpec((n_slots, PACK_W), lambda i, pos: (0, 0)),
                       pl.BlockSpec((n_slots, ROUTER_W), lambda i, pos: (0, 0))],
        ),
        out_shape=(jax.ShapeDtypeStruct((n_slots, PACK_W), jnp.uint32),
                   jax.ShapeDtypeStruct((n_slots, ROUTER_W), jnp.float32)),
        compiler_params=pltpu.CompilerParams(dimension_semantics=("arbitrary",), vmem_limit_bytes=VMEM_LIMIT),
        name="moe_scatter",
    )(pos, xn2p, rec)


def _moe_sorted_kernel(tg_ref, xs_ref, recs_ref, wg_ref, wu_ref, wd_ref, o_ref):
    x = _unpack_bf16_pairs(xs_ref[...]).astype(jnp.bfloat16)
    rec = recs_ref[...]
    cols = [rec[:, e:e + 1] for e in range(EXPERTS_PER_GROUP)]
    o_ref[...] = _pack_bf16_pairs(_group_swiglu(x, cols, wg_ref, wu_ref, wd_ref))


def _moe_sorted(tile_group, xs, recs, wg, wu, wd, *, tile):
    n_slots = xs.shape[0]
    return pl.pallas_call(
        _moe_sorted_kernel,
        grid_spec=pltpu.PrefetchScalarGridSpec(
            num_scalar_prefetch=1,
            grid=(n_slots // tile,),
            in_specs=[pl.BlockSpec((tile, PACK_W), lambda i, tg: (i, 0)),
                      pl.BlockSpec((tile, ROUTER_W), lambda i, tg: (i, 0)),
                      *_group_weight_specs(lambda i, tg: tg[i])],
            out_specs=pl.BlockSpec((tile, PACK_W), lambda i, tg: (i, 0)),
        ),
        out_shape=jax.ShapeDtypeStruct((n_slots, PACK_W), jnp.uint32),
        compiler_params=pltpu.CompilerParams(dimension_semantics=("arbitrary",), vmem_limit_bytes=VMEM_LIMIT),
        name="moe_sorted",
    )(tile_group, xs, recs, wg, wu, wd)


def _gather_norm_kernel(tile, pos_ref, os_ref, x1_ref, norm_ref, y_ref, buf_ref):
    i = pl.program_id(0)

    def body(j, carry):
        r0 = pl.multiple_of(j * SUBLANES, SUBLANES)
        rows = [os_ref[pl.ds(pos_ref[i * tile + j * SUBLANES + k], 1), :] for k in range(SUBLANES)]
        buf_ref[pl.ds(r0, SUBLANES), :] = jnp.concatenate(rows, axis=0)
        return carry

    lax.fori_loop(0, tile // SUBLANES, body, 0)
    y_ref[...] = _rmsnorm(x1_ref[...] + _unpack_bf16_pairs(buf_ref[...]), norm_ref[...])


def _gather_norm(pos, o_sorted, x1, norm_final, *, tile):
    n_tok = x1.shape[0]
    n_slots = o_sorted.shape[0]
    return pl.pallas_call(
        functools.partial(_gather_norm_kernel, tile),
        grid_spec=pltpu.PrefetchScalarGridSpec(
            num_scalar_prefetch=1,
            grid=(n_tok // tile,),
            in_specs=[pl.BlockSpec((n_slots, PACK_W), lambda i, pos: (0, 0), pipeline_mode=pl.Buffered(1)),
                      pl.BlockSpec((tile, D_MODEL), lambda i, pos: (i, 0)),
                      pl.BlockSpec((1, D_MODEL), lambda i, pos: (0, 0))],
            out_specs=pl.BlockSpec((tile, D_MODEL), lambda i, pos: (i, 0)),
            scratch_shapes=[pltpu.VMEM((tile, PACK_W), jnp.uint32)],
        ),
        out_shape=jax.ShapeDtypeStruct((n_tok, D_MODEL), jnp.float32),
        compiler_params=pltpu.CompilerParams(dimension_semantics=("arbitrary",), vmem_limit_bytes=VMEM_LIMIT),
        name="moe_gather_norm",
    )(pos, o_sorted, x1, norm_final)


def _sorted_slots(rec, tile):
    n_tok = rec.shape[0]
    gid = rec[:, GID_LANE].astype(jnp.int32)
    onehot = (gid[:, None] == jnp.arange(N_GROUPS, dtype=jnp.int32)[None, :]).astype(jnp.int32)
    csum = jnp.cumsum(onehot, axis=0)
    rank = jnp.sum((csum - onehot) * onehot, axis=1)
    padded = ((csum[-1] + tile - 1) // tile) * tile
    end = jnp.cumsum(padded)
    pos = jnp.sum(onehot * (end - padded)[None, :], axis=1) + rank
    n_tiles = n_tok // tile + N_GROUPS
    tile_start = jnp.arange(n_tiles, dtype=jnp.int32) * tile
    tile_group = jnp.minimum(jnp.sum((tile_start[:, None] >= end[None, :]).astype(jnp.int32), axis=1), N_GROUPS - 1)
    return pos.astype(jnp.int32), tile_group.astype(jnp.int32), n_tiles * tile


def _rel_bucket(rel):
    nb = REL_BUCKETS // 2
    n = -rel
    ret = jnp.where(n < 0, nb, 0)
    n = jnp.abs(n)
    max_exact = nb // 2
    nf = jnp.maximum(n, 1).astype(jnp.float32)
    large = max_exact + (jnp.log(nf / max_exact) / math.log(REL_MAX_DIST / max_exact) * (nb - max_exact)).astype(jnp.int32)
    large = jnp.minimum(large, nb - 1)
    return ret + jnp.where(n < max_exact, n, large)


def _bias_by_rel(rel_table):
    rels = jnp.arange(-(BAND - 1), CHUNK + 1, dtype=jnp.int32)
    return jnp.transpose(rel_table[_rel_bucket(rels)]).astype(jnp.float32)


def _dup_heads(a):
    a = a.reshape(a.shape[:-1] + (N_KV, HEAD_DIM))
    return jnp.concatenate([a, a], axis=-1).reshape(a.shape[:-2] + (KV2_W,))


def _block_diag_groups(w):
    per = RG_GROUP // LRU_BLOCK_W
    blk = w.reshape(N_RG_GROUPS, per, LRU_BLOCK_W, 1, LRU_BLOCK_W)
    eye = jnp.eye(per, dtype=w.dtype).reshape(1, per, 1, per, 1)
    return (blk * eye).reshape(N_RG_GROUPS, RG_GROUP, RG_GROUP)


def kernel(x_prompt, x_sample, state_lru_h, state_lru_conv, cache_swa_k, cache_swa_v, norm_mix, w_in, b_merge, conv_w, conv_b, w_rg_a, b_rg_a, w_rg_x, b_rg_x, lru_lambda, attn_sink, rel_bias, w_lru_proj, w_attn_proj, w_out, norm_ffn, w_group, b_group, w_router, b_router, w_e_gate, w_e_up, w_e_down, norm_final):
    f32, bf16 = jnp.float32, jnp.bfloat16
    l = 0
    w = w_in[l]
    w_rg = jnp.concatenate([_block_diag_groups(w_rg_a[l]), _block_diag_groups(w_rg_x[l])], axis=2).astype(bf16)
    w_rt = jnp.concatenate([w_router[l], w_group[l],
                            jnp.zeros((D_MODEL, ROUTER_W - N_EXPERTS - N_GROUPS), f32)], axis=1).astype(bf16)
    b_rt = jnp.concatenate([b_router[l], b_group[l], jnp.zeros((ROUTER_W - N_EXPERTS - N_GROUPS,), f32)])[None, :]
    weights = (
        norm_mix[l][None, :], w.astype(bf16), b_merge[l][None, :], conv_w[l], conv_b[l][None, :], w_rg,
        b_rg_a[l][None, :], b_rg_x[l][None, :], lru_lambda[l][None, :], attn_sink[l], _bias_by_rel(rel_bias) * LOG2E,
        w_lru_proj[l].astype(bf16), w_attn_proj[l].astype(bf16), w_out[l].astype(bf16), norm_ffn[l][None, :],
        w_rt, b_rt,
    )
    wg, wu = w_e_gate[l].astype(bf16), w_e_up[l].astype(bf16)
    wd = w_e_down[l].astype(bf16).reshape(N_GROUPS, GROUP_FF, D_MODEL)
    nf = norm_final[None, :]

    bp, tp, _ = x_prompt.shape
    assert bp == 1
    xp = x_prompt.reshape(tp, D_MODEL)
    zeros = lambda *s: jnp.zeros(s, f32)
    p_x1, p_xn2, p_comb, p_conv, p_h, p_k, p_v = _mixer(
        xp, zeros(1, SUBLANES, D_RNN), zeros(1, SUBLANES, D_RNN),
        jnp.zeros((1, HIST, KV2_W), bf16), jnp.zeros((1, HIST, KV2_W), bf16), weights,
        nseg=1, seg_len=PROMPT_TILE, mask_history=True, sparse_out=True)
    pos, tile_group, n_slots = _sorted_slots(p_comb, SORT_TILE)
    x_sorted, rec_sorted = _scatter_rows(pos, p_xn2, p_comb, n_slots, tile=SORT_TILE)
    o_sorted = _moe_sorted(tile_group, x_sorted, rec_sorted, wg, wu, wd, tile=SORT_TILE)
    y_p = _gather_norm(pos, o_sorted, p_x1, nf, tile=SORT_TILE).reshape(x_prompt.shape)

    bs, ts, _ = x_sample.shape
    xs = x_sample.reshape(bs * ts, D_MODEL)
    conv0 = jnp.pad(state_lru_conv[l], ((0, 0), (SUBLANES - (CONV_W - 1), 0), (0, 0)))
    h0 = jnp.broadcast_to(state_lru_h[l][:, None, :], (bs, SUBLANES, D_RNN))
    ck = cache_swa_k[l].reshape(bs, -1, KV_W)
    cv = cache_swa_v[l].reshape(bs, -1, KV_W)
    s_x1, s_xn2, s_comb, s_conv, s_h, s_k, s_v = _mixer(
        xs, conv0, h0, _dup_heads(ck).astype(bf16), _dup_heads(cv).astype(bf16), weights,
        nseg=bs, seg_len=ts, mask_history=False, sparse_out=False)
    y_s = _moe_groups(s_xn2, s_comb, s_x1, wg, wu, wd, nf, tile=bs * ts).reshape(x_sample.shape)

    p_lru_h = p_h[None]
    p_lru_conv = p_conv[None]
    heads = lambda a: a.reshape(a.shape[:-1] + (N_KV, HEAD_DIM))
    p_swa_k = heads(p_k)[None]
    p_swa_v = heads(p_v)[None]
    s_lru_h = s_h[None]
    s_lru_conv = s_conv[None]
    win = ck.shape[1]
    s_swa_k = jnp.concatenate([cache_swa_k[l][:, ts:], heads(s_k)], axis=1)[:, -win:][None]
    s_swa_v = jnp.concatenate([cache_swa_v[l][:, ts:], heads(s_v)], axis=1)[:, -win:][None]
    return (y_p, y_s, p_lru_h, p_lru_conv, p_swa_k, p_swa_v, s_lru_h, s_lru_conv, s_swa_k, s_swa_v)
```

```python
import functools
import math

import jax
import jax.numpy as jnp
from jax import lax
from jax.experimental import pallas as pl
from jax.experimental.pallas import tpu as pltpu

D_MODEL = 1024
CHUNK = 64
D_RNN = D_MODEL
CONV_W = 4
LRU_BLOCKS = 16
LRU_BLOCK_W = D_RNN // LRU_BLOCKS
LRU_C = 8.0
N_HEADS = 16
N_KV = 4
HEAD_DIM = 64
GROUP = N_HEADS // N_KV
WINDOW = 128
LOOKBACK_CHUNKS = -(-WINDOW // CHUNK)
HIST = LOOKBACK_CHUNKS * CHUNK
BAND = HIST + CHUNK
Q_W = N_HEADS * HEAD_DIM
KV_W = N_KV * HEAD_DIM
REL_BUCKETS = 32
REL_MAX_DIST = 128
N_GROUPS = 4
EXPERTS_PER_GROUP = 8
N_EXPERTS = N_GROUPS * EXPERTS_PER_GROUP
D_EXPERT = 256
EPS = 1e-6
NEG_INF = -1e30

LANES = 128
SUBLANES = 8
KV2_W = N_KV * LANES
RG_GROUP = 256
N_RG_GROUPS = D_RNN // RG_GROUP
ROUTER_W = LANES
VMEM_LIMIT = 60 * 1024 * 1024
PACK_W = D_MODEL // 2
GROUP_FF = EXPERTS_PER_GROUP * D_EXPERT
GID_LANE = EXPERTS_PER_GROUP
SORT_TILE = 512
PROMPT_TILE = 512

C_LX, C_LG, C_Q = 0, D_RNN, 2 * D_RNN
C_K = C_Q + Q_W
C_V = C_K + KV_W
C_G = C_V + KV_W
P_LG, P_G = 0, D_RNN
P_KV = P_G + 2 * D_MODEL
P_W = P_KV + 2 * KV_W
PROJ_BLOCK = 256
V_STRIDE = 2 * LANES
SCAN_ROWS = SUBLANES * SUBLANES
LOG2E = math.log2(math.e)
REL_SPAN = BAND + CHUNK


def _rmsnorm(x, g):
    return x * lax.rsqrt(jnp.mean(x * x, axis=-1, keepdims=True) + EPS) * g


def _bdot(a, b):
    return jnp.dot(a, b, preferred_element_type=jnp.float32)


def _sigmoid(x):
    return 0.5 * jnp.tanh(0.5 * x) + 0.5


def _pack_bf16_pairs(x):
    return pltpu.pack_elementwise([x[:, :PACK_W], x[:, PACK_W:]], packed_dtype=jnp.bfloat16)


def _unpack_bf16_pairs(p):
    lo = pltpu.unpack_elementwise(p, index=0, packed_dtype=jnp.bfloat16, unpacked_dtype=jnp.float32)
    hi = pltpu.unpack_elementwise(p, index=1, packed_dtype=jnp.bfloat16, unpacked_dtype=jnp.float32)
    return jnp.concatenate([lo, hi], axis=1)


def _mixer_kernel(nseg, seg_len, mask_history, sparse_out,
                  x_ref, conv0_ref, h0_ref, k0_ref, v0_ref,
                  norm_mix_ref, w_in_ref, b_merge_ref, conv_w_ref, conv_b_ref, w_rg_ref, b_rg_a_ref, b_rg_x_ref,
                  lam_ref, sink_ref, byrel_ref, w_lru_ref, w_attn_ref, w_out_ref, norm_ffn_ref, w_rt_ref, b_rt_ref,
                  x1_ref, xn2_ref, comb_ref, conv_out_ref, h_out_ref, k_out_ref, v_out_ref,
                  xp_ref, hc_ref, kbuf_ref, vbuf_ref, a_ref, b_ref, attn_ref, xn_ref, pf_ref, pq_ref, bias_ref):
    step = pl.program_id(0)
    m_rows = nseg * seg_len
    n_chunks = seg_len // CHUNK
    keep = min(seg_len, HIST)

    @pl.when(step == 0)
    def _():
        xp_ref[...] = conv0_ref[...]
        hc_ref[...] = h0_ref[...]
        kbuf_ref[:, 0:HIST, :] = k0_ref[...]
        vbuf_ref[:, :, :] = jnp.ones(vbuf_ref.shape, jnp.bfloat16)
        for kv in range(N_KV):
            vbuf_ref[:, 0:HIST, kv * V_STRIDE:kv * V_STRIDE + LANES] = v0_ref[:, :, kv * LANES:(kv + 1) * LANES]
        for h in range(N_HEADS):
            tiled = jnp.broadcast_to(byrel_ref[h:h + 1, :], (CHUNK, REL_SPAN))
            block = pltpu.roll(tiled, REL_SPAN - (CHUNK - 1), axis=1, stride=1, stride_axis=0)
            bias_ref[h // GROUP, (h % GROUP) * CHUNK:(h % GROUP + 1) * CHUNK, :] = block[:, 0:BAND]

    x = x_ref[...]
    xn_ref[...] = _rmsnorm(x, norm_mix_ref[...]).astype(jnp.bfloat16)

    tasks = []

    def f32_block(src_ref, c_src, c_dst):
        def run():
            pf_ref[:, c_dst:c_dst + PROJ_BLOCK] = _bdot(xn_ref[...], src_ref[:, c_src:c_src + PROJ_BLOCK])
        return run

    def q_block(c):
        def run():
            pq_ref[:, c:c + PROJ_BLOCK] = (_bdot(xn_ref[...], w_in_ref[:, C_Q + c:C_Q + c + PROJ_BLOCK])
                                           * (HEAD_DIM ** -0.5 * LOG2E)).astype(jnp.bfloat16)
        return run

    for c in range(0, D_RNN, PROJ_BLOCK):
        tasks.append(f32_block(w_in_ref, C_LG + c, P_LG + c))
    for c in range(0, 2 * D_MODEL, PROJ_BLOCK):
        tasks.append(f32_block(w_in_ref, C_G + c, P_G + c))
    for c in range(0, 2 * KV_W, PROJ_BLOCK):
        tasks.append(f32_block(w_in_ref, C_K + c, P_KV + c))
    for c in range(0, Q_W, PROJ_BLOCK):
        tasks.append(q_block(c))

    def pump(n):
        for _ in range(min(n, len(tasks))):
            tasks.pop(0)()

    lru_x = _bdot(xn_ref[...], w_in_ref[:, C_LX:C_LX + D_RNN])
    conv_w = conv_w_ref[...]
    row = lax.broadcasted_iota(jnp.int32, (SUBLANES, D_RNN), 0)
    xc_parts = []
    for s in range(nseg):
        xs = lru_x[s * seg_len:(s + 1) * seg_len, :]
        prev = xp_ref[s]
        tail = xs[seg_len - SUBLANES:, :]
        acc = conv_b_ref[...] + xs * conv_w[CONV_W - 1:CONV_W, :]
        pump(1)
        for k in range(1, CONV_W):
            shifted = pltpu.roll(xs, k, axis=0)
            first = jnp.where(row < k, pltpu.roll(prev, k, axis=0), shifted[0:SUBLANES])
            shifted = jnp.concatenate([first, shifted[SUBLANES:]], axis=0)
            acc = acc + shifted * conv_w[CONV_W - 1 - k:CONV_W - k, :]
            pump(1)
        xc_parts.append(acc)
        conv_out_ref[s] = tail[SUBLANES - (CONV_W - 1):, :]
        xp_ref[s] = tail
    xc = xc_parts[0] if nseg == 1 else jnp.concatenate(xc_parts, axis=0)
    xc_b = xc.astype(jnp.bfloat16)

    lam = lam_ref[...]
    log_sig = jnp.minimum(lam, 0.0) - jnp.log1p(jnp.exp(-jnp.abs(lam)))
    c8 = LRU_C * log_sig
    for j in range(N_RG_GROUPS):
        cs = slice(j * RG_GROUP, (j + 1) * RG_GROUP)
        pre = _bdot(xc_b[:, cs], w_rg_ref[j])
        r = _sigmoid(pre[:, :RG_GROUP] + b_rg_a_ref[:, cs])
        i = _sigmoid(pre[:, RG_GROUP:] + b_rg_x_ref[:, cs])
        log_a = c8[:, cs] * r
        a_val = jnp.exp(log_a)
        th = jnp.tanh(log_a)
        b_val = jnp.sqrt(-2.0 * th / (1.0 - th)) * i * xc[:, cs]
        for t in range(RG_GROUP // LANES):
            lt = j * (RG_GROUP // LANES) + t
            a_ref[lt] = a_val[:, t * LANES:(t + 1) * LANES]
            b_ref[lt] = b_val[:, t * LANES:(t + 1) * LANES]
        pump(2)

    row_c = lax.broadcasted_iota(jnp.int32, (SUBLANES, LANES), 0)
    for s in range(nseg):
        for lt in range(D_RNN // LANES):
            cs = slice(lt * LANES, (lt + 1) * LANES)
            carry = hc_ref[s, :, cs]
            for blk in range(seg_len // SCAN_ROWS):
                r0 = s * seg_len + blk * SCAN_ROWS
                slab = lambda ref, i: ref[lt, pl.ds(r0 + i, SUBLANES, stride=SUBLANES), :]
                a_loc, b_loc = [slab(a_ref, 0)], [slab(b_ref, 0)]
                for i in range(1, SUBLANES):
                    a_i = slab(a_ref, i)
                    b_loc.append(a_i * b_loc[-1] + slab(b_ref, i))
                    a_loc.append(a_i * a_loc[-1])
                a_e, b_e = a_loc[-1], b_loc[-1]
                for d in (1, 2, 4):
                    m = row_c >= d
                    b_e = jnp.where(m, a_e * pltpu.roll(b_e, d, axis=0) + b_e, b_e)
                    a_e = jnp.where(m, a_e * pltpu.roll(a_e, d, axis=0), a_e)
                h_end = a_e * carry + b_e
                c_in = jnp.where(row_c == 0, carry, pltpu.roll(h_end, 1, axis=0))
                for i in range(SUBLANES):
                    b_ref[lt, pl.ds(r0 + i, SUBLANES, stride=SUBLANES), :] = a_loc[i] * c_in + b_loc[i]
                carry = jnp.broadcast_to(h_end[SUBLANES - 1:SUBLANES, :], (SUBLANES, LANES))
            hc_ref[s, :, cs] = carry
            h_out_ref[s:s + 1, cs] = carry[0:1, :]
            pump(1)
    pump(len(tasks))

    lru_gate = pf_ref[:, P_LG:P_LG + D_RNN]
    h_all = jnp.concatenate([b_ref[lt] for lt in range(D_RNN // LANES)], axis=1)
    lru_y = (h_all * jax.nn.gelu(lru_gate)).astype(jnp.bfloat16)
    mixed = _bdot(lru_y, w_lru_ref[...])
    gates = pf_ref[:, P_G:P_G + D_MODEL] + b_merge_ref[:, 0:D_MODEL]
    mixed = _sigmoid(gates) * mixed

    q = pq_ref[...]
    k = pf_ref[:, P_KV:P_KV + KV_W]
    v = pf_ref[:, P_KV + KV_W:P_KV + 2 * KV_W]
    lane_m = lax.broadcasted_iota(jnp.int32, (m_rows, LANES), 1)

    def dup_head(a, kv):
        src = a[:, (kv // 2) * LANES:(kv // 2 + 1) * LANES]
        swapped = pltpu.roll(src, HEAD_DIM, axis=1)
        first, second = (src, swapped) if kv % 2 == 0 else (swapped, src)
        return jnp.where(lane_m < HEAD_DIM, first, second).astype(jnp.bfloat16)

    k_dup = [dup_head(k, kv) for kv in range(N_KV)]
    v_dup = [dup_head(v, kv) for kv in range(N_KV)]

    lane_q = lax.broadcasted_iota(jnp.int32, (CHUNK, LANES), 1)
    lane_o = lax.broadcasted_iota(jnp.int32, (CHUNK, LANES), 1)
    key_lane = lax.broadcasted_iota(jnp.int32, (1, BAND), 1)
    row_grp = lax.broadcasted_iota(jnp.int32, (GROUP * CHUNK, 1), 0) // CHUNK
    for s in range(nseg):
        rows = slice(s * seg_len, (s + 1) * seg_len)
        for kv in range(N_KV):
            kbuf_ref[s, HIST:HIST + seg_len, kv * LANES:(kv + 1) * LANES] = k_dup[kv][rows]
            vbuf_ref[s, HIST:HIST + seg_len, kv * V_STRIDE:kv * V_STRIDE + LANES] = v_dup[kv][rows]
        k_out_ref[s] = k[s * seg_len + seg_len - keep:(s + 1) * seg_len]
        v_out_ref[s] = v[s * seg_len + seg_len - keep:(s + 1) * seg_len]
        for c in range(n_chunks):
            q_c = q[s * seg_len + c * CHUNK:s * seg_len + (c + 1) * CHUNK]
            slabs = []
            for kv in range(N_KV):
                parts = []
                for g in range(GROUP):
                    col = kv * GROUP * HEAD_DIM + (g // 2) * LANES
                    slab = q_c[:, col:col + LANES]
                    keep_lo = (g % 2) == 0
                    sel = (lane_q < HEAD_DIM) if keep_lo else (lane_q >= HEAD_DIM)
                    parts.append(jnp.where(sel, slab, jnp.zeros_like(slab)))
                q_stack = jnp.concatenate(parts, axis=0)
                k_band = kbuf_ref[s, c * CHUNK:c * CHUNK + BAND, kv * LANES:(kv + 1) * LANES]
                v_band = vbuf_ref[s, c * CHUNK:c * CHUNK + BAND, kv * V_STRIDE:(kv + 1) * V_STRIDE]
                sc = lax.dot_general(q_stack, k_band, (((1,), (1,)), ((), ())),
                                     preferred_element_type=jnp.float32)
                sc = sc + bias_ref[kv]
                if mask_history:
                    first_valid = HIST - (step * n_chunks + c) * CHUNK
                    sc = jnp.where(key_lane >= first_valid, sc, NEG_INF)
                sink = jnp.zeros((GROUP * CHUNK, 1), jnp.float32)
                for g in range(GROUP):
                    sink = jnp.where(row_grp == g, sink_ref[kv * GROUP + g] * LOG2E, sink)
                m = jnp.maximum(jnp.max(sc, axis=-1, keepdims=True), sink)
                p = jnp.exp2(sc - m).astype(jnp.bfloat16)
                o = _bdot(p, v_band)
                denom = o[:, LANES:2 * LANES] + jnp.exp2(sink - m)
                o = o[:, 0:LANES] / denom
                for pair in range(GROUP // 2):
                    lo = o[(2 * pair) * CHUNK:(2 * pair + 1) * CHUNK]
                    hi = o[(2 * pair + 1) * CHUNK:(2 * pair + 2) * CHUNK]
                    slabs.append(jnp.where(lane_o < HEAD_DIM, lo, hi))
            attn_ref[s * seg_len + c * CHUNK:s * seg_len + (c + 1) * CHUNK, :] = (
                jnp.concatenate(slabs, axis=1).astype(jnp.bfloat16))
        kbuf_ref[s, 0:HIST, :] = kbuf_ref[s, seg_len:seg_len + HIST, :]
        vbuf_ref[s, 0:HIST, :] = vbuf_ref[s, seg_len:seg_len + HIST, :]

    attn = _bdot(attn_ref[...], w_attn_ref[...])
    gates = pf_ref[:, P_G + D_MODEL:P_G + 2 * D_MODEL] + b_merge_ref[:, D_MODEL:2 * D_MODEL]
    mixed = (mixed + _sigmoid(gates) * attn).astype(jnp.bfloat16)
    x1 = x + _bdot(mixed, w_out_ref[...])
    x1_ref[...] = x1

    xn2 = _rmsnorm(x1, norm_ffn_ref[...])
    xn2_ref[...] = _pack_bf16_pairs(xn2) if sparse_out else xn2.astype(jnp.bfloat16)
    logits = _bdot(xn2.astype(jnp.bfloat16), w_rt_ref[...]) + b_rt_ref[...]
    lane = lax.broadcasted_iota(jnp.int32, (m_rows, ROUTER_W), 1).astype(jnp.float32)
    far = jnp.float32(2 * ROUTER_W)
    is_group = (lane >= N_EXPERTS) & (lane < N_EXPERTS + N_GROUPS)
    gl = jnp.where(is_group, logits, NEG_INF)
    g_max = jnp.max(gl, axis=-1, keepdims=True)
    g_idx = jnp.min(jnp.where(gl == g_max, lane, far), axis=-1, keepdims=True) - N_EXPERTS
    g_w = 1.0 / jnp.sum(jnp.where(is_group, jnp.exp(gl - g_max), 0.0), axis=-1, keepdims=True)
    in_group = (lane < N_EXPERTS) & (jnp.floor(lane * (1.0 / EXPERTS_PER_GROUP)) == g_idx)
    el = jnp.where(in_group, logits, NEG_INF)
    e1 = jnp.max(el, axis=-1, keepdims=True)
    i1 = jnp.min(jnp.where(el == e1, lane, far), axis=-1, keepdims=True)
    el2 = jnp.where(lane == i1, NEG_INF, el)
    e2 = jnp.max(el2, axis=-1, keepdims=True)
    i2 = jnp.min(jnp.where(el2 == e2, lane, far), axis=-1, keepdims=True)
    t = jnp.exp(e2 - e1)
    w1 = g_w / (1.0 + t)
    w2 = w1 * t
    if sparse_out:
        e_lane = lane + g_idx * EXPERTS_PER_GROUP
        rec = jnp.where(e_lane == i1, w1, 0.0) + jnp.where(e_lane == i2, w2, 0.0)
        rec = jnp.where(lane < EXPERTS_PER_GROUP, rec, 0.0)
        comb_ref[...] = jnp.where(lane == GID_LANE, g_idx, rec)
    else:
        comb_ref[...] = jnp.where(lane == i1, w1, 0.0) + jnp.where(lane == i2, w2, 0.0)


def _const_spec(shape):
    zeros = (0,) * len(shape)
    return pl.BlockSpec(shape, lambda i: zeros, pipeline_mode=pl.Buffered(1))


def _mixer(x, conv0, h0, k0, v0, weights, *, nseg, seg_len, mask_history, sparse_out):
    n_tok = x.shape[0]
    m_rows = nseg * seg_len
    n_steps = n_tok // m_rows
    keep = min(seg_len, HIST)
    if n_steps == 1:
        row_spec = lambda w: pl.BlockSpec((m_rows, w), lambda i: (i, 0), pipeline_mode=pl.Buffered(1))
    else:
        row_spec = lambda w: pl.BlockSpec((m_rows, w), lambda i: (i, 0))
    in_specs = [row_spec(D_MODEL), _const_spec(conv0.shape), _const_spec(h0.shape), _const_spec(k0.shape),
                _const_spec(v0.shape)]
    for w in weights:
        if w.ndim == 1:
            in_specs.append(pl.BlockSpec(memory_space=pltpu.SMEM))
        else:
            in_specs.append(_const_spec(w.shape))
    state_spec = lambda r, w: pl.BlockSpec((nseg, r, w), lambda i: (0, 0, 0))
    out_shape = (
        jax.ShapeDtypeStruct((n_tok, D_MODEL), jnp.float32),
        (jax.ShapeDtypeStruct((n_tok, PACK_W), jnp.uint32) if sparse_out
         else jax.ShapeDtypeStruct((n_tok, D_MODEL), jnp.bfloat16)),
        jax.ShapeDtypeStruct((n_tok, ROUTER_W), jnp.float32),
        jax.ShapeDtypeStruct((nseg, CONV_W - 1, D_RNN), jnp.float32),
        jax.ShapeDtypeStruct((nseg, D_RNN), jnp.float32),
        jax.ShapeDtypeStruct((nseg, keep, KV_W), jnp.float32),
        jax.ShapeDtypeStruct((nseg, keep, KV_W), jnp.float32),
    )
    out_specs = (row_spec(D_MODEL), row_spec(PACK_W if sparse_out else D_MODEL), row_spec(ROUTER_W),
                 state_spec(CONV_W - 1, D_RNN), pl.BlockSpec((nseg, D_RNN), lambda i: (0, 0)),
                 state_spec(keep, KV_W), state_spec(keep, KV_W))
    scratch = [
        pltpu.VMEM((nseg, SUBLANES, D_RNN), jnp.float32),
        pltpu.VMEM((nseg, SUBLANES, D_RNN), jnp.float32),
        pltpu.VMEM((nseg, HIST + seg_len, KV2_W), jnp.bfloat16),
        pltpu.VMEM((nseg, HIST + seg_len, N_KV * V_STRIDE), jnp.bfloat16),
        pltpu.VMEM((D_RNN // LANES, m_rows, LANES), jnp.float32),
        pltpu.VMEM((D_RNN // LANES, m_rows, LANES), jnp.float32),
        pltpu.VMEM((m_rows, Q_W), jnp.bfloat16),
        pltpu.VMEM((m_rows, D_MODEL), jnp.bfloat16),
        pltpu.VMEM((m_rows, P_W), jnp.float32),
        pltpu.VMEM((m_rows, Q_W), jnp.bfloat16),
        pltpu.VMEM((N_KV, GROUP * CHUNK, BAND), jnp.float32),
    ]
    return pl.pallas_call(
        functools.partial(_mixer_kernel, nseg, seg_len, mask_history, sparse_out),
        grid=(n_steps,),
        in_specs=in_specs,
        out_specs=out_specs,
        out_shape=out_shape,
        scratch_shapes=scratch,
        compiler_params=pltpu.CompilerParams(dimension_semantics=("arbitrary",), vmem_limit_bytes=VMEM_LIMIT),
        name="mixer_prompt" if mask_history else "mixer_sample",
    )(x, conv0, h0, k0, v0, *weights)


def _group_swiglu(x, cols, wg_ref, wu_ref, wd_ref):
    parts = []
    for e in range(EXPERTS_PER_GROUP):
        h = jax.nn.silu(_bdot(x, wg_ref[e])) * _bdot(x, wu_ref[e])
        parts.append((h * cols[e]).astype(jnp.bfloat16))
    return _bdot(jnp.concatenate(parts, axis=1), wd_ref[...])


def _moe_groups_kernel(xn2_ref, comb_ref, x1_ref, wg_ref, wu_ref, wd_ref, norm_ref, y_ref, acc_ref):
    g = pl.program_id(1)

    @pl.when(g == 0)
    def _():
        acc_ref[...] = jnp.zeros_like(acc_ref)

    comb = comb_ref[...]
    lane = lax.broadcasted_iota(jnp.int32, comb.shape, 1)
    cols = [jnp.sum(jnp.where(lane == g * EXPERTS_PER_GROUP + e, comb, 0.0), axis=-1, keepdims=True)
            for e in range(EXPERTS_PER_GROUP)]
    acc_ref[...] += _group_swiglu(xn2_ref[...], cols, wg_ref, wu_ref, wd_ref)

    @pl.when(g == N_GROUPS - 1)
    def _():
        y_ref[...] = _rmsnorm(x1_ref[...] + acc_ref[...], norm_ref[...])


def _group_weight_specs(group_of):
    return [pl.BlockSpec((EXPERTS_PER_GROUP, D_MODEL, D_EXPERT), lambda *a: (group_of(*a), 0, 0)),
            pl.BlockSpec((EXPERTS_PER_GROUP, D_MODEL, D_EXPERT), lambda *a: (group_of(*a), 0, 0)),
            pl.BlockSpec((None, GROUP_FF, D_MODEL), lambda *a: (group_of(*a), 0, 0))]


def _moe_groups(xn2, comb, x1, wg, wu, wd, norm_final, *, tile):
    n_tok = xn2.shape[0]
    row_spec = lambda w: pl.BlockSpec((tile, w), lambda i, g: (i, 0))
    return pl.pallas_call(
        _moe_groups_kernel,
        grid=(n_tok // tile, N_GROUPS),
        in_specs=[row_spec(D_MODEL), row_spec(ROUTER_W), row_spec(D_MODEL),
                  *_group_weight_specs(lambda i, g: g),
                  pl.BlockSpec((1, D_MODEL), lambda i, g: (0, 0))],
        out_specs=row_spec(D_MODEL),
        out_shape=jax.ShapeDtypeStruct((n_tok, D_MODEL), jnp.float32),
        scratch_shapes=[pltpu.VMEM((tile, D_MODEL), jnp.float32)],
        compiler_params=pltpu.CompilerParams(dimension_semantics=("arbitrary", "arbitrary"),
                                             vmem_limit_bytes=VMEM_LIMIT),
        name="moe_groups",
    )(xn2, comb, x1, wg, wu, wd, norm_final)


def _scatter_rows_kernel(tile, pos_ref, x_ref, rec_ref, wg_ref, wu_ref, wd_ref,
                         xs_ref, recs_ref, wg_out_ref, wu_out_ref, wd_out_ref):
    i = pl.program_id(0)

    @pl.when(i == 0)
    def _():
        xs_ref[...] = jnp.zeros_like(xs_ref)
        recs_ref[...] = jnp.zeros_like(recs_ref)

    wg_out_ref[...] = wg_ref[...].astype(jnp.bfloat16)
    wu_out_ref[...] = wu_ref[...].astype(jnp.bfloat16)
    wd_out_ref[...] = wd_ref[...].astype(jnp.bfloat16)

    def body(j, carry):
        r0 = pl.multiple_of(j * SUBLANES, SUBLANES)
        xb = x_ref[pl.ds(r0, SUBLANES), :]
        rb = rec_ref[pl.ds(r0, SUBLANES), :]
        for k in range(SUBLANES):
            p = pos_ref[i * tile + j * SUBLANES + k]
            xs_ref[pl.ds(p, 1), :] = xb[k:k + 1, :]
            recs_ref[pl.ds(p, 1), :] = rb[k:k + 1, :]
        return carry

    lax.fori_loop(0, tile // SUBLANES, body, 0)


def _scatter_rows(pos, xn2p, rec, w_gate, w_up, w_down, n_slots, *, tile):
    n_tok = xn2p.shape[0]
    n_steps = n_tok // tile
    per_step = N_EXPERTS // n_steps
    assert per_step * n_steps == N_EXPERTS
    w_spec = lambda a: pl.BlockSpec((per_step,) + a.shape[1:], lambda i, pos: (i, 0, 0))
    return pl.pallas_call(
        functools.partial(_scatter_rows_kernel, tile),
        grid_spec=pltpu.PrefetchScalarGridSpec(
            num_scalar_prefetch=1,
            grid=(n_steps,),
            in_specs=[pl.BlockSpec((tile, PACK_W), lambda i, pos: (i, 0)),
                      pl.BlockSpec((tile, ROUTER_W), lambda i, pos: (i, 0)),
                      w_spec(w_gate), w_spec(w_up), w_spec(w_down)],
            out_specs=[pl.BlockSpec((n_slots, PACK_W), lambda i, pos: (0, 0)),
                       pl.BlockSpec((n_slots, ROUTER_W), lambda i, pos: (0, 0)),
                       w_spec(w_gate), w_spec(w_up), w_spec(w_down)],
        ),
        out_shape=(jax.ShapeDtypeStruct((n_slots, PACK_W), jnp.uint32),
                   jax.ShapeDtypeStruct((n_slots, ROUTER_W), jnp.float32),
                   jax.ShapeDtypeStruct(w_gate.shape, jnp.bfloat16),
                   jax.ShapeDtypeStruct(w_up.shape, jnp.bfloat16),
                   jax.ShapeDtypeStruct(w_down.shape, jnp.bfloat16)),
        compiler_params=pltpu.CompilerParams(dimension_semantics=("arbitrary",), vmem_limit_bytes=VMEM_LIMIT),
        name="moe_scatter",
    )(pos, xn2p, rec, w_gate, w_up, w_down)


def _moe_sorted_kernel(tg_ref, xs_ref, recs_ref, wg_ref, wu_ref, wd_ref, o_ref):
    x = _unpack_bf16_pairs(xs_ref[...]).astype(jnp.bfloat16)
    rec = recs_ref[...]
    cols = [rec[:, e:e + 1] for e in range(EXPERTS_PER_GROUP)]
    o_ref[...] = _pack_bf16_pairs(_group_swiglu(x, cols, wg_ref, wu_ref, wd_ref))


def _moe_sorted(tile_group, xs, recs, wg, wu, wd, *, tile):
    n_slots = xs.shape[0]
    return pl.pallas_call(
        _moe_sorted_kernel,
        grid_spec=pltpu.PrefetchScalarGridSpec(
            num_scalar_prefetch=1,
            grid=(n_slots // tile,),
            in_specs=[pl.BlockSpec((tile, PACK_W), lambda i, tg: (i, 0)),
                      pl.BlockSpec((tile, ROUTER_W), lambda i, tg: (i, 0)),
                      *_group_weight_specs(lambda i, tg: tg[i])],
            out_specs=pl.BlockSpec((tile, PACK_W), lambda i, tg: (i, 0)),
        ),
        out_shape=jax.ShapeDtypeStruct((n_slots, PACK_W), jnp.uint32),
        compiler_params=pltpu.CompilerParams(dimension_semantics=("arbitrary",), vmem_limit_bytes=VMEM_LIMIT),
        name="moe_sorted",
    )(tile_group, xs, recs, wg, wu, wd)


def _gather_norm_kernel(tile, pos_ref, os_ref, x1_ref, norm_ref, y_ref, buf_ref):
    i = pl.program_id(0)

    def body(j, carry):
        r0 = pl.multiple_of(j * SUBLANES, SUBLANES)
        rows = [os_ref[pl.ds(pos_ref[i * tile + j * SUBLANES + k], 1), :] for k in range(SUBLANES)]
        buf_ref[pl.ds(r0, SUBLANES), :] = jnp.concatenate(rows, axis=0)
        return carry

    lax.fori_loop(0, tile // SUBLANES, body, 0)
    y_ref[...] = _rmsnorm(x1_ref[...] + _unpack_bf16_pairs(buf_ref[...]), norm_ref[...])


def _gather_norm(pos, o_sorted, x1, norm_final, *, tile):
    n_tok = x1.shape[0]
    n_slots = o_sorted.shape[0]
    return pl.pallas_call(
        functools.partial(_gather_norm_kernel, tile),
        grid_spec=pltpu.PrefetchScalarGridSpec(
            num_scalar_prefetch=1,
            grid=(n_tok // tile,),
            in_specs=[pl.BlockSpec((n_slots, PACK_W), lambda i, pos: (0, 0), pipeline_mode=pl.Buffered(1)),
                      pl.BlockSpec((tile, D_MODEL), lambda i, pos: (i, 0)),
                      pl.BlockSpec((1, D_MODEL), lambda i, pos: (0, 0))],
            out_specs=pl.BlockSpec((tile, D_MODEL), lambda i, pos: (i, 0)),
            scratch_shapes=[pltpu.VMEM((tile, PACK_W), jnp.uint32)],
        ),
        out_shape=jax.ShapeDtypeStruct((n_tok, D_MODEL), jnp.float32),
        compiler_params=pltpu.CompilerParams(dimension_semantics=("arbitrary",), vmem_limit_bytes=VMEM_LIMIT),
        name="moe_gather_norm",
    )(pos, o_sorted, x1, norm_final)


def _sorted_slots(rec, tile):
    n_tok = rec.shape[0]
    gid = rec[:, GID_LANE].astype(jnp.int32)
    onehot = (gid[:, None] == jnp.arange(N_GROUPS, dtype=jnp.int32)[None, :]).astype(jnp.int32)
    csum = jnp.cumsum(onehot, axis=0)
    rank = jnp.sum((csum - onehot) * onehot, axis=1)
    padded = ((csum[-1] + tile - 1) // tile) * tile
    end = jnp.cumsum(padded)
    pos = jnp.sum(onehot * (end - padded)[None, :], axis=1) + rank
    n_tiles = n_tok // tile + N_GROUPS
    tile_start = jnp.arange(n_tiles, dtype=jnp.int32) * tile
    tile_group = jnp.minimum(jnp.sum((tile_start[:, None] >= end[None, :]).astype(jnp.int32), axis=1), N_GROUPS - 1)
    return pos.astype(jnp.int32), tile_group.astype(jnp.int32), n_tiles * tile


def _rel_bucket(rel):
    nb = REL_BUCKETS // 2
    n = -rel
    ret = jnp.where(n < 0, nb, 0)
    n = jnp.abs(n)
    max_exact = nb // 2
    nf = jnp.maximum(n, 1).astype(jnp.float32)
    large = max_exact + (jnp.log(nf / max_exact) / math.log(REL_MAX_DIST / max_exact) * (nb - max_exact)).astype(jnp.int32)
    large = jnp.minimum(large, nb - 1)
    return ret + jnp.where(n < max_exact, n, large)


def _bias_by_rel(rel_table):
    rels = jnp.arange(-(BAND - 1), CHUNK + 1, dtype=jnp.int32)
    return jnp.transpose(rel_table[_rel_bucket(rels)]).astype(jnp.float32)


def _dup_heads(a):
    a = a.reshape(a.shape[:-1] + (N_KV, HEAD_DIM))
    return jnp.concatenate([a, a], axis=-1).reshape(a.shape[:-2] + (KV2_W,))


def _block_diag_groups(w):
    per = RG_GROUP // LRU_BLOCK_W
    blk = w.reshape(N_RG_GROUPS, per, LRU_BLOCK_W, 1, LRU_BLOCK_W)
    eye = jnp.eye(per, dtype=w.dtype).reshape(1, per, 1, per, 1)
    return (blk * eye).reshape(N_RG_GROUPS, RG_GROUP, RG_GROUP)


def kernel(x_prompt, x_sample, state_lru_h, state_lru_conv, cache_swa_k, cache_swa_v, norm_mix, w_in, b_merge, conv_w, conv_b, w_rg_a, b_rg_a, w_rg_x, b_rg_x, lru_lambda, attn_sink, rel_bias, w_lru_proj, w_attn_proj, w_out, norm_ffn, w_group, b_group, w_router, b_router, w_e_gate, w_e_up, w_e_down, norm_final):
    f32, bf16 = jnp.float32, jnp.bfloat16
    l = 0
    w = w_in[l]
    w_rg = jnp.concatenate([_block_diag_groups(w_rg_a[l]), _block_diag_groups(w_rg_x[l])], axis=2).astype(bf16)
    w_rt = jnp.concatenate([w_router[l], w_group[l],
                            jnp.zeros((D_MODEL, ROUTER_W - N_EXPERTS - N_GROUPS), f32)], axis=1).astype(bf16)
    b_rt = jnp.concatenate([b_router[l], b_group[l], jnp.zeros((ROUTER_W - N_EXPERTS - N_GROUPS,), f32)])[None, :]
    weights = (
        norm_mix[l][None, :], w.astype(bf16), b_merge[l][None, :], conv_w[l], conv_b[l][None, :], w_rg,
        b_rg_a[l][None, :], b_rg_x[l][None, :], lru_lambda[l][None, :], attn_sink[l], _bias_by_rel(rel_bias) * LOG2E,
        w_lru_proj[l].astype(bf16), w_attn_proj[l].astype(bf16), w_out[l].astype(bf16), norm_ffn[l][None, :],
        w_rt, b_rt,
    )
    nf = norm_final[None, :]

    bp, tp, _ = x_prompt.shape
    assert bp == 1
    xp = x_prompt.reshape(tp, D_MODEL)
    zeros = lambda *s: jnp.zeros(s, f32)
    p_x1, p_xn2, p_comb, p_conv, p_h, p_k, p_v = _mixer(
        xp, zeros(1, SUBLANES, D_RNN), zeros(1, SUBLANES, D_RNN),
        jnp.zeros((1, HIST, KV2_W), bf16), jnp.zeros((1, HIST, KV2_W), bf16), weights,
        nseg=1, seg_len=PROMPT_TILE, mask_history=True, sparse_out=True)
    pos, tile_group, n_slots = _sorted_slots(p_comb, SORT_TILE)
    x_sorted, rec_sorted, wg, wu, wd = _scatter_rows(pos, p_xn2, p_comb, w_e_gate[l], w_e_up[l], w_e_down[l],
                                                     n_slots, tile=SORT_TILE)
    wd = wd.reshape(N_GROUPS, GROUP_FF, D_MODEL)
    o_sorted = _moe_sorted(tile_group, x_sorted, rec_sorted, wg, wu, wd, tile=SORT_TILE)
    y_p = _gather_norm(pos, o_sorted, p_x1, nf, tile=SORT_TILE).reshape(x_prompt.shape)

    bs, ts, _ = x_sample.shape
    xs = x_sample.reshape(bs * ts, D_MODEL)
    conv0 = jnp.pad(state_lru_conv[l], ((0, 0), (SUBLANES - (CONV_W - 1), 0), (0, 0)))
    h0 = jnp.broadcast_to(state_lru_h[l][:, None, :], (bs, SUBLANES, D_RNN))
    ck = cache_swa_k[l].reshape(bs, -1, KV_W)
    cv = cache_swa_v[l].reshape(bs, -1, KV_W)
    s_x1, s_xn2, s_comb, s_conv, s_h, s_k, s_v = _mixer(
        xs, conv0, h0, _dup_heads(ck).astype(bf16), _dup_heads(cv).astype(bf16), weights,
        nseg=bs, seg_len=ts, mask_history=False, sparse_out=False)
    y_s = _moe_groups(s_xn2, s_comb, s_x1, wg, wu, wd, nf, tile=bs * ts).reshape(x_sample.shape)

    p_lru_h = p_h[None]
    p_lru_conv = p_conv[None]
    heads = lambda a: a.reshape(a.shape[:-1] + (N_KV, HEAD_DIM))
    p_swa_k = heads(p_k)[None]
    p_swa_v = heads(p_v)[None]
    s_lru_h = s_h[None]
    s_lru_conv = s_conv[None]
    win = ck.shape[1]
    s_swa_k = jnp.concatenate([cache_swa_k[l][:, ts:], heads(s_k)], axis=1)[:, -win:][None]
    s_swa_v = jnp.concatenate([cache_swa_v[l][:, ts:], heads(s_v)], axis=1)[:, -win:][None]
    return (y_p, y_s, p_lru_h, p_lru_conv, p_swa_k, p_swa_v, s_lru_h, s_lru_conv, s_swa_k, s_swa_v)
```

```python
import functools
import math

import jax
import jax.numpy as jnp
from jax import lax
from jax.experimental import pallas as pl
from jax.experimental.pallas import tpu as pltpu

D_MODEL = 1024
CHUNK = 64
D_RNN = D_MODEL
CONV_W = 4
LRU_BLOCKS = 16
LRU_BLOCK_W = D_RNN // LRU_BLOCKS
LRU_C = 8.0
N_HEADS = 16
N_KV = 4
HEAD_DIM = 64
GROUP = N_HEADS // N_KV
WINDOW = 128
LOOKBACK_CHUNKS = -(-WINDOW // CHUNK)
HIST = LOOKBACK_CHUNKS * CHUNK
BAND = HIST + CHUNK
Q_W = N_HEADS * HEAD_DIM
KV_W = N_KV * HEAD_DIM
REL_BUCKETS = 32
REL_MAX_DIST = 128
N_GROUPS = 4
EXPERTS_PER_GROUP = 8
N_EXPERTS = N_GROUPS * EXPERTS_PER_GROUP
D_EXPERT = 256
EPS = 1e-6
NEG_INF = -1e30

LANES = 128
SUBLANES = 8
KV2_W = N_KV * LANES
RG_GROUP = 256
N_RG_GROUPS = D_RNN // RG_GROUP
ROUTER_W = LANES
VMEM_LIMIT = 60 * 1024 * 1024
PACK_W = D_MODEL // 2
GROUP_FF = EXPERTS_PER_GROUP * D_EXPERT
GID_LANE = EXPERTS_PER_GROUP
SORT_TILE = 512
PROMPT_TILE = 512

C_LX, C_LG, C_Q = 0, D_RNN, 2 * D_RNN
C_K = C_Q + Q_W
C_V = C_K + KV_W
C_G = C_V + KV_W
P_LG, P_G = 0, D_RNN
P_KV = P_G + 2 * D_MODEL
P_W = P_KV + 2 * KV_W
PROJ_BLOCK = 256
V_STRIDE = 2 * LANES
SCAN_ROWS = SUBLANES * SUBLANES
LOG2E = math.log2(math.e)
REL_SPAN = BAND + CHUNK


def _rmsnorm(x, g):
    return x * lax.rsqrt(jnp.mean(x * x, axis=-1, keepdims=True) + EPS) * g


def _bdot(a, b):
    return jnp.dot(a, b, preferred_element_type=jnp.float32)


def _pack_bf16_pairs(x):
    return pltpu.pack_elementwise([x[:, :PACK_W], x[:, PACK_W:]], packed_dtype=jnp.bfloat16)


def _unpack_bf16_pairs(p):
    lo = pltpu.unpack_elementwise(p, index=0, packed_dtype=jnp.bfloat16, unpacked_dtype=jnp.float32)
    hi = pltpu.unpack_elementwise(p, index=1, packed_dtype=jnp.bfloat16, unpacked_dtype=jnp.float32)
    return jnp.concatenate([lo, hi], axis=1)


def _mixer_kernel(nseg, seg_len, mask_history, sparse_out,
                  x_ref, conv0_ref, h0_ref, k0_ref, v0_ref,
                  norm_mix_ref, w_in_ref, b_merge_ref, conv_w_ref, conv_b_ref, w_rg_ref, b_rg_a_ref, b_rg_x_ref,
                  lam_ref, sink_ref, byrel_ref, w_lru_ref, w_attn_ref, w_out_ref, norm_ffn_ref, w_rt_ref, b_rt_ref,
                  x1_ref, xn2_ref, comb_ref, conv_out_ref, h_out_ref, k_out_ref, v_out_ref,
                  xp_ref, hc_ref, kbuf_ref, vbuf_ref, a_ref, b_ref, attn_ref, xn_ref, pf_ref, pq_ref, bias_ref):
    step = pl.program_id(0)
    m_rows = nseg * seg_len
    n_chunks = seg_len // CHUNK
    keep = min(seg_len, HIST)

    @pl.when(step == 0)
    def _():
        xp_ref[...] = conv0_ref[...]
        hc_ref[...] = h0_ref[...]
        kbuf_ref[:, 0:HIST, :] = k0_ref[...]
        vbuf_ref[:, :, :] = jnp.ones(vbuf_ref.shape, jnp.bfloat16)
        for kv in range(N_KV):
            vbuf_ref[:, 0:HIST, kv * V_STRIDE:kv * V_STRIDE + LANES] = v0_ref[:, :, kv * LANES:(kv + 1) * LANES]
        for h in range(N_HEADS):
            tiled = jnp.broadcast_to(byrel_ref[h:h + 1, :], (CHUNK, REL_SPAN))
            block = pltpu.roll(tiled, REL_SPAN - (CHUNK - 1), axis=1, stride=1, stride_axis=0)
            bias_ref[h // GROUP, (h % GROUP) * CHUNK:(h % GROUP + 1) * CHUNK, :] = block[:, 0:BAND]

    x = x_ref[...]
    xn_ref[...] = _rmsnorm(x, norm_mix_ref[...]).astype(jnp.bfloat16)

    tasks = []

    def f32_block(src_ref, c_src, c_dst):
        def run():
            pf_ref[:, c_dst:c_dst + PROJ_BLOCK] = _bdot(xn_ref[...], src_ref[:, c_src:c_src + PROJ_BLOCK])
        return run

    def q_block(c):
        def run():
            pq_ref[:, c:c + PROJ_BLOCK] = (_bdot(xn_ref[...], w_in_ref[:, C_Q + c:C_Q + c + PROJ_BLOCK])
                                           * (HEAD_DIM ** -0.5 * LOG2E)).astype(jnp.bfloat16)
        return run

    for c in range(0, D_RNN, PROJ_BLOCK):
        tasks.append(f32_block(w_in_ref, C_LG + c, P_LG + c))
    for c in range(0, 2 * D_MODEL, PROJ_BLOCK):
        tasks.append(f32_block(w_in_ref, C_G + c, P_G + c))
    for c in range(0, 2 * KV_W, PROJ_BLOCK):
        tasks.append(f32_block(w_in_ref, C_K + c, P_KV + c))
    for c in range(0, Q_W, PROJ_BLOCK):
        tasks.append(q_block(c))

    def pump(n):
        for _ in range(min(n, len(tasks))):
            tasks.pop(0)()

    lru_x = _bdot(xn_ref[...], w_in_ref[:, C_LX:C_LX + D_RNN])
    conv_w = conv_w_ref[...]
    row = lax.broadcasted_iota(jnp.int32, (SUBLANES, D_RNN), 0)
    xc_parts = []
    for s in range(nseg):
        xs = lru_x[s * seg_len:(s + 1) * seg_len, :]
        prev = xp_ref[s]
        tail = xs[seg_len - SUBLANES:, :]
        acc = conv_b_ref[...] + xs * conv_w[CONV_W - 1:CONV_W, :]
        pump(1)
        for k in range(1, CONV_W):
            shifted = pltpu.roll(xs, k, axis=0)
            first = jnp.where(row < k, pltpu.roll(prev, k, axis=0), shifted[0:SUBLANES])
            shifted = jnp.concatenate([first, shifted[SUBLANES:]], axis=0)
            acc = acc + shifted * conv_w[CONV_W - 1 - k:CONV_W - k, :]
            pump(1)
        xc_parts.append(acc)
        conv_out_ref[s] = tail[SUBLANES - (CONV_W - 1):, :]
        xp_ref[s] = tail
    xc = xc_parts[0] if nseg == 1 else jnp.concatenate(xc_parts, axis=0)
    xc_b = xc.astype(jnp.bfloat16)

    lam = lam_ref[...]
    log_sig = jnp.minimum(lam, 0.0) - jnp.log1p(jnp.exp(-jnp.abs(lam)))
    c8_half = (0.5 * LRU_C) * log_sig
    for j in range(N_RG_GROUPS):
        cs = slice(j * RG_GROUP, (j + 1) * RG_GROUP)
        pre = _bdot(xc_b[:, cs], w_rg_ref[j])
        t_r = jnp.tanh(pre[:, :RG_GROUP] + 0.5 * b_rg_a_ref[:, cs])
        i = 0.5 * jnp.tanh(pre[:, RG_GROUP:] + 0.5 * b_rg_x_ref[:, cs]) + 0.5
        log_a = c8_half[:, cs] * t_r + c8_half[:, cs]
        a_val = jnp.exp(log_a)
        th = jnp.tanh(log_a)
        b_val = jnp.sqrt(-2.0 * th / (1.0 - th)) * i * xc[:, cs]
        for t in range(RG_GROUP // LANES):
            lt = j * (RG_GROUP // LANES) + t
            a_ref[lt] = a_val[:, t * LANES:(t + 1) * LANES]
            b_ref[lt] = b_val[:, t * LANES:(t + 1) * LANES]
        pump(2)

    row_c = lax.broadcasted_iota(jnp.int32, (SUBLANES, LANES), 0)
    for s in range(nseg):
        for lt in range(D_RNN // LANES):
            cs = slice(lt * LANES, (lt + 1) * LANES)
            carry = hc_ref[s, :, cs]
            for blk in range(seg_len // SCAN_ROWS):
                r0 = s * seg_len + blk * SCAN_ROWS
                slab = lambda ref, i: ref[lt, pl.ds(r0 + i, SUBLANES, stride=SUBLANES), :]
                a_loc, b_loc = [slab(a_ref, 0)], [slab(b_ref, 0)]
                for i in range(1, SUBLANES):
                    a_i = slab(a_ref, i)
                    b_loc.append(a_i * b_loc[-1] + slab(b_ref, i))
                    a_loc.append(a_i * a_loc[-1])
                a_e, b_e = a_loc[-1], b_loc[-1]
                for d in (1, 2, 4):
                    m = row_c >= d
                    b_e = jnp.where(m, a_e * pltpu.roll(b_e, d, axis=0) + b_e, b_e)
                    a_e = jnp.where(m, a_e * pltpu.roll(a_e, d, axis=0), a_e)
                h_end = a_e * carry + b_e
                c_in = jnp.where(row_c == 0, carry, pltpu.roll(h_end, 1, axis=0))
                for i in range(SUBLANES):
                    b_ref[lt, pl.ds(r0 + i, SUBLANES, stride=SUBLANES), :] = a_loc[i] * c_in + b_loc[i]
                carry = jnp.broadcast_to(h_end[SUBLANES - 1:SUBLANES, :], (SUBLANES, LANES))
            hc_ref[s, :, cs] = carry
            h_out_ref[s:s + 1, cs] = carry[0:1, :]
            pump(1)
    pump(len(tasks))

    lru_gate = pf_ref[:, P_LG:P_LG + D_RNN]
    h_all = jnp.concatenate([b_ref[lt] for lt in range(D_RNN // LANES)], axis=1)
    lru_y = (h_all * jax.nn.gelu(lru_gate)).astype(jnp.bfloat16)
    half_lru = _bdot(lru_y, w_lru_ref[...])
    t_gate = jnp.tanh(pf_ref[:, P_G:P_G + D_MODEL] + 0.5 * b_merge_ref[:, 0:D_MODEL])
    mixed = t_gate * half_lru + half_lru

    q = pq_ref[...]
    k = pf_ref[:, P_KV:P_KV + KV_W]
    v = pf_ref[:, P_KV + KV_W:P_KV + 2 * KV_W]
    lane_m = lax.broadcasted_iota(jnp.int32, (m_rows, LANES), 1)

    def dup_head(a, kv):
        src = a[:, (kv // 2) * LANES:(kv // 2 + 1) * LANES]
        swapped = pltpu.roll(src, HEAD_DIM, axis=1)
        first, second = (src, swapped) if kv % 2 == 0 else (swapped, src)
        return jnp.where(lane_m < HEAD_DIM, first, second).astype(jnp.bfloat16)

    k_dup = [dup_head(k, kv) for kv in range(N_KV)]
    v_dup = [dup_head(v, kv) for kv in range(N_KV)]

    lane_q = lax.broadcasted_iota(jnp.int32, (CHUNK, LANES), 1)
    lane_o = lax.broadcasted_iota(jnp.int32, (CHUNK, LANES), 1)
    key_lane = lax.broadcasted_iota(jnp.int32, (1, BAND), 1)
    row_grp = lax.broadcasted_iota(jnp.int32, (GROUP * CHUNK, 1), 0) // CHUNK
    for s in range(nseg):
        rows = slice(s * seg_len, (s + 1) * seg_len)
        for kv in range(N_KV):
            kbuf_ref[s, HIST:HIST + seg_len, kv * LANES:(kv + 1) * LANES] = k_dup[kv][rows]
            vbuf_ref[s, HIST:HIST + seg_len, kv * V_STRIDE:kv * V_STRIDE + LANES] = v_dup[kv][rows]
        k_out_ref[s] = k[s * seg_len + seg_len - keep:(s + 1) * seg_len]
        v_out_ref[s] = v[s * seg_len + seg_len - keep:(s + 1) * seg_len]
        for c in range(n_chunks):
            q_c = q[s * seg_len + c * CHUNK:s * seg_len + (c + 1) * CHUNK]
            slabs = []
            for kv in range(N_KV):
                parts = []
                for g in range(GROUP):
                    col = kv * GROUP * HEAD_DIM + (g // 2) * LANES
                    slab = q_c[:, col:col + LANES]
                    keep_lo = (g % 2) == 0
                    sel = (lane_q < HEAD_DIM) if keep_lo else (lane_q >= HEAD_DIM)
                    parts.append(jnp.where(sel, slab, jnp.zeros_like(slab)))
                q_stack = jnp.concatenate(parts, axis=0)
                k_band = kbuf_ref[s, c * CHUNK:c * CHUNK + BAND, kv * LANES:(kv + 1) * LANES]
                v_band = vbuf_ref[s, c * CHUNK:c * CHUNK + BAND, kv * V_STRIDE:(kv + 1) * V_STRIDE]
                sc = lax.dot_general(q_stack, k_band, (((1,), (1,)), ((), ())),
                                     preferred_element_type=jnp.float32)
                sc = sc + bias_ref[kv]
                if mask_history and c < LOOKBACK_CHUNKS:
                    first_valid = HIST - (step * n_chunks + c) * CHUNK
                    sc = jnp.where(key_lane >= first_valid, sc, NEG_INF)
                sink = jnp.zeros((GROUP * CHUNK, 1), jnp.float32)
                for g in range(GROUP):
                    sink = jnp.where(row_grp == g, sink_ref[kv * GROUP + g] * LOG2E, sink)
                m = jnp.maximum(jnp.max(sc, axis=-1, keepdims=True), sink)
                p = jnp.exp2(sc - m).astype(jnp.bfloat16)
                o = _bdot(p, v_band)
                denom = o[:, LANES:2 * LANES] + jnp.exp2(sink - m)
                o = o[:, 0:LANES] / denom
                for pair in range(GROUP // 2):
                    lo = o[(2 * pair) * CHUNK:(2 * pair + 1) * CHUNK]
                    hi = o[(2 * pair + 1) * CHUNK:(2 * pair + 2) * CHUNK]
                    slabs.append(jnp.where(lane_o < HEAD_DIM, lo, hi))
            attn_ref[s * seg_len + c * CHUNK:s * seg_len + (c + 1) * CHUNK, :] = (
                jnp.concatenate(slabs, axis=1).astype(jnp.bfloat16))
        kbuf_ref[s, 0:HIST, :] = kbuf_ref[s, seg_len:seg_len + HIST, :]
        vbuf_ref[s, 0:HIST, :] = vbuf_ref[s, seg_len:seg_len + HIST, :]

    attn = _bdot(attn_ref[...], w_attn_ref[...])
    gates = pf_ref[:, P_G + D_MODEL:P_G + 2 * D_MODEL] + 0.5 * b_merge_ref[:, D_MODEL:2 * D_MODEL]
    mixed = (mixed + (jnp.tanh(gates) * attn + attn)).astype(jnp.bfloat16)
    x1 = x + _bdot(mixed, w_out_ref[...])
    x1_ref[...] = x1

    xn2 = _rmsnorm(x1, norm_ffn_ref[...])
    xn2_ref[...] = _pack_bf16_pairs(xn2) if sparse_out else xn2.astype(jnp.bfloat16)
    logits = _bdot(xn2.astype(jnp.bfloat16), w_rt_ref[...]) + b_rt_ref[...]
    lane = lax.broadcasted_iota(jnp.int32, (m_rows, ROUTER_W), 1).astype(jnp.float32)
    far = jnp.float32(2 * ROUTER_W)
    is_group = (lane >= N_EXPERTS) & (lane < N_EXPERTS + N_GROUPS)
    gl = jnp.where(is_group, logits, NEG_INF)
    g_max = jnp.max(gl, axis=-1, keepdims=True)
    g_idx = jnp.min(jnp.where(gl == g_max, lane, far), axis=-1, keepdims=True) - N_EXPERTS
    g_w = 1.0 / jnp.sum(jnp.where(is_group, jnp.exp(gl - g_max), 0.0), axis=-1, keepdims=True)
    in_group = (lane < N_EXPERTS) & (jnp.floor(lane * (1.0 / EXPERTS_PER_GROUP)) == g_idx)
    el = jnp.where(in_group, logits, NEG_INF)
    e1 = jnp.max(el, axis=-1, keepdims=True)
    i1 = jnp.min(jnp.where(el == e1, lane, far), axis=-1, keepdims=True)
    el2 = jnp.where(lane == i1, NEG_INF, el)
    e2 = jnp.max(el2, axis=-1, keepdims=True)
    i2 = jnp.min(jnp.where(el2 == e2, lane, far), axis=-1, keepdims=True)
    t = jnp.exp(e2 - e1)
    w1 = g_w / (1.0 + t)
    w2 = w1 * t
    if sparse_out:
        e_lane = lane + g_idx * EXPERTS_PER_GROUP
        rec = jnp.where(e_lane == i1, w1, 0.0) + jnp.where(e_lane == i2, w2, 0.0)
        rec = jnp.where(lane < EXPERTS_PER_GROUP, rec, 0.0)
        comb_ref[...] = jnp.where(lane == GID_LANE, g_idx, rec)
    else:
        comb_ref[...] = jnp.where(lane == i1, w1, 0.0) + jnp.where(lane == i2, w2, 0.0)


def _const_spec(shape):
    zeros = (0,) * len(shape)
    return pl.BlockSpec(shape, lambda i: zeros, pipeline_mode=pl.Buffered(1))


def _mixer(x, conv0, h0, k0, v0, weights, *, nseg, seg_len, mask_history, sparse_out):
    n_tok = x.shape[0]
    m_rows = nseg * seg_len
    n_steps = n_tok // m_rows
    keep = min(seg_len, HIST)
    if n_steps == 1:
        row_spec = lambda w: pl.BlockSpec((m_rows, w), lambda i: (i, 0), pipeline_mode=pl.Buffered(1))
    else:
        row_spec = lambda w: pl.BlockSpec((m_rows, w), lambda i: (i, 0))
    in_specs = [row_spec(D_MODEL), _const_spec(conv0.shape), _const_spec(h0.shape), _const_spec(k0.shape),
                _const_spec(v0.shape)]
    for w in weights:
        if w.ndim == 1:
            in_specs.append(pl.BlockSpec(memory_space=pltpu.SMEM))
        else:
            in_specs.append(_const_spec(w.shape))
    state_spec = lambda r, w: pl.BlockSpec((nseg, r, w), lambda i: (0, 0, 0))
    out_shape = (
        jax.ShapeDtypeStruct((n_tok, D_MODEL), jnp.float32),
        (jax.ShapeDtypeStruct((n_tok, PACK_W), jnp.uint32) if sparse_out
         else jax.ShapeDtypeStruct((n_tok, D_MODEL), jnp.bfloat16)),
        jax.ShapeDtypeStruct((n_tok, ROUTER_W), jnp.float32),
        jax.ShapeDtypeStruct((nseg, CONV_W - 1, D_RNN), jnp.float32),
        jax.ShapeDtypeStruct((nseg, D_RNN), jnp.float32),
        jax.ShapeDtypeStruct((nseg, keep, KV_W), jnp.float32),
        jax.ShapeDtypeStruct((nseg, keep, KV_W), jnp.float32),
    )
    out_specs = (row_spec(D_MODEL), row_spec(PACK_W if sparse_out else D_MODEL), row_spec(ROUTER_W),
                 state_spec(CONV_W - 1, D_RNN), pl.BlockSpec((nseg, D_RNN), lambda i: (0, 0)),
                 state_spec(keep, KV_W), state_spec(keep, KV_W))
    scratch = [
        pltpu.VMEM((nseg, SUBLANES, D_RNN), jnp.float32),
        pltpu.VMEM((nseg, SUBLANES, D_RNN), jnp.float32),
        pltpu.VMEM((nseg, HIST + seg_len, KV2_W), jnp.bfloat16),
        pltpu.VMEM((nseg, HIST + seg_len, N_KV * V_STRIDE), jnp.bfloat16),
        pltpu.VMEM((D_RNN // LANES, m_rows, LANES), jnp.float32),
        pltpu.VMEM((D_RNN // LANES, m_rows, LANES), jnp.float32),
        pltpu.VMEM((m_rows, Q_W), jnp.bfloat16),
        pltpu.VMEM((m_rows, D_MODEL), jnp.bfloat16),
        pltpu.VMEM((m_rows, P_W), jnp.float32),
        pltpu.VMEM((m_rows, Q_W), jnp.bfloat16),
        pltpu.VMEM((N_KV, GROUP * CHUNK, BAND), jnp.float32),
    ]
    return pl.pallas_call(
        functools.partial(_mixer_kernel, nseg, seg_len, mask_history, sparse_out),
        grid=(n_steps,),
        in_specs=in_specs,
        out_specs=out_specs,
        out_shape=out_shape,
        scratch_shapes=scratch,
        compiler_params=pltpu.CompilerParams(dimension_semantics=("arbitrary",), vmem_limit_bytes=VMEM_LIMIT),
        name="mixer_prompt" if mask_history else "mixer_sample",
    )(x, conv0, h0, k0, v0, *weights)


def _group_swiglu(x, cols, wg_ref, wu_ref, wd_ref):
    parts = []
    for e in range(EXPERTS_PER_GROUP):
        h = jax.nn.silu(_bdot(x, wg_ref[e])) * _bdot(x, wu_ref[e])
        parts.append((h * cols[e]).astype(jnp.bfloat16))
    return _bdot(jnp.concatenate(parts, axis=1), wd_ref[...])


def _moe_groups_kernel(xn2_ref, comb_ref, x1_ref, wg_ref, wu_ref, wd_ref, norm_ref, y_ref, acc_ref):
    g = pl.program_id(1)

    @pl.when(g == 0)
    def _():
        acc_ref[...] = jnp.zeros_like(acc_ref)

    comb = comb_ref[...]
    lane = lax.broadcasted_iota(jnp.int32, comb.shape, 1)
    cols = [jnp.sum(jnp.where(lane == g * EXPERTS_PER_GROUP + e, comb, 0.0), axis=-1, keepdims=True)
            for e in range(EXPERTS_PER_GROUP)]
    acc_ref[...] += _group_swiglu(xn2_ref[...], cols, wg_ref, wu_ref, wd_ref)

    @pl.when(g == N_GROUPS - 1)
    def _():
        y_ref[...] = _rmsnorm(x1_ref[...] + acc_ref[...], norm_ref[...])


def _group_weight_specs(group_of):
    return [pl.BlockSpec((EXPERTS_PER_GROUP, D_MODEL, D_EXPERT), lambda *a: (group_of(*a), 0, 0)),
            pl.BlockSpec((EXPERTS_PER_GROUP, D_MODEL, D_EXPERT), lambda *a: (group_of(*a), 0, 0)),
            pl.BlockSpec((None, GROUP_FF, D_MODEL), lambda *a: (group_of(*a), 0, 0))]


def _moe_groups(xn2, comb, x1, wg, wu, wd, norm_final, *, tile):
    n_tok = xn2.shape[0]
    row_spec = lambda w: pl.BlockSpec((tile, w), lambda i, g: (i, 0))
    return pl.pallas_call(
        _moe_groups_kernel,
        grid=(n_tok // tile, N_GROUPS),
        in_specs=[row_spec(D_MODEL), row_spec(ROUTER_W), row_spec(D_MODEL),
                  *_group_weight_specs(lambda i, g: g),
                  pl.BlockSpec((1, D_MODEL), lambda i, g: (0, 0))],
        out_specs=row_spec(D_MODEL),
        out_shape=jax.ShapeDtypeStruct((n_tok, D_MODEL), jnp.float32),
        scratch_shapes=[pltpu.VMEM((tile, D_MODEL), jnp.float32)],
        compiler_params=pltpu.CompilerParams(dimension_semantics=("arbitrary", "arbitrary"),
                                             vmem_limit_bytes=VMEM_LIMIT),
        name="moe_groups",
    )(xn2, comb, x1, wg, wu, wd, norm_final)


def _scatter_rows_kernel(tile, pos_ref, x_ref, rec_ref, wg_ref, wu_ref, wd_ref,
                         xs_ref, recs_ref, wg_out_ref, wu_out_ref, wd_out_ref):
    i = pl.program_id(0)

    @pl.when(i == 0)
    def _():
        xs_ref[...] = jnp.zeros_like(xs_ref)
        recs_ref[...] = jnp.zeros_like(recs_ref)

    wg_out_ref[...] = wg_ref[...].astype(jnp.bfloat16)
    wu_out_ref[...] = wu_ref[...].astype(jnp.bfloat16)
    wd_out_ref[...] = wd_ref[...].astype(jnp.bfloat16)

    def body(j, carry):
        r0 = pl.multiple_of(j * SUBLANES, SUBLANES)
        xb = x_ref[pl.ds(r0, SUBLANES), :]
        rb = rec_ref[pl.ds(r0, SUBLANES), :]
        for k in range(SUBLANES):
            p = pos_ref[i * tile + j * SUBLANES + k]
            xs_ref[pl.ds(p, 1), :] = xb[k:k + 1, :]
            recs_ref[pl.ds(p, 1), :] = rb[k:k + 1, :]
        return carry

    lax.fori_loop(0, tile // SUBLANES, body, 0)


def _scatter_rows(pos, xn2p, rec, w_gate, w_up, w_down, n_slots, *, tile):
    n_tok = xn2p.shape[0]
    n_steps = n_tok // tile
    per_step = N_EXPERTS // n_steps
    assert per_step * n_steps == N_EXPERTS
    w_spec = lambda a: pl.BlockSpec((per_step,) + a.shape[1:], lambda i, pos: (i, 0, 0))
    return pl.pallas_call(
        functools.partial(_scatter_rows_kernel, tile),
        grid_spec=pltpu.PrefetchScalarGridSpec(
            num_scalar_prefetch=1,
            grid=(n_steps,),
            in_specs=[pl.BlockSpec((tile, PACK_W), lambda i, pos: (i, 0)),
                      pl.BlockSpec((tile, ROUTER_W), lambda i, pos: (i, 0)),
                      w_spec(w_gate), w_spec(w_up), w_spec(w_down)],
            out_specs=[pl.BlockSpec((n_slots, PACK_W), lambda i, pos: (0, 0)),
                       pl.BlockSpec((n_slots, ROUTER_W), lambda i, pos: (0, 0)),
                       w_spec(w_gate), w_spec(w_up), w_spec(w_down)],
        ),
        out_shape=(jax.ShapeDtypeStruct((n_slots, PACK_W), jnp.uint32),
                   jax.ShapeDtypeStruct((n_slots, ROUTER_W), jnp.float32),
                   jax.ShapeDtypeStruct(w_gate.shape, jnp.bfloat16),
                   jax.ShapeDtypeStruct(w_up.shape, jnp.bfloat16),
                   jax.ShapeDtypeStruct(w_down.shape, jnp.bfloat16)),
        compiler_params=pltpu.CompilerParams(dimension_semantics=("arbitrary",), vmem_limit_bytes=VMEM_LIMIT),
        name="moe_scatter",
    )(pos, xn2p, rec, w_gate, w_up, w_down)


def _moe_sorted_kernel(tg_ref, xs_ref, recs_ref, wg_ref, wu_ref, wd_ref, o_ref):
    x = _unpack_bf16_pairs(xs_ref[...]).astype(jnp.bfloat16)
    rec = recs_ref[...]
    cols = [rec[:, e:e + 1] for e in range(EXPERTS_PER_GROUP)]
    o_ref[...] = _pack_bf16_pairs(_group_swiglu(x, cols, wg_ref, wu_ref, wd_ref))


def _moe_sorted(tile_group, xs, recs, wg, wu, wd, *, tile):
    n_slots = xs.shape[0]
    return pl.pallas_call(
        _moe_sorted_kernel,
        grid_spec=pltpu.PrefetchScalarGridSpec(
            num_scalar_prefetch=1,
            grid=(n_slots // tile,),
            in_specs=[pl.BlockSpec((tile, PACK_W), lambda i, tg: (i, 0)),
                      pl.BlockSpec((tile, ROUTER_W), lambda i, tg: (i, 0)),
                      *_group_weight_specs(lambda i, tg: tg[i])],
            out_specs=pl.BlockSpec((tile, PACK_W), lambda i, tg: (i, 0)),
        ),
        out_shape=jax.ShapeDtypeStruct((n_slots, PACK_W), jnp.uint32),
        compiler_params=pltpu.CompilerParams(dimension_semantics=("arbitrary",), vmem_limit_bytes=VMEM_LIMIT),
        name="moe_sorted",
    )(tile_group, xs, recs, wg, wu, wd)


def _gather_norm_kernel(tile, pos_ref, os_ref, x1_ref, norm_ref, y_ref, buf_ref):
    i = pl.program_id(0)

    def body(j, carry):
        r0 = pl.multiple_of(j * SUBLANES, SUBLANES)
        rows = [os_ref[pl.ds(pos_ref[i * tile + j * SUBLANES + k], 1), :] for k in range(SUBLANES)]
        buf_ref[pl.ds(r0, SUBLANES), :] = jnp.concatenate(rows, axis=0)
        return carry

    lax.fori_loop(0, tile // SUBLANES, body, 0)
    y_ref[...] = _rmsnorm(x1_ref[...] + _unpack_bf16_pairs(buf_ref[...]), norm_ref[...])


def _gather_norm(pos, o_sorted, x1, norm_final, *, tile):
    n_tok = x1.shape[0]
    n_slots = o_sorted.shape[0]
    return pl.pallas_call(
        functools.partial(_gather_norm_kernel, tile),
        grid_spec=pltpu.PrefetchScalarGridSpec(
            num_scalar_prefetch=1,
            grid=(n_tok // tile,),
            in_specs=[pl.BlockSpec((n_slots, PACK_W), lambda i, pos: (0, 0), pipeline_mode=pl.Buffered(1)),
                      pl.BlockSpec((tile, D_MODEL), lambda i, pos: (i, 0)),
                      pl.BlockSpec((1, D_MODEL), lambda i, pos: (0, 0))],
            out_specs=pl.BlockSpec((tile, D_MODEL), lambda i, pos: (i, 0)),
            scratch_shapes=[pltpu.VMEM((tile, PACK_W), jnp.uint32)],
        ),
        out_shape=jax.ShapeDtypeStruct((n_tok, D_MODEL), jnp.float32),
        compiler_params=pltpu.CompilerParams(dimension_semantics=("arbitrary",), vmem_limit_bytes=VMEM_LIMIT),
        name="moe_gather_norm",
    )(pos, o_sorted, x1, norm_final)


def _sorted_slots(rec, tile):
    n_tok = rec.shape[0]
    gid = rec[:, GID_LANE].astype(jnp.int32)
    onehot = (gid[:, None] == jnp.arange(N_GROUPS, dtype=jnp.int32)[None, :]).astype(jnp.int32)
    csum = jnp.cumsum(onehot, axis=0)
    rank = jnp.sum((csum - onehot) * onehot, axis=1)
    padded = ((csum[-1] + tile - 1) // tile) * tile
    end = jnp.cumsum(padded)
    pos = jnp.sum(onehot * (end - padded)[None, :], axis=1) + rank
    n_tiles = n_tok // tile + N_GROUPS
    tile_start = jnp.arange(n_tiles, dtype=jnp.int32) * tile
    tile_group = jnp.minimum(jnp.sum((tile_start[:, None] >= end[None, :]).astype(jnp.int32), axis=1), N_GROUPS - 1)
    return pos.astype(jnp.int32), tile_group.astype(jnp.int32), n_tiles * tile


def _rel_bucket(rel):
    nb = REL_BUCKETS // 2
    n = -rel
    ret = jnp.where(n < 0, nb, 0)
    n = jnp.abs(n)
    max_exact = nb // 2
    nf = jnp.maximum(n, 1).astype(jnp.float32)
    large = max_exact + (jnp.log(nf / max_exact) / math.log(REL_MAX_DIST / max_exact) * (nb - max_exact)).astype(jnp.int32)
    large = jnp.minimum(large, nb - 1)
    return ret + jnp.where(n < max_exact, n, large)


def _bias_by_rel(rel_table):
    rels = jnp.arange(-(BAND - 1), CHUNK + 1, dtype=jnp.int32)
    return jnp.transpose(rel_table[_rel_bucket(rels)]).astype(jnp.float32)


def _dup_heads(a):
    a = a.reshape(a.shape[:-1] + (N_KV, HEAD_DIM))
    return jnp.concatenate([a, a], axis=-1).reshape(a.shape[:-2] + (KV2_W,))


def _block_diag_groups(w):
    per = RG_GROUP // LRU_BLOCK_W
    blk = w.reshape(N_RG_GROUPS, per, LRU_BLOCK_W, 1, LRU_BLOCK_W)
    eye = jnp.eye(per, dtype=w.dtype).reshape(1, per, 1, per, 1)
    return (blk * eye).reshape(N_RG_GROUPS, RG_GROUP, RG_GROUP)


def kernel(x_prompt, x_sample, state_lru_h, state_lru_conv, cache_swa_k, cache_swa_v, norm_mix, w_in, b_merge, conv_w, conv_b, w_rg_a, b_rg_a, w_rg_x, b_rg_x, lru_lambda, attn_sink, rel_bias, w_lru_proj, w_attn_proj, w_out, norm_ffn, w_group, b_group, w_router, b_router, w_e_gate, w_e_up, w_e_down, norm_final):
    f32, bf16 = jnp.float32, jnp.bfloat16
    l = 0
    col_scale = jnp.where(jnp.arange(w_in.shape[-1]) >= C_G, 0.5, 1.0).astype(f32)
    w = w_in[l] * col_scale[None, :]
    w_rg = (0.5 * jnp.concatenate([_block_diag_groups(w_rg_a[l]), _block_diag_groups(w_rg_x[l])], axis=2)).astype(bf16)
    w_rt = jnp.concatenate([w_router[l], w_group[l],
                            jnp.zeros((D_MODEL, ROUTER_W - N_EXPERTS - N_GROUPS), f32)], axis=1).astype(bf16)
    b_rt = jnp.concatenate([b_router[l], b_group[l], jnp.zeros((ROUTER_W - N_EXPERTS - N_GROUPS,), f32)])[None, :]
    weights = (
        norm_mix[l][None, :], w.astype(bf16), b_merge[l][None, :], conv_w[l], conv_b[l][None, :], w_rg,
        b_rg_a[l][None, :], b_rg_x[l][None, :], lru_lambda[l][None, :], attn_sink[l], _bias_by_rel(rel_bias) * LOG2E,
        (0.5 * w_lru_proj[l]).astype(bf16), (0.5 * w_attn_proj[l]).astype(bf16), w_out[l].astype(bf16),
        norm_ffn[l][None, :],
        w_rt, b_rt,
    )
    nf = norm_final[None, :]

    bp, tp, _ = x_prompt.shape
    assert bp == 1
    xp = x_prompt.reshape(tp, D_MODEL)
    zeros = lambda *s: jnp.zeros(s, f32)
    p_x1, p_xn2, p_comb, p_conv, p_h, p_k, p_v = _mixer(
        xp, zeros(1, SUBLANES, D_RNN), zeros(1, SUBLANES, D_RNN),
        jnp.zeros((1, HIST, KV2_W), bf16), jnp.zeros((1, HIST, KV2_W), bf16), weights,
        nseg=1, seg_len=PROMPT_TILE, mask_history=True, sparse_out=True)
    pos, tile_group, n_slots = _sorted_slots(p_comb, SORT_TILE)
    x_sorted, rec_sorted, wg, wu, wd = _scatter_rows(pos, p_xn2, p_comb, w_e_gate[l], w_e_up[l], w_e_down[l],
                                                     n_slots, tile=SORT_TILE)
    wd = wd.reshape(N_GROUPS, GROUP_FF, D_MODEL)
    o_sorted = _moe_sorted(tile_group, x_sorted, rec_sorted, wg, wu, wd, tile=SORT_TILE)
    y_p = _gather_norm(pos, o_sorted, p_x1, nf, tile=SORT_TILE).reshape(x_prompt.shape)

    bs, ts, _ = x_sample.shape
    xs = x_sample.reshape(bs * ts, D_MODEL)
    conv0 = jnp.pad(state_lru_conv[l], ((0, 0), (SUBLANES - (CONV_W - 1), 0), (0, 0)))
    h0 = jnp.broadcast_to(state_lru_h[l][:, None, :], (bs, SUBLANES, D_RNN))
    ck = cache_swa_k[l].reshape(bs, -1, KV_W)
    cv = cache_swa_v[l].reshape(bs, -1, KV_W)
    s_x1, s_xn2, s_comb, s_conv, s_h, s_k, s_v = _mixer(
        xs, conv0, h0, _dup_heads(ck).astype(bf16), _dup_heads(cv).astype(bf16), weights,
        nseg=bs, seg_len=ts, mask_history=False, sparse_out=False)
    y_s = _moe_groups(s_xn2, s_comb, s_x1, wg, wu, wd, nf, tile=bs * ts).reshape(x_sample.shape)

    p_lru_h = p_h[None]
    p_lru_conv = p_conv[None]
    heads = lambda a: a.reshape(a.shape[:-1] + (N_KV, HEAD_DIM))
    p_swa_k = heads(p_k)[None]
    p_swa_v = heads(p_v)[None]
    s_lru_h = s_h[None]
    s_lru_conv = s_conv[None]
    win = ck.shape[1]
    s_swa_k = jnp.concatenate([cache_swa_k[l][:, ts:], heads(s_k)], axis=1)[:, -win:][None]
    s_swa_v = jnp.concatenate([cache_swa_v[l][:, ts:], heads(s_v)], axis=1)[:, -win:][None]
    return (y_p, y_s, p_lru_h, p_lru_conv, p_swa_k, p_swa_v, s_lru_h, s_lru_conv, s_swa_k, s_swa_v)
```

```python
import functools
import math

import jax
import jax.numpy as jnp
from jax import lax
from jax.experimental import pallas as pl
from jax.experimental.pallas import tpu as pltpu

D_MODEL = 1024
CHUNK = 64
D_RNN = D_MODEL
CONV_W = 4
LRU_BLOCKS = 16
LRU_BLOCK_W = D_RNN // LRU_BLOCKS
LRU_C = 8.0
N_HEADS = 16
N_KV = 4
HEAD_DIM = 64
GROUP = N_HEADS // N_KV
WINDOW = 128
LOOKBACK_CHUNKS = -(-WINDOW // CHUNK)
HIST = LOOKBACK_CHUNKS * CHUNK
BAND = HIST + CHUNK
Q_W = N_HEADS * HEAD_DIM
KV_W = N_KV * HEAD_DIM
REL_BUCKETS = 32
REL_MAX_DIST = 128
N_GROUPS = 4
EXPERTS_PER_GROUP = 8
N_EXPERTS = N_GROUPS * EXPERTS_PER_GROUP
D_EXPERT = 256
EPS = 1e-6
NEG_INF = -1e30

LANES = 128
SUBLANES = 8
KV2_W = N_KV * LANES
RG_GROUP = 256
N_RG_GROUPS = D_RNN // RG_GROUP
ROUTER_W = LANES
VMEM_LIMIT = 60 * 1024 * 1024
PACK_W = D_MODEL // 2
GROUP_FF = EXPERTS_PER_GROUP * D_EXPERT
GID_LANE = EXPERTS_PER_GROUP
SORT_TILE = 512
GATHER_TILE = 1024
PROMPT_TILE = 512

C_LX, C_LG, C_Q = 0, D_RNN, 2 * D_RNN
C_K = C_Q + Q_W
C_V = C_K + KV_W
C_G = C_V + KV_W
P_LG, P_G = 0, D_RNN
P_KV = P_G + 2 * D_MODEL
P_W = P_KV + 2 * KV_W
PROJ_BLOCK = 256
V_STRIDE = 2 * LANES
SCAN_ROWS = SUBLANES * SUBLANES
LOG2E = math.log2(math.e)
REL_SPAN = BAND + CHUNK


def _rmsnorm(x, g):
    return x * lax.rsqrt(jnp.mean(x * x, axis=-1, keepdims=True) + EPS) * g


def _bdot(a, b):
    return jnp.dot(a, b, preferred_element_type=jnp.float32)


def _pack_bf16_pairs(x):
    return pltpu.pack_elementwise([x[:, :PACK_W], x[:, PACK_W:]], packed_dtype=jnp.bfloat16)


def _unpack_bf16_pairs(p):
    lo = pltpu.unpack_elementwise(p, index=0, packed_dtype=jnp.bfloat16, unpacked_dtype=jnp.float32)
    hi = pltpu.unpack_elementwise(p, index=1, packed_dtype=jnp.bfloat16, unpacked_dtype=jnp.float32)
    return jnp.concatenate([lo, hi], axis=1)


def _mixer_kernel(nseg, seg_len, mask_history, sparse_out,
                  x_ref, conv0_ref, h0_ref, k0_ref, v0_ref,
                  norm_mix_ref, w_in_ref, b_merge_ref, conv_w_ref, conv_b_ref, w_rg_ref, b_rg_a_ref, b_rg_x_ref,
                  lam_ref, sink_ref, byrel_ref, w_lru_ref, w_attn_ref, w_out_ref, norm_ffn_ref, w_rt_ref, b_rt_ref,
                  x1_ref, xn2_ref, comb_ref, conv_out_ref, h_out_ref, k_out_ref, v_out_ref,
                  xp_ref, hc_ref, kbuf_ref, vbuf_ref, a_ref, b_ref, attn_ref, xn_ref, pf_ref, pq_ref, bias_ref):
    step = pl.program_id(0)
    m_rows = nseg * seg_len
    n_chunks = seg_len // CHUNK
    keep = min(seg_len, HIST)

    @pl.when(step == 0)
    def _():
        xp_ref[...] = conv0_ref[...]
        hc_ref[...] = h0_ref[...]
        kbuf_ref[:, 0:HIST, :] = k0_ref[...]
        vbuf_ref[:, :, :] = jnp.ones(vbuf_ref.shape, jnp.bfloat16)
        for kv in range(N_KV):
            vbuf_ref[:, 0:HIST, kv * V_STRIDE:kv * V_STRIDE + LANES] = v0_ref[:, :, kv * LANES:(kv + 1) * LANES]
        for h in range(N_HEADS):
            tiled = jnp.broadcast_to(byrel_ref[h:h + 1, :], (CHUNK, REL_SPAN))
            block = pltpu.roll(tiled, REL_SPAN - (CHUNK - 1), axis=1, stride=1, stride_axis=0)
            bias_ref[h // GROUP, (h % GROUP) * CHUNK:(h % GROUP + 1) * CHUNK, :] = block[:, 0:BAND]

    x = x_ref[...]
    xn_ref[...] = _rmsnorm(x, norm_mix_ref[...]).astype(jnp.bfloat16)

    tasks = []

    def f32_block(src_ref, c_src, c_dst):
        def run():
            pf_ref[:, c_dst:c_dst + PROJ_BLOCK] = _bdot(xn_ref[...], src_ref[:, c_src:c_src + PROJ_BLOCK])
        return run

    def q_block(c):
        def run():
            pq_ref[:, c:c + PROJ_BLOCK] = (_bdot(xn_ref[...], w_in_ref[:, C_Q + c:C_Q + c + PROJ_BLOCK])
                                           * (HEAD_DIM ** -0.5 * LOG2E)).astype(jnp.bfloat16)
        return run

    for c in range(0, D_RNN, PROJ_BLOCK):
        tasks.append(f32_block(w_in_ref, C_LG + c, P_LG + c))
    for c in range(0, 2 * D_MODEL, PROJ_BLOCK):
        tasks.append(f32_block(w_in_ref, C_G + c, P_G + c))
    for c in range(0, 2 * KV_W, PROJ_BLOCK):
        tasks.append(f32_block(w_in_ref, C_K + c, P_KV + c))
    for c in range(0, Q_W, PROJ_BLOCK):
        tasks.append(q_block(c))

    def pump(n):
        for _ in range(min(n, len(tasks))):
            tasks.pop(0)()

    lru_x = _bdot(xn_ref[...], w_in_ref[:, C_LX:C_LX + D_RNN])
    conv_w = conv_w_ref[...]
    row = lax.broadcasted_iota(jnp.int32, (SUBLANES, D_RNN), 0)
    xc_parts = []
    for s in range(nseg):
        xs = lru_x[s * seg_len:(s + 1) * seg_len, :]
        prev = xp_ref[s]
        tail = xs[seg_len - SUBLANES:, :]
        acc = conv_b_ref[...] + xs * conv_w[CONV_W - 1:CONV_W, :]
        pump(1)
        for k in range(1, CONV_W):
            shifted = pltpu.roll(xs, k, axis=0)
            first = jnp.where(row < k, pltpu.roll(prev, k, axis=0), shifted[0:SUBLANES])
            shifted = jnp.concatenate([first, shifted[SUBLANES:]], axis=0)
            acc = acc + shifted * conv_w[CONV_W - 1 - k:CONV_W - k, :]
            pump(1)
        xc_parts.append(acc)
        conv_out_ref[s] = tail[SUBLANES - (CONV_W - 1):, :]
        xp_ref[s] = tail
    xc = xc_parts[0] if nseg == 1 else jnp.concatenate(xc_parts, axis=0)
    xc_b = xc.astype(jnp.bfloat16)

    lam = lam_ref[...]
    log_sig = jnp.minimum(lam, 0.0) - jnp.log1p(jnp.exp(-jnp.abs(lam)))
    c8_half = (0.5 * LRU_C) * log_sig
    for j in range(N_RG_GROUPS):
        cs = slice(j * RG_GROUP, (j + 1) * RG_GROUP)
        pre = _bdot(xc_b[:, cs], w_rg_ref[j])
        t_r = jnp.tanh(pre[:, :RG_GROUP] + 0.5 * b_rg_a_ref[:, cs])
        i = 0.5 * jnp.tanh(pre[:, RG_GROUP:] + 0.5 * b_rg_x_ref[:, cs]) + 0.5
        log_a = c8_half[:, cs] * t_r + c8_half[:, cs]
        a_val = jnp.exp(log_a)
        th = jnp.tanh(log_a)
        b_val = jnp.sqrt(-2.0 * th / (1.0 - th)) * i * xc[:, cs]
        for t in range(RG_GROUP // LANES):
            lt = j * (RG_GROUP // LANES) + t
            a_ref[lt] = a_val[:, t * LANES:(t + 1) * LANES]
            b_ref[lt] = b_val[:, t * LANES:(t + 1) * LANES]
        pump(2)

    row_c = lax.broadcasted_iota(jnp.int32, (SUBLANES, LANES), 0)
    for s in range(nseg):
        for lt in range(D_RNN // LANES):
            cs = slice(lt * LANES, (lt + 1) * LANES)
            carry = hc_ref[s, :, cs]
            for blk in range(seg_len // SCAN_ROWS):
                r0 = s * seg_len + blk * SCAN_ROWS
                slab = lambda ref, i: ref[lt, pl.ds(r0 + i, SUBLANES, stride=SUBLANES), :]
                a_loc, b_loc = [slab(a_ref, 0)], [slab(b_ref, 0)]
                for i in range(1, SUBLANES):
                    a_i = slab(a_ref, i)
                    b_loc.append(a_i * b_loc[-1] + slab(b_ref, i))
                    a_loc.append(a_i * a_loc[-1])
                a_e, b_e = a_loc[-1], b_loc[-1]
                for d in (1, 2, 4):
                    m = row_c >= d
                    b_e = jnp.where(m, a_e * pltpu.roll(b_e, d, axis=0) + b_e, b_e)
                    a_e = jnp.where(m, a_e * pltpu.roll(a_e, d, axis=0), a_e)
                h_end = a_e * carry + b_e
                c_in = jnp.where(row_c == 0, carry, pltpu.roll(h_end, 1, axis=0))
                for i in range(SUBLANES):
                    b_ref[lt, pl.ds(r0 + i, SUBLANES, stride=SUBLANES), :] = a_loc[i] * c_in + b_loc[i]
                carry = jnp.broadcast_to(h_end[SUBLANES - 1:SUBLANES, :], (SUBLANES, LANES))
            hc_ref[s, :, cs] = carry
            h_out_ref[s:s + 1, cs] = carry[0:1, :]
            pump(1)
    pump(len(tasks))

    lru_gate = pf_ref[:, P_LG:P_LG + D_RNN]
    h_all = jnp.concatenate([b_ref[lt] for lt in range(D_RNN // LANES)], axis=1)
    lru_y = (h_all * jax.nn.gelu(lru_gate)).astype(jnp.bfloat16)
    half_lru = _bdot(lru_y, w_lru_ref[...])
    t_gate = jnp.tanh(pf_ref[:, P_G:P_G + D_MODEL] + 0.5 * b_merge_ref[:, 0:D_MODEL])
    mixed = t_gate * half_lru + half_lru

    q = pq_ref[...]
    k = pf_ref[:, P_KV:P_KV + KV_W]
    v = pf_ref[:, P_KV + KV_W:P_KV + 2 * KV_W]
    lane_m = lax.broadcasted_iota(jnp.int32, (m_rows, LANES), 1)

    def dup_head(a, kv):
        src = a[:, (kv // 2) * LANES:(kv // 2 + 1) * LANES]
        swapped = pltpu.roll(src, HEAD_DIM, axis=1)
        first, second = (src, swapped) if kv % 2 == 0 else (swapped, src)
        return jnp.where(lane_m < HEAD_DIM, first, second).astype(jnp.bfloat16)

    k_dup = [dup_head(k, kv) for kv in range(N_KV)]
    v_dup = [dup_head(v, kv) for kv in range(N_KV)]

    lane_q = lax.broadcasted_iota(jnp.int32, (CHUNK, LANES), 1)
    lane_o = lax.broadcasted_iota(jnp.int32, (CHUNK, LANES), 1)
    key_lane = lax.broadcasted_iota(jnp.int32, (1, BAND), 1)
    row_grp = lax.broadcasted_iota(jnp.int32, (GROUP * CHUNK, 1), 0) // CHUNK
    for s in range(nseg):
        rows = slice(s * seg_len, (s + 1) * seg_len)
        for kv in range(N_KV):
            kbuf_ref[s, HIST:HIST + seg_len, kv * LANES:(kv + 1) * LANES] = k_dup[kv][rows]
            vbuf_ref[s, HIST:HIST + seg_len, kv * V_STRIDE:kv * V_STRIDE + LANES] = v_dup[kv][rows]
        k_out_ref[s] = k[s * seg_len + seg_len - keep:(s + 1) * seg_len]
        v_out_ref[s] = v[s * seg_len + seg_len - keep:(s + 1) * seg_len]
        for c in range(n_chunks):
            q_c = q[s * seg_len + c * CHUNK:s * seg_len + (c + 1) * CHUNK]
            slabs = []
            for kv in range(N_KV):
                parts = []
                for g in range(GROUP):
                    col = kv * GROUP * HEAD_DIM + (g // 2) * LANES
                    slab = q_c[:, col:col + LANES]
                    keep_lo = (g % 2) == 0
                    sel = (lane_q < HEAD_DIM) if keep_lo else (lane_q >= HEAD_DIM)
                    parts.append(jnp.where(sel, slab, jnp.zeros_like(slab)))
                q_stack = jnp.concatenate(parts, axis=0)
                k_band = kbuf_ref[s, c * CHUNK:c * CHUNK + BAND, kv * LANES:(kv + 1) * LANES]
                v_band = vbuf_ref[s, c * CHUNK:c * CHUNK + BAND, kv * V_STRIDE:(kv + 1) * V_STRIDE]
                sc = lax.dot_general(q_stack, k_band, (((1,), (1,)), ((), ())),
                                     preferred_element_type=jnp.float32)
                sc = sc + bias_ref[kv]
                if mask_history and c < LOOKBACK_CHUNKS:
                    first_valid = HIST - (step * n_chunks + c) * CHUNK
                    sc = jnp.where(key_lane >= first_valid, sc, NEG_INF)
                sink = jnp.zeros((GROUP * CHUNK, 1), jnp.float32)
                for g in range(GROUP):
                    sink = jnp.where(row_grp == g, sink_ref[kv * GROUP + g] * LOG2E, sink)
                m = jnp.maximum(jnp.max(sc, axis=-1, keepdims=True), sink)
                p = jnp.exp2(sc - m).astype(jnp.bfloat16)
                o = _bdot(p, v_band)
                denom = o[:, LANES:2 * LANES] + jnp.exp2(sink - m)
                o = o[:, 0:LANES] / denom
                for pair in range(GROUP // 2):
                    lo = o[(2 * pair) * CHUNK:(2 * pair + 1) * CHUNK]
                    hi = o[(2 * pair + 1) * CHUNK:(2 * pair + 2) * CHUNK]
                    slabs.append(jnp.where(lane_o < HEAD_DIM, lo, hi))
            attn_ref[s * seg_len + c * CHUNK:s * seg_len + (c + 1) * CHUNK, :] = (
                jnp.concatenate(slabs, axis=1).astype(jnp.bfloat16))
        kbuf_ref[s, 0:HIST, :] = kbuf_ref[s, seg_len:seg_len + HIST, :]
        vbuf_ref[s, 0:HIST, :] = vbuf_ref[s, seg_len:seg_len + HIST, :]

    attn = _bdot(attn_ref[...], w_attn_ref[...])
    gates = pf_ref[:, P_G + D_MODEL:P_G + 2 * D_MODEL] + 0.5 * b_merge_ref[:, D_MODEL:2 * D_MODEL]
    mixed = (mixed + (jnp.tanh(gates) * attn + attn)).astype(jnp.bfloat16)
    x1 = x + _bdot(mixed, w_out_ref[...])
    x1_ref[...] = x1

    xn2 = _rmsnorm(x1, norm_ffn_ref[...])
    xn2_ref[...] = _pack_bf16_pairs(xn2) if sparse_out else xn2.astype(jnp.bfloat16)
    logits = _bdot(xn2.astype(jnp.bfloat16), w_rt_ref[...]) + b_rt_ref[...]
    lane = lax.broadcasted_iota(jnp.int32, (m_rows, ROUTER_W), 1).astype(jnp.float32)
    far = jnp.float32(2 * ROUTER_W)
    is_group = (lane >= N_EXPERTS) & (lane < N_EXPERTS + N_GROUPS)
    gl = jnp.where(is_group, logits, NEG_INF)
    g_max = jnp.max(gl, axis=-1, keepdims=True)
    g_idx = jnp.min(jnp.where(gl == g_max, lane, far), axis=-1, keepdims=True) - N_EXPERTS
    g_w = 1.0 / jnp.sum(jnp.where(is_group, jnp.exp(gl - g_max), 0.0), axis=-1, keepdims=True)
    in_group = (lane < N_EXPERTS) & (jnp.floor(lane * (1.0 / EXPERTS_PER_GROUP)) == g_idx)
    el = jnp.where(in_group, logits, NEG_INF)
    e1 = jnp.max(el, axis=-1, keepdims=True)
    i1 = jnp.min(jnp.where(el == e1, lane, far), axis=-1, keepdims=True)
    el2 = jnp.where(lane == i1, NEG_INF, el)
    e2 = jnp.max(el2, axis=-1, keepdims=True)
    i2 = jnp.min(jnp.where(el2 == e2, lane, far), axis=-1, keepdims=True)
    t = jnp.exp(e2 - e1)
    w1 = g_w / (1.0 + t)
    w2 = w1 * t
    if sparse_out:
        e_lane = lane + g_idx * EXPERTS_PER_GROUP
        rec = jnp.where(e_lane == i1, w1, 0.0) + jnp.where(e_lane == i2, w2, 0.0)
        rec = jnp.where(lane < EXPERTS_PER_GROUP, rec, 0.0)
        comb_ref[...] = jnp.where(lane == GID_LANE, g_idx, rec)
    else:
        comb_ref[...] = jnp.where(lane == i1, w1, 0.0) + jnp.where(lane == i2, w2, 0.0)


def _const_spec(shape):
    zeros = (0,) * len(shape)
    return pl.BlockSpec(shape, lambda i: zeros, pipeline_mode=pl.Buffered(1))


def _mixer(x, conv0, h0, k0, v0, weights, *, nseg, seg_len, mask_history, sparse_out):
    n_tok = x.shape[0]
    m_rows = nseg * seg_len
    n_steps = n_tok // m_rows
    keep = min(seg_len, HIST)
    if n_steps == 1:
        row_spec = lambda w: pl.BlockSpec((m_rows, w), lambda i: (i, 0), pipeline_mode=pl.Buffered(1))
    else:
        row_spec = lambda w: pl.BlockSpec((m_rows, w), lambda i: (i, 0))
    in_specs = [row_spec(D_MODEL), _const_spec(conv0.shape), _const_spec(h0.shape), _const_spec(k0.shape),
                _const_spec(v0.shape)]
    for w in weights:
        if w.ndim == 1:
            in_specs.append(pl.BlockSpec(memory_space=pltpu.SMEM))
        else:
            in_specs.append(_const_spec(w.shape))
    state_spec = lambda r, w: pl.BlockSpec((nseg, r, w), lambda i: (0, 0, 0))
    out_shape = (
        jax.ShapeDtypeStruct((n_tok, D_MODEL), jnp.float32),
        (jax.ShapeDtypeStruct((n_tok, PACK_W), jnp.uint32) if sparse_out
         else jax.ShapeDtypeStruct((n_tok, D_MODEL), jnp.bfloat16)),
        jax.ShapeDtypeStruct((n_tok, ROUTER_W), jnp.float32),
        jax.ShapeDtypeStruct((nseg, CONV_W - 1, D_RNN), jnp.float32),
        jax.ShapeDtypeStruct((nseg, D_RNN), jnp.float32),
        jax.ShapeDtypeStruct((nseg, keep, KV_W), jnp.float32),
        jax.ShapeDtypeStruct((nseg, keep, KV_W), jnp.float32),
    )
    out_specs = (row_spec(D_MODEL), row_spec(PACK_W if sparse_out else D_MODEL), row_spec(ROUTER_W),
                 state_spec(CONV_W - 1, D_RNN), pl.BlockSpec((nseg, D_RNN), lambda i: (0, 0)),
                 state_spec(keep, KV_W), state_spec(keep, KV_W))
    scratch = [
        pltpu.VMEM((nseg, SUBLANES, D_RNN), jnp.float32),
        pltpu.VMEM((nseg, SUBLANES, D_RNN), jnp.float32),
        pltpu.VMEM((nseg, HIST + seg_len, KV2_W), jnp.bfloat16),
        pltpu.VMEM((nseg, HIST + seg_len, N_KV * V_STRIDE), jnp.bfloat16),
        pltpu.VMEM((D_RNN // LANES, m_rows, LANES), jnp.float32),
        pltpu.VMEM((D_RNN // LANES, m_rows, LANES), jnp.float32),
        pltpu.VMEM((m_rows, Q_W), jnp.bfloat16),
        pltpu.VMEM((m_rows, D_MODEL), jnp.bfloat16),
        pltpu.VMEM((m_rows, P_W), jnp.float32),
        pltpu.VMEM((m_rows, Q_W), jnp.bfloat16),
        pltpu.VMEM((N_KV, GROUP * CHUNK, BAND), jnp.float32),
    ]
    return pl.pallas_call(
        functools.partial(_mixer_kernel, nseg, seg_len, mask_history, sparse_out),
        grid=(n_steps,),
        in_specs=in_specs,
        out_specs=out_specs,
        out_shape=out_shape,
        scratch_shapes=scratch,
        compiler_params=pltpu.CompilerParams(dimension_semantics=("arbitrary",), vmem_limit_bytes=VMEM_LIMIT),
        name="mixer_prompt" if mask_history else "mixer_sample",
    )(x, conv0, h0, k0, v0, *weights)


def _group_swiglu(x, cols, wg_ref, wu_ref, wd_ref):
    parts = []
    for e in range(EXPERTS_PER_GROUP):
        h = jax.nn.silu(_bdot(x, wg_ref[e])) * _bdot(x, wu_ref[e])
        parts.append((h * cols[e]).astype(jnp.bfloat16))
    return _bdot(jnp.concatenate(parts, axis=1), wd_ref[...])


def _moe_groups_kernel(xn2_ref, comb_ref, x1_ref, wg_ref, wu_ref, wd_ref, norm_ref, y_ref, acc_ref):
    g = pl.program_id(1)

    @pl.when(g == 0)
    def _():
        acc_ref[...] = jnp.zeros_like(acc_ref)

    comb = comb_ref[...]
    lane = lax.broadcasted_iota(jnp.int32, comb.shape, 1)
    cols = [jnp.sum(jnp.where(lane == g * EXPERTS_PER_GROUP + e, comb, 0.0), axis=-1, keepdims=True)
            for e in range(EXPERTS_PER_GROUP)]
    acc_ref[...] += _group_swiglu(xn2_ref[...], cols, wg_ref, wu_ref, wd_ref)

    @pl.when(g == N_GROUPS - 1)
    def _():
        y_ref[...] = _rmsnorm(x1_ref[...] + acc_ref[...], norm_ref[...])


def _group_weight_specs(group_of):
    return [pl.BlockSpec((EXPERTS_PER_GROUP, D_MODEL, D_EXPERT), lambda *a: (group_of(*a), 0, 0)),
            pl.BlockSpec((EXPERTS_PER_GROUP, D_MODEL, D_EXPERT), lambda *a: (group_of(*a), 0, 0)),
            pl.BlockSpec((None, GROUP_FF, D_MODEL), lambda *a: (group_of(*a), 0, 0))]


def _moe_groups(xn2, comb, x1, wg, wu, wd, norm_final, *, tile):
    n_tok = xn2.shape[0]
    row_spec = lambda w: pl.BlockSpec((tile, w), lambda i, g: (i, 0))
    return pl.pallas_call(
        _moe_groups_kernel,
        grid=(n_tok // tile, N_GROUPS),
        in_specs=[row_spec(D_MODEL), row_spec(ROUTER_W), row_spec(D_MODEL),
                  *_group_weight_specs(lambda i, g: g),
                  pl.BlockSpec((1, D_MODEL), lambda i, g: (0, 0))],
        out_specs=row_spec(D_MODEL),
        out_shape=jax.ShapeDtypeStruct((n_tok, D_MODEL), jnp.float32),
        scratch_shapes=[pltpu.VMEM((tile, D_MODEL), jnp.float32)],
        compiler_params=pltpu.CompilerParams(dimension_semantics=("arbitrary", "arbitrary"),
                                             vmem_limit_bytes=VMEM_LIMIT),
        name="moe_groups",
    )(xn2, comb, x1, wg, wu, wd, norm_final)


def _scatter_rows_kernel(tile, pos_ref, x_ref, rec_ref, wg_ref, wu_ref, wd_ref,
                         xs_ref, recs_ref, wg_out_ref, wu_out_ref, wd_out_ref):
    i = pl.program_id(0)

    @pl.when(i == 0)
    def _():
        xs_ref[...] = jnp.zeros_like(xs_ref)
        recs_ref[...] = jnp.zeros_like(recs_ref)

    wg_out_ref[...] = wg_ref[...].astype(jnp.bfloat16)
    wu_out_ref[...] = wu_ref[...].astype(jnp.bfloat16)
    wd_out_ref[...] = wd_ref[...].astype(jnp.bfloat16)

    def body(j, carry):
        r0 = pl.multiple_of(j * SUBLANES, SUBLANES)
        xb = x_ref[pl.ds(r0, SUBLANES), :]
        rb = rec_ref[pl.ds(r0, SUBLANES), :]
        for k in range(SUBLANES):
            p = pos_ref[i * tile + j * SUBLANES + k]
            xs_ref[pl.ds(p, 1), :] = xb[k:k + 1, :]
            recs_ref[pl.ds(p, 1), :] = rb[k:k + 1, :]
        return carry

    lax.fori_loop(0, tile // SUBLANES, body, 0)


def _scatter_rows(pos, xn2p, rec, w_gate, w_up, w_down, n_slots, *, tile):
    n_tok = xn2p.shape[0]
    n_steps = n_tok // tile
    per_step = N_EXPERTS // n_steps
    assert per_step * n_steps == N_EXPERTS
    w_spec = lambda a: pl.BlockSpec((per_step,) + a.shape[1:], lambda i, pos: (i, 0, 0))
    return pl.pallas_call(
        functools.partial(_scatter_rows_kernel, tile),
        grid_spec=pltpu.PrefetchScalarGridSpec(
            num_scalar_prefetch=1,
            grid=(n_steps,),
            in_specs=[pl.BlockSpec((tile, PACK_W), lambda i, pos: (i, 0)),
                      pl.BlockSpec((tile, ROUTER_W), lambda i, pos: (i, 0)),
                      w_spec(w_gate), w_spec(w_up), w_spec(w_down)],
            out_specs=[pl.BlockSpec((n_slots, PACK_W), lambda i, pos: (0, 0)),
                       pl.BlockSpec((n_slots, ROUTER_W), lambda i, pos: (0, 0)),
                       w_spec(w_gate), w_spec(w_up), w_spec(w_down)],
        ),
        out_shape=(jax.ShapeDtypeStruct((n_slots, PACK_W), jnp.uint32),
                   jax.ShapeDtypeStruct((n_slots, ROUTER_W), jnp.float32),
                   jax.ShapeDtypeStruct(w_gate.shape, jnp.bfloat16),
                   jax.ShapeDtypeStruct(w_up.shape, jnp.bfloat16),
                   jax.ShapeDtypeStruct(w_down.shape, jnp.bfloat16)),
        compiler_params=pltpu.CompilerParams(dimension_semantics=("arbitrary",), vmem_limit_bytes=VMEM_LIMIT),
        name="moe_scatter",
    )(pos, xn2p, rec, w_gate, w_up, w_down)


def _moe_sorted_kernel(tg_ref, xs_ref, recs_ref, wg_ref, wu_ref, wd_ref, o_ref):
    x = _unpack_bf16_pairs(xs_ref[...]).astype(jnp.bfloat16)
    rec = recs_ref[...]
    cols = [rec[:, e:e + 1] for e in range(EXPERTS_PER_GROUP)]
    o_ref[...] = _pack_bf16_pairs(_group_swiglu(x, cols, wg_ref, wu_ref, wd_ref))


def _moe_sorted(tile_group, xs, recs, wg, wu, wd, *, tile):
    n_slots = xs.shape[0]
    return pl.pallas_call(
        _moe_sorted_kernel,
        grid_spec=pltpu.PrefetchScalarGridSpec(
            num_scalar_prefetch=1,
            grid=(n_slots // tile,),
            in_specs=[pl.BlockSpec((tile, PACK_W), lambda i, tg: (i, 0)),
                      pl.BlockSpec((tile, ROUTER_W), lambda i, tg: (i, 0)),
                      *_group_weight_specs(lambda i, tg: tg[i])],
            out_specs=pl.BlockSpec((tile, PACK_W), lambda i, tg: (i, 0)),
        ),
        out_shape=jax.ShapeDtypeStruct((n_slots, PACK_W), jnp.uint32),
        compiler_params=pltpu.CompilerParams(dimension_semantics=("arbitrary",), vmem_limit_bytes=VMEM_LIMIT),
        name="moe_sorted",
    )(tile_group, xs, recs, wg, wu, wd)


def _gather_norm_kernel(tile, pos_ref, os_ref, x1_ref, norm_ref, y_ref, buf_ref):
    i = pl.program_id(0)

    def body(j, carry):
        r0 = pl.multiple_of(j * SUBLANES, SUBLANES)
        rows = [os_ref[pl.ds(pos_ref[i * tile + j * SUBLANES + k], 1), :] for k in range(SUBLANES)]
        buf_ref[pl.ds(r0, SUBLANES), :] = jnp.concatenate(rows, axis=0)
        return carry

    lax.fori_loop(0, tile // SUBLANES, body, 0)
    y_ref[...] = _rmsnorm(x1_ref[...] + _unpack_bf16_pairs(buf_ref[...]), norm_ref[...])


def _gather_norm(pos, o_sorted, x1, norm_final, *, tile):
    n_tok = x1.shape[0]
    n_slots = o_sorted.shape[0]
    return pl.pallas_call(
        functools.partial(_gather_norm_kernel, tile),
        grid_spec=pltpu.PrefetchScalarGridSpec(
            num_scalar_prefetch=1,
            grid=(n_tok // tile,),
            in_specs=[pl.BlockSpec((n_slots, PACK_W), lambda i, pos: (0, 0), pipeline_mode=pl.Buffered(1)),
                      pl.BlockSpec((tile, D_MODEL), lambda i, pos: (i, 0)),
                      pl.BlockSpec((1, D_MODEL), lambda i, pos: (0, 0))],
            out_specs=pl.BlockSpec((tile, D_MODEL), lambda i, pos: (i, 0)),
            scratch_shapes=[pltpu.VMEM((tile, PACK_W), jnp.uint32)],
        ),
        out_shape=jax.ShapeDtypeStruct((n_tok, D_MODEL), jnp.float32),
        compiler_params=pltpu.CompilerParams(dimension_semantics=("arbitrary",), vmem_limit_bytes=VMEM_LIMIT),
        name="moe_gather_norm",
    )(pos, o_sorted, x1, norm_final)


def _sorted_slots(rec, tile):
    n_tok = rec.shape[0]
    gid = rec[:, GID_LANE].astype(jnp.int32)
    onehot = (gid[:, None] == jnp.arange(N_GROUPS, dtype=jnp.int32)[None, :]).astype(jnp.int32)
    csum = jnp.cumsum(onehot, axis=0)
    rank = jnp.sum((csum - onehot) * onehot, axis=1)
    padded = ((csum[-1] + tile - 1) // tile) * tile
    end = jnp.cumsum(padded)
    pos = jnp.sum(onehot * (end - padded)[None, :], axis=1) + rank
    n_tiles = n_tok // tile + N_GROUPS
    tile_start = jnp.arange(n_tiles, dtype=jnp.int32) * tile
    tile_group = jnp.minimum(jnp.sum((tile_start[:, None] >= end[None, :]).astype(jnp.int32), axis=1), N_GROUPS - 1)
    return pos.astype(jnp.int32), tile_group.astype(jnp.int32), n_tiles * tile


def _rel_bucket(rel):
    nb = REL_BUCKETS // 2
    n = -rel
    ret = jnp.where(n < 0, nb, 0)
    n = jnp.abs(n)
    max_exact = nb // 2
    nf = jnp.maximum(n, 1).astype(jnp.float32)
    large = max_exact + (jnp.log(nf / max_exact) / math.log(REL_MAX_DIST / max_exact) * (nb - max_exact)).astype(jnp.int32)
    large = jnp.minimum(large, nb - 1)
    return ret + jnp.where(n < max_exact, n, large)


def _bias_by_rel(rel_table):
    rels = jnp.arange(-(BAND - 1), CHUNK + 1, dtype=jnp.int32)
    return jnp.transpose(rel_table[_rel_bucket(rels)]).astype(jnp.float32)


def _dup_heads(a):
    a = a.reshape(a.shape[:-1] + (N_KV, HEAD_DIM))
    return jnp.concatenate([a, a], axis=-1).reshape(a.shape[:-2] + (KV2_W,))


def _block_diag_groups(w):
    per = RG_GROUP // LRU_BLOCK_W
    blk = w.reshape(N_RG_GROUPS, per, LRU_BLOCK_W, 1, LRU_BLOCK_W)
    eye = jnp.eye(per, dtype=w.dtype).reshape(1, per, 1, per, 1)
    return (blk * eye).reshape(N_RG_GROUPS, RG_GROUP, RG_GROUP)


def kernel(x_prompt, x_sample, state_lru_h, state_lru_conv, cache_swa_k, cache_swa_v, norm_mix, w_in, b_merge, conv_w, conv_b, w_rg_a, b_rg_a, w_rg_x, b_rg_x, lru_lambda, attn_sink, rel_bias, w_lru_proj, w_attn_proj, w_out, norm_ffn, w_group, b_group, w_router, b_router, w_e_gate, w_e_up, w_e_down, norm_final):
    f32, bf16 = jnp.float32, jnp.bfloat16
    l = 0
    col_scale = jnp.where(jnp.arange(w_in.shape[-1]) >= C_G, 0.5, 1.0).astype(f32)
    w = w_in[l] * col_scale[None, :]
    w_rg = (0.5 * jnp.concatenate([_block_diag_groups(w_rg_a[l]), _block_diag_groups(w_rg_x[l])], axis=2)).astype(bf16)
    w_rt = jnp.concatenate([w_router[l], w_group[l],
                            jnp.zeros((D_MODEL, ROUTER_W - N_EXPERTS - N_GROUPS), f32)], axis=1).astype(bf16)
    b_rt = jnp.concatenate([b_router[l], b_group[l], jnp.zeros((ROUTER_W - N_EXPERTS - N_GROUPS,), f32)])[None, :]
    weights = (
        norm_mix[l][None, :], w.astype(bf16), b_merge[l][None, :], conv_w[l], conv_b[l][None, :], w_rg,
        b_rg_a[l][None, :], b_rg_x[l][None, :], lru_lambda[l][None, :], attn_sink[l], _bias_by_rel(rel_bias) * LOG2E,
        (0.5 * w_lru_proj[l]).astype(bf16), (0.5 * w_attn_proj[l]).astype(bf16), w_out[l].astype(bf16),
        norm_ffn[l][None, :],
        w_rt, b_rt,
    )
    nf = norm_final[None, :]

    bp, tp, _ = x_prompt.shape
    assert bp == 1
    xp = x_prompt.reshape(tp, D_MODEL)
    zero_rows = jnp.zeros((1, SUBLANES, D_RNN), f32)
    zero_kv = jnp.zeros((1, HIST, KV2_W), bf16)
    p_x1, p_xn2, p_comb, p_conv, p_h, p_k, p_v = _mixer(
        xp, zero_rows, zero_rows, zero_kv, zero_kv, weights,
        nseg=1, seg_len=PROMPT_TILE, mask_history=True, sparse_out=True)
    pos, tile_group, n_slots = _sorted_slots(p_comb, SORT_TILE)
    x_sorted, rec_sorted, wg, wu, wd = _scatter_rows(pos, p_xn2, p_comb, w_e_gate[l], w_e_up[l], w_e_down[l],
                                                     n_slots, tile=SORT_TILE)
    wd = wd.reshape(N_GROUPS, GROUP_FF, D_MODEL)
    o_sorted = _moe_sorted(tile_group, x_sorted, rec_sorted, wg, wu, wd, tile=SORT_TILE)
    y_p = _gather_norm(pos, o_sorted, p_x1, nf, tile=GATHER_TILE).reshape(x_prompt.shape)

    bs, ts, _ = x_sample.shape
    xs = x_sample.reshape(bs * ts, D_MODEL)
    conv0 = jnp.pad(state_lru_conv[l], ((0, 0), (SUBLANES - (CONV_W - 1), 0), (0, 0)))
    h0 = jnp.broadcast_to(state_lru_h[l][:, None, :], (bs, SUBLANES, D_RNN))
    ck = cache_swa_k[l].reshape(bs, -1, KV_W)
    cv = cache_swa_v[l].reshape(bs, -1, KV_W)
    s_x1, s_xn2, s_comb, s_conv, s_h, s_k, s_v = _mixer(
        xs, conv0, h0, _dup_heads(ck).astype(bf16), _dup_heads(cv).astype(bf16), weights,
        nseg=bs, seg_len=ts, mask_history=False, sparse_out=False)
    y_s = _moe_groups(s_xn2, s_comb, s_x1, wg, wu, wd, nf, tile=bs * ts).reshape(x_sample.shape)

    p_lru_h = p_h[None]
    p_lru_conv = p_conv[None]
    heads = lambda a: a.reshape(a.shape[:-1] + (N_KV, HEAD_DIM))
    p_swa_k = heads(p_k)[None]
    p_swa_v = heads(p_v)[None]
    s_lru_h = s_h[None]
    s_lru_conv = s_conv[None]
    s_swa_k = jnp.concatenate([cache_swa_k[l][:, ts:], heads(s_k)], axis=1)[None]
    s_swa_v = jnp.concatenate([cache_swa_v[l][:, ts:], heads(s_v)], axis=1)[None]
    return (y_p, y_s, p_lru_h, p_lru_conv, p_swa_k, p_swa_v, s_lru_h, s_lru_conv, s_swa_k, s_swa_v)
```

```python
import functools
import math

import jax
import jax.numpy as jnp
from jax import lax
from jax.experimental import pallas as pl
from jax.experimental.pallas import tpu as pltpu

D_MODEL = 1024
CHUNK = 64
D_RNN = D_MODEL
CONV_W = 4
LRU_BLOCKS = 16
LRU_BLOCK_W = D_RNN // LRU_BLOCKS
LRU_C = 8.0
N_HEADS = 16
N_KV = 4
HEAD_DIM = 64
GROUP = N_HEADS // N_KV
WINDOW = 128
LOOKBACK_CHUNKS = -(-WINDOW // CHUNK)
HIST = LOOKBACK_CHUNKS * CHUNK
BAND = HIST + CHUNK
Q_W = N_HEADS * HEAD_DIM
KV_W = N_KV * HEAD_DIM
REL_BUCKETS = 32
REL_MAX_DIST = 128
N_GROUPS = 4
EXPERTS_PER_GROUP = 8
N_EXPERTS = N_GROUPS * EXPERTS_PER_GROUP
D_EXPERT = 256
EPS = 1e-6
NEG_INF = -1e30

LANES = 128
SUBLANES = 8
KV2_W = N_KV * LANES
RG_GROUP = 256
N_RG_GROUPS = D_RNN // RG_GROUP
ROUTER_W = LANES
VMEM_LIMIT = 60 * 1024 * 1024
PACK_W = D_MODEL // 2
GROUP_FF = EXPERTS_PER_GROUP * D_EXPERT
GID_LANE = EXPERTS_PER_GROUP
SORT_TILE = 512
GATHER_TILE = 1024
PROMPT_TILE = 512

C_LX, C_LG, C_Q = 0, D_RNN, 2 * D_RNN
C_K = C_Q + Q_W
C_V = C_K + KV_W
C_G = C_V + KV_W
P_LG, P_G = 0, D_RNN
P_KV = P_G + 2 * D_MODEL
P_W = P_KV + 2 * KV_W
PROJ_BLOCK = 256
V_STRIDE = 2 * LANES
SCAN_ROWS = SUBLANES * SUBLANES
LOG2E = math.log2(math.e)
REL_SPAN = BAND + CHUNK


def _rmsnorm(x, g):
    return x * lax.rsqrt(jnp.mean(x * x, axis=-1, keepdims=True) + EPS) * g


def _bdot(a, b):
    return jnp.dot(a, b, preferred_element_type=jnp.float32)


def _pack_bf16_pairs(x):
    return pltpu.pack_elementwise([x[:, :PACK_W], x[:, PACK_W:]], packed_dtype=jnp.bfloat16)


def _unpack_bf16_pairs(p):
    lo = pltpu.unpack_elementwise(p, index=0, packed_dtype=jnp.bfloat16, unpacked_dtype=jnp.float32)
    hi = pltpu.unpack_elementwise(p, index=1, packed_dtype=jnp.bfloat16, unpacked_dtype=jnp.float32)
    return jnp.concatenate([lo, hi], axis=1)


def _mixer_kernel(nseg, seg_len, mask_history, sparse_out,
                  x_ref, conv0_ref, h0_ref, k0_ref, v0_ref,
                  norm_mix_ref, w_in_ref, b_merge_ref, conv_w_ref, conv_b_ref, w_rg_ref, b_rg_a_ref, b_rg_x_ref,
                  lam_ref, sink_ref, byrel_ref, w_lru_ref, w_attn_ref, w_out_ref, norm_ffn_ref, w_rt_ref, b_rt_ref,
                  x1_ref, xn2_ref, comb_ref, conv_out_ref, h_out_ref, k_out_ref, v_out_ref,
                  xp_ref, hc_ref, kbuf_ref, vbuf_ref, a_ref, b_ref, attn_ref, xn_ref, pf_ref, pq_ref, bias_ref):
    step = pl.program_id(0)
    m_rows = nseg * seg_len
    n_chunks = seg_len // CHUNK
    keep = min(seg_len, HIST)

    @pl.when(step == 0)
    def _():
        xp_ref[...] = conv0_ref[...]
        hc_ref[...] = h0_ref[...]
        kbuf_ref[:, 0:HIST, :] = k0_ref[...]
        vbuf_ref[:, :, :] = jnp.ones(vbuf_ref.shape, jnp.bfloat16)
        for kv in range(N_KV):
            vbuf_ref[:, 0:HIST, kv * V_STRIDE:kv * V_STRIDE + LANES] = v0_ref[:, :, kv * LANES:(kv + 1) * LANES]
        for h in range(N_HEADS):
            tiled = jnp.broadcast_to(byrel_ref[h:h + 1, :], (CHUNK, REL_SPAN))
            block = pltpu.roll(tiled, REL_SPAN - (CHUNK - 1), axis=1, stride=1, stride_axis=0)
            bias_ref[h // GROUP, (h % GROUP) * CHUNK:(h % GROUP + 1) * CHUNK, :] = block[:, 0:BAND]

    x = x_ref[...]
    xn_ref[...] = _rmsnorm(x, norm_mix_ref[...]).astype(jnp.bfloat16)

    tasks = []

    def f32_block(src_ref, c_src, c_dst):
        def run():
            pf_ref[:, c_dst:c_dst + PROJ_BLOCK] = _bdot(xn_ref[...], src_ref[:, c_src:c_src + PROJ_BLOCK])
        return run

    def q_block(c):
        def run():
            pq_ref[:, c:c + PROJ_BLOCK] = (_bdot(xn_ref[...], w_in_ref[:, C_Q + c:C_Q + c + PROJ_BLOCK])
                                           * (HEAD_DIM ** -0.5 * LOG2E)).astype(jnp.bfloat16)
        return run

    for c in range(0, D_RNN, PROJ_BLOCK):
        tasks.append(f32_block(w_in_ref, C_LG + c, P_LG + c))
    for c in range(0, 2 * D_MODEL, PROJ_BLOCK):
        tasks.append(f32_block(w_in_ref, C_G + c, P_G + c))
    for c in range(0, 2 * KV_W, PROJ_BLOCK):
        tasks.append(f32_block(w_in_ref, C_K + c, P_KV + c))
    for c in range(0, Q_W, PROJ_BLOCK):
        tasks.append(q_block(c))

    def pump(n):
        for _ in range(min(n, len(tasks))):
            tasks.pop(0)()

    lru_x = _bdot(xn_ref[...], w_in_ref[:, C_LX:C_LX + D_RNN])
    conv_w = conv_w_ref[...]
    row = lax.broadcasted_iota(jnp.int32, (SUBLANES, D_RNN), 0)
    xc_parts = []
    for s in range(nseg):
        xs = lru_x[s * seg_len:(s + 1) * seg_len, :]
        prev = xp_ref[s]
        tail = xs[seg_len - SUBLANES:, :]
        acc = conv_b_ref[...] + xs * conv_w[CONV_W - 1:CONV_W, :]
        pump(1)
        for k in range(1, CONV_W):
            shifted = pltpu.roll(xs, k, axis=0)
            first = jnp.where(row < k, pltpu.roll(prev, k, axis=0), shifted[0:SUBLANES])
            shifted = jnp.concatenate([first, shifted[SUBLANES:]], axis=0)
            acc = acc + shifted * conv_w[CONV_W - 1 - k:CONV_W - k, :]
            pump(1)
        xc_parts.append(acc)
        conv_out_ref[s] = tail[SUBLANES - (CONV_W - 1):, :]
        xp_ref[s] = tail
    xc = xc_parts[0] if nseg == 1 else jnp.concatenate(xc_parts, axis=0)
    xc_b = xc.astype(jnp.bfloat16)

    lam = lam_ref[...]
    log_sig = jnp.minimum(lam, 0.0) - jnp.log1p(jnp.exp(-jnp.abs(lam)))
    c8_half = (0.5 * LRU_C) * log_sig
    for j in range(N_RG_GROUPS):
        cs = slice(j * RG_GROUP, (j + 1) * RG_GROUP)
        pre = _bdot(xc_b[:, cs], w_rg_ref[j])
        t_r = jnp.tanh(pre[:, :RG_GROUP] + 0.5 * b_rg_a_ref[:, cs])
        i = 0.5 * jnp.tanh(pre[:, RG_GROUP:] + 0.5 * b_rg_x_ref[:, cs]) + 0.5
        log_a = c8_half[:, cs] * t_r + c8_half[:, cs]
        a_val = jnp.exp(log_a)
        th = jnp.tanh(log_a)
        b_val = jnp.sqrt(-2.0 * th / (1.0 - th)) * i * xc[:, cs]
        for t in range(RG_GROUP // LANES):
            lt = j * (RG_GROUP // LANES) + t
            a_ref[lt] = a_val[:, t * LANES:(t + 1) * LANES]
            b_ref[lt] = b_val[:, t * LANES:(t + 1) * LANES]
        pump(2)

    row_c = lax.broadcasted_iota(jnp.int32, (SUBLANES, LANES), 0)
    for s in range(nseg):
        for lt in range(D_RNN // LANES):
            cs = slice(lt * LANES, (lt + 1) * LANES)
            carry = hc_ref[s, :, cs]
            for blk in range(seg_len // SCAN_ROWS):
                r0 = s * seg_len + blk * SCAN_ROWS
                slab = lambda ref, i: ref[lt, pl.ds(r0 + i, SUBLANES, stride=SUBLANES), :]
                a_loc, b_loc = [slab(a_ref, 0)], [slab(b_ref, 0)]
                for i in range(1, SUBLANES):
                    a_i = slab(a_ref, i)
                    b_loc.append(a_i * b_loc[-1] + slab(b_ref, i))
                    a_loc.append(a_i * a_loc[-1])
                a_e, b_e = a_loc[-1], b_loc[-1]
                for d in (1, 2, 4):
                    m = row_c >= d
                    b_e = jnp.where(m, a_e * pltpu.roll(b_e, d, axis=0) + b_e, b_e)
                    a_e = jnp.where(m, a_e * pltpu.roll(a_e, d, axis=0), a_e)
                h_end = a_e * carry + b_e
                c_in = jnp.where(row_c == 0, carry, pltpu.roll(h_end, 1, axis=0))
                for i in range(SUBLANES):
                    b_ref[lt, pl.ds(r0 + i, SUBLANES, stride=SUBLANES), :] = a_loc[i] * c_in + b_loc[i]
                carry = jnp.broadcast_to(h_end[SUBLANES - 1:SUBLANES, :], (SUBLANES, LANES))
            hc_ref[s, :, cs] = carry
            h_out_ref[s:s + 1, cs] = carry[0:1, :]
            pump(1)
    pump(len(tasks))

    lru_gate = pf_ref[:, P_LG:P_LG + D_RNN]
    h_all = jnp.concatenate([b_ref[lt] for lt in range(D_RNN // LANES)], axis=1)
    lru_y = (h_all * jax.nn.gelu(lru_gate)).astype(jnp.bfloat16)
    half_lru = _bdot(lru_y, w_lru_ref[...])
    t_gate = jnp.tanh(pf_ref[:, P_G:P_G + D_MODEL] + 0.5 * b_merge_ref[:, 0:D_MODEL])
    mixed = t_gate * half_lru + half_lru

    q = pq_ref[...]
    k = pf_ref[:, P_KV:P_KV + KV_W]
    v = pf_ref[:, P_KV + KV_W:P_KV + 2 * KV_W]
    lane_m = lax.broadcasted_iota(jnp.int32, (m_rows, LANES), 1)

    def dup_head(a, kv):
        src = a[:, (kv // 2) * LANES:(kv // 2 + 1) * LANES]
        swapped = pltpu.roll(src, HEAD_DIM, axis=1)
        first, second = (src, swapped) if kv % 2 == 0 else (swapped, src)
        return jnp.where(lane_m < HEAD_DIM, first, second).astype(jnp.bfloat16)

    k_dup = [dup_head(k, kv) for kv in range(N_KV)]
    v_dup = [dup_head(v, kv) for kv in range(N_KV)]

    lane_q = lax.broadcasted_iota(jnp.int32, (CHUNK, LANES), 1)
    lane_o = lax.broadcasted_iota(jnp.int32, (CHUNK, LANES), 1)
    key_lane = lax.broadcasted_iota(jnp.int32, (1, BAND), 1)
    row_grp = lax.broadcasted_iota(jnp.int32, (GROUP * CHUNK, 1), 0) // CHUNK
    for s in range(nseg):
        rows = slice(s * seg_len, (s + 1) * seg_len)
        for kv in range(N_KV):
            kbuf_ref[s, HIST:HIST + seg_len, kv * LANES:(kv + 1) * LANES] = k_dup[kv][rows]
            vbuf_ref[s, HIST:HIST + seg_len, kv * V_STRIDE:kv * V_STRIDE + LANES] = v_dup[kv][rows]
        k_out_ref[s] = k[s * seg_len + seg_len - keep:(s + 1) * seg_len]
        v_out_ref[s] = v[s * seg_len + seg_len - keep:(s + 1) * seg_len]
        for c in range(n_chunks):
            q_c = q[s * seg_len + c * CHUNK:s * seg_len + (c + 1) * CHUNK]
            slabs = []
            for kv in range(N_KV):
                parts = []
                for g in range(GROUP):
                    col = kv * GROUP * HEAD_DIM + (g // 2) * LANES
                    slab = q_c[:, col:col + LANES]
                    keep_lo = (g % 2) == 0
                    sel = (lane_q < HEAD_DIM) if keep_lo else (lane_q >= HEAD_DIM)
                    parts.append(jnp.where(sel, slab, jnp.zeros_like(slab)))
                q_stack = jnp.concatenate(parts, axis=0)
                k_band = kbuf_ref[s, c * CHUNK:c * CHUNK + BAND, kv * LANES:(kv + 1) * LANES]
                v_band = vbuf_ref[s, c * CHUNK:c * CHUNK + BAND, kv * V_STRIDE:(kv + 1) * V_STRIDE]
                sc = lax.dot_general(q_stack, k_band, (((1,), (1,)), ((), ())),
                                     preferred_element_type=jnp.float32)
                sc = sc + bias_ref[kv]
                if mask_history and c < LOOKBACK_CHUNKS:
                    first_valid = HIST - (step * n_chunks + c) * CHUNK
                    sc = jnp.where(key_lane >= first_valid, sc, NEG_INF)
                sink = jnp.zeros((GROUP * CHUNK, 1), jnp.float32)
                for g in range(GROUP):
                    sink = jnp.where(row_grp == g, sink_ref[kv * GROUP + g] * LOG2E, sink)
                m = jnp.maximum(jnp.max(sc, axis=-1, keepdims=True), sink)
                p = jnp.exp2(sc - m).astype(jnp.bfloat16)
                o = _bdot(p, v_band)
                denom = o[:, LANES:2 * LANES] + jnp.exp2(sink - m)
                o = o[:, 0:LANES] / denom
                for pair in range(GROUP // 2):
                    lo = o[(2 * pair) * CHUNK:(2 * pair + 1) * CHUNK]
                    hi = o[(2 * pair + 1) * CHUNK:(2 * pair + 2) * CHUNK]
                    slabs.append(jnp.where(lane_o < HEAD_DIM, lo, hi))
            attn_ref[s * seg_len + c * CHUNK:s * seg_len + (c + 1) * CHUNK, :] = (
                jnp.concatenate(slabs, axis=1).astype(jnp.bfloat16))
        kbuf_ref[s, 0:HIST, :] = kbuf_ref[s, seg_len:seg_len + HIST, :]
        vbuf_ref[s, 0:HIST, :] = vbuf_ref[s, seg_len:seg_len + HIST, :]

    attn = _bdot(attn_ref[...], w_attn_ref[...])
    gates = pf_ref[:, P_G + D_MODEL:P_G + 2 * D_MODEL] + 0.5 * b_merge_ref[:, D_MODEL:2 * D_MODEL]
    mixed = (mixed + (jnp.tanh(gates) * attn + attn)).astype(jnp.bfloat16)
    x1 = x + _bdot(mixed, w_out_ref[...])
    x1_ref[...] = x1

    xn2 = _rmsnorm(x1, norm_ffn_ref[...])
    xn2_ref[...] = _pack_bf16_pairs(xn2) if sparse_out else xn2.astype(jnp.bfloat16)
    logits = _bdot(xn2.astype(jnp.bfloat16), w_rt_ref[...]) + b_rt_ref[...]
    lane = lax.broadcasted_iota(jnp.int32, (m_rows, ROUTER_W), 1).astype(jnp.float32)
    far = jnp.float32(2 * ROUTER_W)
    is_group = (lane >= N_EXPERTS) & (lane < N_EXPERTS + N_GROUPS)
    gl = jnp.where(is_group, logits, NEG_INF)
    g_max = jnp.max(gl, axis=-1, keepdims=True)
    g_idx = jnp.min(jnp.where(gl == g_max, lane, far), axis=-1, keepdims=True) - N_EXPERTS
    g_w = 1.0 / jnp.sum(jnp.where(is_group, jnp.exp(gl - g_max), 0.0), axis=-1, keepdims=True)
    in_group = (lane < N_EXPERTS) & (jnp.floor(lane * (1.0 / EXPERTS_PER_GROUP)) == g_idx)
    el = jnp.where(in_group, logits, NEG_INF)
    e1 = jnp.max(el, axis=-1, keepdims=True)
    i1 = jnp.min(jnp.where(el == e1, lane, far), axis=-1, keepdims=True)
    el2 = jnp.where(lane == i1, NEG_INF, el)
    e2 = jnp.max(el2, axis=-1, keepdims=True)
    i2 = jnp.min(jnp.where(el2 == e2, lane, far), axis=-1, keepdims=True)
    t = jnp.exp(e2 - e1)
    w1 = g_w / (1.0 + t)
    w2 = w1 * t
    if sparse_out:
        e_lane = lane + g_idx * EXPERTS_PER_GROUP
        rec = jnp.where(e_lane == i1, w1, 0.0) + jnp.where(e_lane == i2, w2, 0.0)
        rec = jnp.where(lane < EXPERTS_PER_GROUP, rec, 0.0)
        comb_ref[...] = jnp.where(lane == GID_LANE, g_idx, rec)
    else:
        comb_ref[...] = jnp.where(lane == i1, w1, 0.0) + jnp.where(lane == i2, w2, 0.0)


def _const_spec(shape):
    zeros = (0,) * len(shape)
    return pl.BlockSpec(shape, lambda i: zeros, pipeline_mode=pl.Buffered(1))


def _mixer(x, conv0, h0, k0, v0, weights, *, nseg, seg_len, mask_history, sparse_out):
    n_tok = x.shape[0]
    m_rows = nseg * seg_len
    n_steps = n_tok // m_rows
    keep = min(seg_len, HIST)
    if n_steps == 1:
        row_spec = lambda w: pl.BlockSpec((m_rows, w), lambda i: (i, 0), pipeline_mode=pl.Buffered(1))
    else:
        row_spec = lambda w: pl.BlockSpec((m_rows, w), lambda i: (i, 0))
    in_specs = [row_spec(D_MODEL), _const_spec(conv0.shape), _const_spec(h0.shape), _const_spec(k0.shape),
                _const_spec(v0.shape)]
    for w in weights:
        if w.ndim == 1:
            in_specs.append(pl.BlockSpec(memory_space=pltpu.SMEM))
        else:
            in_specs.append(_const_spec(w.shape))
    state_spec = lambda r, w: pl.BlockSpec((nseg, r, w), lambda i: (0, 0, 0))
    out_shape = (
        jax.ShapeDtypeStruct((n_tok, D_MODEL), jnp.float32),
        (jax.ShapeDtypeStruct((n_tok, PACK_W), jnp.uint32) if sparse_out
         else jax.ShapeDtypeStruct((n_tok, D_MODEL), jnp.bfloat16)),
        jax.ShapeDtypeStruct((n_tok, ROUTER_W), jnp.float32),
        jax.ShapeDtypeStruct((nseg, CONV_W - 1, D_RNN), jnp.float32),
        jax.ShapeDtypeStruct((nseg, D_RNN), jnp.float32),
        jax.ShapeDtypeStruct((nseg, keep, KV_W), jnp.float32),
        jax.ShapeDtypeStruct((nseg, keep, KV_W), jnp.float32),
    )
    out_specs = (row_spec(D_MODEL), row_spec(PACK_W if sparse_out else D_MODEL), row_spec(ROUTER_W),
                 state_spec(CONV_W - 1, D_RNN), pl.BlockSpec((nseg, D_RNN), lambda i: (0, 0)),
                 state_spec(keep, KV_W), state_spec(keep, KV_W))
    scratch = [
        pltpu.VMEM((nseg, SUBLANES, D_RNN), jnp.float32),
        pltpu.VMEM((nseg, SUBLANES, D_RNN), jnp.float32),
        pltpu.VMEM((nseg, HIST + seg_len, KV2_W), jnp.bfloat16),
        pltpu.VMEM((nseg, HIST + seg_len, N_KV * V_STRIDE), jnp.bfloat16),
        pltpu.VMEM((D_RNN // LANES, m_rows, LANES), jnp.float32),
        pltpu.VMEM((D_RNN // LANES, m_rows, LANES), jnp.float32),
        pltpu.VMEM((m_rows, Q_W), jnp.bfloat16),
        pltpu.VMEM((m_rows, D_MODEL), jnp.bfloat16),
        pltpu.VMEM((m_rows, P_W), jnp.float32),
        pltpu.VMEM((m_rows, Q_W), jnp.bfloat16),
        pltpu.VMEM((N_KV, GROUP * CHUNK, BAND), jnp.float32),
    ]
    return pl.pallas_call(
        functools.partial(_mixer_kernel, nseg, seg_len, mask_history, sparse_out),
        grid=(n_steps,),
        in_specs=in_specs,
        out_specs=out_specs,
        out_shape=out_shape,
        scratch_shapes=scratch,
        compiler_params=pltpu.CompilerParams(dimension_semantics=("arbitrary",), vmem_limit_bytes=VMEM_LIMIT),
        name="mixer_prompt" if mask_history else "mixer_sample",
    )(x, conv0, h0, k0, v0, *weights)


def _group_swiglu(x, cols, wg_ref, wu_ref, wd_ref):
    parts = []
    for e in range(EXPERTS_PER_GROUP):
        h = jax.nn.silu(_bdot(x, wg_ref[e])) * _bdot(x, wu_ref[e])
        parts.append((h * cols[e]).astype(jnp.bfloat16))
    return _bdot(jnp.concatenate(parts, axis=1), wd_ref[...])


def _moe_groups_kernel(xn2_ref, comb_ref, x1_ref, wg_ref, wu_ref, wd_ref, norm_ref, y_ref, acc_ref):
    g = pl.program_id(1)

    @pl.when(g == 0)
    def _():
        acc_ref[...] = jnp.zeros_like(acc_ref)

    comb = comb_ref[...]
    lane = lax.broadcasted_iota(jnp.int32, comb.shape, 1)
    cols = [jnp.sum(jnp.where(lane == g * EXPERTS_PER_GROUP + e, comb, 0.0), axis=-1, keepdims=True)
            for e in range(EXPERTS_PER_GROUP)]
    acc_ref[...] += _group_swiglu(xn2_ref[...], cols, wg_ref, wu_ref, wd_ref)

    @pl.when(g == N_GROUPS - 1)
    def _():
        y_ref[...] = _rmsnorm(x1_ref[...] + acc_ref[...], norm_ref[...])


def _group_weight_specs(group_of):
    return [pl.BlockSpec((EXPERTS_PER_GROUP, D_MODEL, D_EXPERT), lambda *a: (group_of(*a), 0, 0)),
            pl.BlockSpec((EXPERTS_PER_GROUP, D_MODEL, D_EXPERT), lambda *a: (group_of(*a), 0, 0)),
            pl.BlockSpec((None, GROUP_FF, D_MODEL), lambda *a: (group_of(*a), 0, 0))]


def _moe_groups(xn2, comb, x1, wg, wu, wd, norm_final, *, tile):
    n_tok = xn2.shape[0]
    row_spec = lambda w: pl.BlockSpec((tile, w), lambda i, g: (i, 0))
    return pl.pallas_call(
        _moe_groups_kernel,
        grid=(n_tok // tile, N_GROUPS),
        in_specs=[row_spec(D_MODEL), row_spec(ROUTER_W), row_spec(D_MODEL),
                  *_group_weight_specs(lambda i, g: g),
                  pl.BlockSpec((1, D_MODEL), lambda i, g: (0, 0))],
        out_specs=row_spec(D_MODEL),
        out_shape=jax.ShapeDtypeStruct((n_tok, D_MODEL), jnp.float32),
        scratch_shapes=[pltpu.VMEM((tile, D_MODEL), jnp.float32)],
        compiler_params=pltpu.CompilerParams(dimension_semantics=("arbitrary", "arbitrary"),
                                             vmem_limit_bytes=VMEM_LIMIT),
        name="moe_groups",
    )(xn2, comb, x1, wg, wu, wd, norm_final)


def _scatter_rows_kernel(tile, pos_ref, x_ref, rec_ref, wg_ref, wu_ref, wd_ref,
                         xs_ref, recs_ref, wg_out_ref, wu_out_ref, wd_out_ref):
    i = pl.program_id(0)

    @pl.when(i == 0)
    def _():
        xs_ref[...] = jnp.zeros_like(xs_ref)
        recs_ref[...] = jnp.zeros_like(recs_ref)

    wg_out_ref[...] = wg_ref[...].astype(jnp.bfloat16)
    wu_out_ref[...] = wu_ref[...].astype(jnp.bfloat16)
    wd_out_ref[...] = wd_ref[...].astype(jnp.bfloat16)

    def body(j, carry):
        r0 = pl.multiple_of(j * SUBLANES, SUBLANES)
        xb = x_ref[pl.ds(r0, SUBLANES), :]
        rb = rec_ref[pl.ds(r0, SUBLANES), :]
        for k in range(SUBLANES):
            p = pos_ref[i * tile + j * SUBLANES + k]
            xs_ref[pl.ds(p, 1), :] = xb[k:k + 1, :]
            recs_ref[pl.ds(p, 1), :] = rb[k:k + 1, :]
        return carry

    lax.fori_loop(0, tile // SUBLANES, body, 0)


def _scatter_rows(pos, xn2p, rec, w_gate, w_up, w_down, n_slots, *, tile):
    n_tok = xn2p.shape[0]
    n_steps = n_tok // tile
    per_step = N_EXPERTS // n_steps
    assert per_step * n_steps == N_EXPERTS
    w_spec = lambda a: pl.BlockSpec((per_step,) + a.shape[1:], lambda i, pos: (i, 0, 0))
    return pl.pallas_call(
        functools.partial(_scatter_rows_kernel, tile),
        grid_spec=pltpu.PrefetchScalarGridSpec(
            num_scalar_prefetch=1,
            grid=(n_steps,),
            in_specs=[pl.BlockSpec((tile, PACK_W), lambda i, pos: (i, 0)),
                      pl.BlockSpec((tile, ROUTER_W), lambda i, pos: (i, 0)),
                      w_spec(w_gate), w_spec(w_up), w_spec(w_down)],
            out_specs=[pl.BlockSpec((n_slots, PACK_W), lambda i, pos: (0, 0)),
                       pl.BlockSpec((n_slots, ROUTER_W), lambda i, pos: (0, 0)),
                       w_spec(w_gate), w_spec(w_up), w_spec(w_down)],
        ),
        out_shape=(jax.ShapeDtypeStruct((n_slots, PACK_W), jnp.uint32),
                   jax.ShapeDtypeStruct((n_slots, ROUTER_W), jnp.float32),
                   jax.ShapeDtypeStruct(w_gate.shape, jnp.bfloat16),
                   jax.ShapeDtypeStruct(w_up.shape, jnp.bfloat16),
                   jax.ShapeDtypeStruct(w_down.shape, jnp.bfloat16)),
        compiler_params=pltpu.CompilerParams(dimension_semantics=("arbitrary",), vmem_limit_bytes=VMEM_LIMIT),
        name="moe_scatter",
    )(pos, xn2p, rec, w_gate, w_up, w_down)


def _moe_sorted_kernel(tg_ref, xs_ref, recs_ref, wg_ref, wu_ref, wd_ref, o_ref):
    x = _unpack_bf16_pairs(xs_ref[...]).astype(jnp.bfloat16)
    rec = recs_ref[...]
    cols = [rec[:, e:e + 1] for e in range(EXPERTS_PER_GROUP)]
    o_ref[...] = _pack_bf16_pairs(_group_swiglu(x, cols, wg_ref, wu_ref, wd_ref))


def _moe_sorted(tile_group, xs, recs, wg, wu, wd, *, tile):
    n_slots = xs.shape[0]
    return pl.pallas_call(
        _moe_sorted_kernel,
        grid_spec=pltpu.PrefetchScalarGridSpec(
            num_scalar_prefetch=1,
            grid=(n_slots // tile,),
            in_specs=[pl.BlockSpec((tile, PACK_W), lambda i, tg: (i, 0)),
                      pl.BlockSpec((tile, ROUTER_W), lambda i, tg: (i, 0)),
                      *_group_weight_specs(lambda i, tg: tg[i])],
            out_specs=pl.BlockSpec((tile, PACK_W), lambda i, tg: (i, 0)),
        ),
        out_shape=jax.ShapeDtypeStruct((n_slots, PACK_W), jnp.uint32),
        compiler_params=pltpu.CompilerParams(dimension_semantics=("arbitrary",), vmem_limit_bytes=VMEM_LIMIT),
        name="moe_sorted",
    )(tile_group, xs, recs, wg, wu, wd)


def _gather_norm_kernel(tile, pos_ref, os_ref, x1_ref, norm_ref, y_ref, buf_ref):
    i = pl.program_id(0)

    def body(j, carry):
        r0 = pl.multiple_of(j * SUBLANES, SUBLANES)
        rows = [os_ref[pl.ds(pos_ref[i * tile + j * SUBLANES + k], 1), :] for k in range(SUBLANES)]
        buf_ref[pl.ds(r0, SUBLANES), :] = jnp.concatenate(rows, axis=0)
        return carry

    lax.fori_loop(0, tile // SUBLANES, body, 0, unroll=4)
    y_ref[...] = _rmsnorm(x1_ref[...] + _unpack_bf16_pairs(buf_ref[...]), norm_ref[...])


def _gather_norm(pos, o_sorted, x1, norm_final, *, tile):
    n_tok = x1.shape[0]
    n_slots = o_sorted.shape[0]
    return pl.pallas_call(
        functools.partial(_gather_norm_kernel, tile),
        grid_spec=pltpu.PrefetchScalarGridSpec(
            num_scalar_prefetch=1,
            grid=(n_tok // tile,),
            in_specs=[pl.BlockSpec((n_slots, PACK_W), lambda i, pos: (0, 0), pipeline_mode=pl.Buffered(1)),
                      pl.BlockSpec((tile, D_MODEL), lambda i, pos: (i, 0)),
                      pl.BlockSpec((1, D_MODEL), lambda i, pos: (0, 0))],
            out_specs=pl.BlockSpec((tile, D_MODEL), lambda i, pos: (i, 0)),
            scratch_shapes=[pltpu.VMEM((tile, PACK_W), jnp.uint32)],
        ),
        out_shape=jax.ShapeDtypeStruct((n_tok, D_MODEL), jnp.float32),
        compiler_params=pltpu.CompilerParams(dimension_semantics=("arbitrary",), vmem_limit_bytes=VMEM_LIMIT),
        name="moe_gather_norm",
    )(pos, o_sorted, x1, norm_final)


def _sorted_slots(rec, tile):
    n_tok = rec.shape[0]
    gid = rec[:, GID_LANE].astype(jnp.int32)
    onehot = (gid[:, None] == jnp.arange(N_GROUPS, dtype=jnp.int32)[None, :]).astype(jnp.int32)
    csum = jnp.cumsum(onehot, axis=0)
    rank = jnp.sum((csum - onehot) * onehot, axis=1)
    padded = ((csum[-1] + tile - 1) // tile) * tile
    end = jnp.cumsum(padded)
    pos = jnp.sum(onehot * (end - padded)[None, :], axis=1) + rank
    n_tiles = n_tok // tile + N_GROUPS
    tile_start = jnp.arange(n_tiles, dtype=jnp.int32) * tile
    tile_group = jnp.minimum(jnp.sum((tile_start[:, None] >= end[None, :]).astype(jnp.int32), axis=1), N_GROUPS - 1)
    return pos.astype(jnp.int32), tile_group.astype(jnp.int32), n_tiles * tile


def _rel_bucket(rel):
    nb = REL_BUCKETS // 2
    n = -rel
    ret = jnp.where(n < 0, nb, 0)
    n = jnp.abs(n)
    max_exact = nb // 2
    nf = jnp.maximum(n, 1).astype(jnp.float32)
    large = max_exact + (jnp.log(nf / max_exact) / math.log(REL_MAX_DIST / max_exact) * (nb - max_exact)).astype(jnp.int32)
    large = jnp.minimum(large, nb - 1)
    return ret + jnp.where(n < max_exact, n, large)


def _bias_by_rel(rel_table):
    rels = jnp.arange(-(BAND - 1), CHUNK + 1, dtype=jnp.int32)
    return jnp.transpose(rel_table[_rel_bucket(rels)]).astype(jnp.float32)


def _dup_heads(a):
    a = a.reshape(a.shape[:-1] + (N_KV, HEAD_DIM))
    return jnp.concatenate([a, a], axis=-1).reshape(a.shape[:-2] + (KV2_W,))


def _block_diag_groups(w):
    per = RG_GROUP // LRU_BLOCK_W
    blk = w.reshape(N_RG_GROUPS, per, LRU_BLOCK_W, 1, LRU_BLOCK_W)
    eye = jnp.eye(per, dtype=w.dtype).reshape(1, per, 1, per, 1)
    return (blk * eye).reshape(N_RG_GROUPS, RG_GROUP, RG_GROUP)


def kernel(x_prompt, x_sample, state_lru_h, state_lru_conv, cache_swa_k, cache_swa_v, norm_mix, w_in, b_merge, conv_w, conv_b, w_rg_a, b_rg_a, w_rg_x, b_rg_x, lru_lambda, attn_sink, rel_bias, w_lru_proj, w_attn_proj, w_out, norm_ffn, w_group, b_group, w_router, b_router, w_e_gate, w_e_up, w_e_down, norm_final):
    f32, bf16 = jnp.float32, jnp.bfloat16
    l = 0
    col_scale = jnp.where(jnp.arange(w_in.shape[-1]) >= C_G, 0.5, 1.0).astype(f32)
    w = w_in[l] * col_scale[None, :]
    w_rg = (0.5 * jnp.concatenate([_block_diag_groups(w_rg_a[l]), _block_diag_groups(w_rg_x[l])], axis=2)).astype(bf16)
    w_rt = jnp.concatenate([w_router[l], w_group[l],
                            jnp.zeros((D_MODEL, ROUTER_W - N_EXPERTS - N_GROUPS), f32)], axis=1).astype(bf16)
    b_rt = jnp.concatenate([b_router[l], b_group[l], jnp.zeros((ROUTER_W - N_EXPERTS - N_GROUPS,), f32)])[None, :]
    weights = (
        norm_mix[l][None, :], w.astype(bf16), b_merge[l][None, :], conv_w[l], conv_b[l][None, :], w_rg,
        b_rg_a[l][None, :], b_rg_x[l][None, :], lru_lambda[l][None, :], attn_sink[l], _bias_by_rel(rel_bias) * LOG2E,
        (0.5 * w_lru_proj[l]).astype(bf16), (0.5 * w_attn_proj[l]).astype(bf16), w_out[l].astype(bf16),
        norm_ffn[l][None, :],
        w_rt, b_rt,
    )
    nf = norm_final[None, :]

    bp, tp, _ = x_prompt.shape
    assert bp == 1
    xp = x_prompt.reshape(tp, D_MODEL)
    zeros = lambda *s: jnp.zeros(s, f32)
    p_x1, p_xn2, p_comb, p_conv, p_h, p_k, p_v = _mixer(
        xp, zeros(1, SUBLANES, D_RNN), zeros(1, SUBLANES, D_RNN),
        jnp.zeros((1, HIST, KV2_W), bf16), jnp.zeros((1, HIST, KV2_W), bf16), weights,
        nseg=1, seg_len=PROMPT_TILE, mask_history=True, sparse_out=True)
    pos, tile_group, n_slots = _sorted_slots(p_comb, SORT_TILE)
    x_sorted, rec_sorted, wg, wu, wd = _scatter_rows(pos, p_xn2, p_comb, w_e_gate[l], w_e_up[l], w_e_down[l],
                                                     n_slots, tile=SORT_TILE)
    wd = wd.reshape(N_GROUPS, GROUP_FF, D_MODEL)
    o_sorted = _moe_sorted(tile_group, x_sorted, rec_sorted, wg, wu, wd, tile=SORT_TILE)
    y_p = _gather_norm(pos, o_sorted, p_x1, nf, tile=GATHER_TILE).reshape(x_prompt.shape)

    bs, ts, _ = x_sample.shape
    xs = x_sample.reshape(bs * ts, D_MODEL)
    conv0 = jnp.pad(state_lru_conv[l], ((0, 0), (SUBLANES - (CONV_W - 1), 0), (0, 0)))
    h0 = jnp.broadcast_to(state_lru_h[l][:, None, :], (bs, SUBLANES, D_RNN))
    ck = cache_swa_k[l].reshape(bs, -1, KV_W)
    cv = cache_swa_v[l].reshape(bs, -1, KV_W)
    s_x1, s_xn2, s_comb, s_conv, s_h, s_k, s_v = _mixer(
        xs, conv0, h0, _dup_heads(ck).astype(bf16), _dup_heads(cv).astype(bf16), weights,
        nseg=bs, seg_len=ts, mask_history=False, sparse_out=False)
    y_s = _moe_groups(s_xn2, s_comb, s_x1, wg, wu, wd, nf, tile=bs * ts).reshape(x_sample.shape)

    p_lru_h = p_h[None]
    p_lru_conv = p_conv[None]
    heads = lambda a: a.reshape(a.shape[:-1] + (N_KV, HEAD_DIM))
    p_swa_k = heads(p_k)[None]
    p_swa_v = heads(p_v)[None]
    s_lru_h = s_h[None]
    s_lru_conv = s_conv[None]
    win = ck.shape[1]
    s_swa_k = jnp.concatenate([cache_swa_k[l][:, ts:], heads(s_k)], axis=1)[:, -win:][None]
    s_swa_v = jnp.concatenate([cache_swa_v[l][:, ts:], heads(s_v)], axis=1)[:, -win:][None]
    return (y_p, y_s, p_lru_h, p_lru_conv, p_swa_k, p_swa_v, s_lru_h, s_lru_conv, s_swa_k, s_swa_v)
```
